```python
import jax, jax.numpy as jnp
from jax import lax
import numpy as np

D_MODEL = 1024
BATCH = 8
SEQ = 8192
DEPTH = 2
DEC_BATCH = 32
DEC_SEQ = 16
PAST_LEN = 1024

CHUNK = 64
N_AB = (DEPTH + 1) // 2
N_SB = DEPTH // 2
N_MOD = 6
EPS = 1e-6

GLA_HEADS = 4
GLA_V = D_MODEL // 2
GLA_QK = GLA_V // 2
GLA_DK = GLA_QK // GLA_HEADS
GLA_DV = GLA_V // GLA_HEADS
GLA_GATE_RANK = 16
GLA_GATE_TAU = 16.0
GLA_BLOCK = CHUNK // 4

LRU_WIDTH = D_MODEL // 2
LRU_BLOCKS = 8
LRU_BLOCK_DIM = LRU_WIDTH // LRU_BLOCKS
LRU_C = 8.0
CONV_WIDTH = 4

AB_IN = 2 * GLA_QK + 2 * GLA_V + GLA_GATE_RANK + 2 * LRU_WIDTH
AB_MIX = GLA_V + LRU_WIDTH

SB_HEADS = 16
SB_DH = D_MODEL // SB_HEADS
SB_WIDTH = SB_HEADS * SB_DH
SB_BLOCK = 128

N_EXPERTS = 16
N_GROUPS = 4
EXPERTS_PER_GROUP = N_EXPERTS // N_GROUPS
TOP_K = 2
D_EXPERT = D_MODEL // 2
MOE_BLOCK = 128

kernel_name = 'hybrid_gla_rglru_stickbreaking_moe_stream_step'


def _rmsnorm(x, g):
    xf = x.astype(jnp.float32)
    y = xf * lax.rsqrt(jnp.mean(xf * xf, axis=-1, keepdims=True) + EPS)
    return (y * g.astype(jnp.float32)).astype(x.dtype)


def _modulate(x, shift, scale):
    return x * (1 + scale[:, None, :]) + shift[:, None, :]


def _gla_scan(q, k, v, log_a, s0):
    B, T = q.shape[:2]
    L = GLA_BLOCK
    nb = -(-T // L)
    pad = nb * L - T
    padf = lambda z: jnp.pad(z, ((0, 0), (0, pad), (0, 0), (0, 0)))
    blk = lambda z: padf(z).reshape(B, nb, L, *z.shape[2:]).swapaxes(0, 1)
    q, k, v, log_a = blk(q), blk(k), blk(v), blk(log_a)
    b = jnp.cumsum(log_a, axis=2)
    b_last = b[:, :, -1]
    q_e = q * jnp.exp(b)
    k_e = k * jnp.exp(-b)
    k_tail = k * jnp.exp(b_last[:, :, None] - b)
    causal = jnp.tril(jnp.ones((L, L), dtype=bool))

    def step(S, inp):
        qe, ke, kt, vv, bl = inp
        att = jnp.where(causal, jnp.einsum('blhd,bmhd->bhlm', qe, ke), 0.0)
        o = jnp.einsum('bhlm,bmhv->blhv', att, vv) + jnp.einsum('blhd,bhdv->blhv', qe, S)
        S = S * jnp.exp(bl)[..., None] + jnp.einsum('blhd,blhv->bhdv', kt, vv)
        return S, o

    S, o = lax.scan(step, s0, (q_e, k_e, k_tail, v, b_last))
    o = o.swapaxes(0, 1).reshape(B, nb * L, GLA_HEADS, GLA_DV)[:, :T]
    return o, S


def _lin_combine(left, right):
    a_l, b_l = left
    a_r, b_r = right
    return a_l * a_r, a_r * b_l + b_r


def _ab_mixer(h, pos, s_gla, h_lru, conv_buf, w_in, w_gk2, b_gk, gla_norm, conv_w, conv_b,
              lru_wa, lru_ba, lru_wi, lru_bi, lru_lam, w_out):
    f32 = jnp.float32
    B, T, _ = h.shape
    proj = jnp.einsum('btd,de->bte', h, w_in).astype(f32)
    c1 = GLA_QK
    c2 = c1 + GLA_QK
    c3 = c2 + GLA_V
    c4 = c3 + GLA_V
    c5 = c4 + GLA_GATE_RANK
    c6 = c5 + LRU_WIDTH
    q, k, v, g, gk, xr, xg = jnp.split(proj, [c1, c2, c3, c4, c5, c6], axis=-1)

    q = q.reshape(B, T, GLA_HEADS, GLA_DK) * (GLA_DK ** -0.5)
    k = k.reshape(B, T, GLA_HEADS, GLA_DK)
    v = v.reshape(B, T, GLA_HEADS, GLA_DV)
    log_a = jax.nn.log_sigmoid(gk @ w_gk2.astype(f32) + b_gk.astype(f32))
    log_a = log_a.reshape(B, T, GLA_HEADS, GLA_DK) / GLA_GATE_TAU
    o, s_gla_new = _gla_scan(q, k, v, log_a, s_gla.astype(f32))
    o = _rmsnorm(o, gla_norm) * jax.nn.silu(g.reshape(B, T, GLA_HEADS, GLA_DV))
    y_a = o.reshape(B, T, GLA_V)

    xp = jnp.concatenate([conv_buf.astype(f32), xr], axis=1)
    cw = conv_w.astype(f32)
    xc = conv_b.astype(f32) + sum(cw[i] * xp[:, i:i + T] for i in range(CONV_WIDTH))
    new_buf = xp[:, -(CONV_WIDTH - 1):]
    xb = xc.reshape(B, T, LRU_BLOCKS, LRU_BLOCK_DIM)
    r = jax.nn.sigmoid(jnp.einsum('btni,nio->btno', xb, lru_wa.astype(f32)).reshape(B, T, LRU_WIDTH) + lru_ba.astype(f32))
    i_g = jax.nn.sigmoid(jnp.einsum('btni,nio->btno', xb, lru_wi.astype(f32)).reshape(B, T, LRU_WIDTH) + lru_bi.astype(f32))
    log_at = -LRU_C * r * jax.nn.softplus(-lru_lam.astype(f32))
    a = jnp.exp(log_at)
    mult = jnp.where((pos == 0)[None, :, None], 1.0, jnp.sqrt(-jnp.expm1(2.0 * log_at)))
    bterm = mult * (i_g * xc)
    bterm = bterm.at[:, 0].add(a[:, 0] * h_lru.astype(f32))
    _, hs = lax.associative_scan(_lin_combine, (a, bterm), axis=1)
    y_b = hs * jax.nn.gelu(xg)

    mix = jnp.concatenate([y_a, y_b], axis=-1).astype(h.dtype)
    return mix @ w_out, (s_gla_new, hs[:, -1], new_buf)


def _sb_attend(q, k, v, q_pos, k_pos, acc, suffix):
    z = jnp.einsum('bhqd,bhkd->bhqk', q, k)
    m = k_pos[None, :] < q_pos[:, None]
    log_keep = jnp.where(m, jax.nn.log_sigmoid(-z), 0.0)
    csum = jnp.cumsum(log_keep, axis=-1)
    tot = csum[..., -1]
    log_w = jax.nn.log_sigmoid(z) + (tot[..., None] - csum) + suffix[..., None]
    w = jnp.where(m, jnp.exp(log_w), 0.0)
    return acc + jnp.einsum('bhqk,bhkd->bhqd', w, v), suffix + tot


def _sb_prompt(q, k, v):
    B, H, S, Dh = q.shape
    n_blocks = S // SB_BLOCK
    offs = jnp.arange(SB_BLOCK)

    def query_block(qi):
        q0 = qi * SB_BLOCK
        qb = lax.dynamic_slice_in_dim(q, q0, SB_BLOCK, axis=2)
        q_pos = q0 + offs

        def key_block(j, carry):
            k0 = (qi - j) * SB_BLOCK
            kb = lax.dynamic_slice_in_dim(k, k0, SB_BLOCK, axis=2)
            vb = lax.dynamic_slice_in_dim(v, k0, SB_BLOCK, axis=2)
            return _sb_attend(qb, kb, vb, q_pos, k0 + offs, carry[0], carry[1])

        init = (jnp.zeros((B, H, SB_BLOCK, Dh), jnp.float32), jnp.zeros((B, H, SB_BLOCK), jnp.float32))
        acc, _ = lax.fori_loop(0, qi + 1, key_block, init)
        return acc

    out = lax.map(query_block, jnp.arange(n_blocks))
    return out.transpose(1, 0, 3, 2, 4).reshape(B, S, H, Dh)


def _sb_mixer(h, w_qkv, w_o, past_k, past_v):
    f32 = jnp.float32
    B, T, _ = h.shape
    qkv = jnp.einsum('btd,de->bte', h, w_qkv).reshape(B, T, 3, SB_HEADS, SB_DH)
    q, k, v = qkv[:, :, 0], qkv[:, :, 1], qkv[:, :, 2]
    qf = (q.astype(f32) * (SB_DH ** -0.5)).transpose(0, 2, 1, 3)
    if past_k is None:
        o = _sb_prompt(qf, k.astype(f32).transpose(0, 2, 1, 3), v.astype(f32).transpose(0, 2, 1, 3))
    else:
        P = past_k.shape[1]
        k_all = jnp.concatenate([past_k.astype(f32), k.astype(f32)], axis=1).transpose(0, 2, 1, 3)
        v_all = jnp.concatenate([past_v.astype(f32), v.astype(f32)], axis=1).transpose(0, 2, 1, 3)
        acc0 = jnp.zeros((B, SB_HEADS, T, SB_DH), f32)
        suf0 = jnp.zeros((B, SB_HEADS, T), f32)
        acc, _ = _sb_attend(qf, k_all, v_all, P + jnp.arange(T), jnp.arange(P + T), acc0, suf0)
        o = acc.transpose(0, 2, 1, 3)
    y = o.reshape(B, T, SB_WIDTH).astype(h.dtype) @ w_o
    return y, k, v


def _route(x2, w_router, b_router):
    N = x2.shape[0]
    s = jax.nn.sigmoid((x2 @ w_router).astype(jnp.float32))
    sel = (s + b_router.astype(jnp.float32)).reshape(N, N_GROUPS, EXPERTS_PER_GROUP)
    group_score = lax.top_k(sel, 2)[0].sum(-1)
    best = jnp.argmax(group_score, axis=-1)
    in_group = sel[jnp.arange(N), best]
    _, local = lax.top_k(in_group, TOP_K)
    idx = best[:, None] * EXPERTS_PER_GROUP + local
    w = jnp.take_along_axis(s, idx, axis=-1)
    return idx, w / jnp.sum(w, axis=-1, keepdims=True)


def _moe(x, w_router, b_router, w1, w3, w2):
    B, T, D = x.shape
    x2 = x.reshape(-1, D)
    N = x2.shape[0]
    idx, gw = _route(x2, w_router, b_router)
    M = N * TOP_K
    flat_e = idx.reshape(-1)
    order = jnp.argsort(flat_e)
    e_sorted = flat_e[order]
    tok = order // TOP_K
    counts = jnp.bincount(flat_e, length=N_EXPERTS)
    padded = (counts + MOE_BLOCK - 1) // MOE_BLOCK * MOE_BLOCK
    pad_end = jnp.cumsum(padded)
    pad_start = pad_end - padded
    start = jnp.cumsum(counts) - counts
    dest = pad_start[e_sorted] + jnp.arange(M) - start[e_sorted]
    n_blocks = -(-M // MOE_BLOCK) + N_EXPERTS
    xs = jnp.zeros((n_blocks * MOE_BLOCK, D), x.dtype).at[dest].set(x2[tok])
    blk_e = jnp.minimum(jnp.searchsorted(pad_end, jnp.arange(n_blocks) * MOE_BLOCK, side='right'), N_EXPERTS - 1)

    def expert_block(args):
        xb, e = args
        return (jax.nn.silu(xb @ w1[e]) * (xb @ w3[e])) @ w2[e]

    ys = lax.map(expert_block, (xs.reshape(n_blocks, MOE_BLOCK, D), blk_e)).reshape(-1, D)
    contrib = ys[dest].astype(jnp.float32) * gw.reshape(-1)[order][:, None]
    out = jnp.zeros((N, D), jnp.float32).at[tok].add(contrib)
    return out.reshape(B, T, D).astype(x.dtype)


def _trunk(x, c, p, past):
    f32 = jnp.float32
    B, T, _ = x.shape
    pos = (0 if past is None else PAST_LEN) + jnp.arange(T)
    cond = jax.nn.silu(c.astype(f32))
    gla_s, lru_s, conv_s, ks, vs = [], [], [], [], []
    for l in range(DEPTH):
        mod = (cond @ p['w_ada'][l].astype(f32) + p['b_ada'][l].astype(f32)).reshape(B, N_MOD, D_MODEL).astype(x.dtype)
        sh1, sc1, g1, sh2, sc2, g2 = [mod[:, i] for i in range(N_MOD)]
        h = _modulate(_rmsnorm(x, p['norm_mix'][l]), sh1, sc1)
        j = l // 2
        if l % 2 == 0:
            if past is None:
                s0 = jnp.zeros((B, GLA_HEADS, GLA_DK, GLA_DV), f32)
                h0 = jnp.zeros((B, LRU_WIDTH), f32)
                buf = jnp.zeros((B, CONV_WIDTH - 1, LRU_WIDTH), f32)
            else:
                s0, h0, buf = past['state_gla'][j], past['state_lru'][j], past['state_conv'][j]
            out, (sg, sl, sc) = _ab_mixer(h, pos, s0, h0, buf, p['w_ab_in'][j], p['w_gk2'][j], p['b_gk'][j],
                                          p['gla_norm'][j], p['conv_w'][j], p['conv_b'][j], p['lru_wa'][j],
                                          p['lru_ba'][j], p['lru_wi'][j], p['lru_bi'][j], p['lru_lam'][j],
                                          p['w_ab_out'][j])
            gla_s.append(sg)
            lru_s.append(sl)
            conv_s.append(sc)
        else:
            pk = None if past is None else past['cache_k'][j]
            pv = None if past is None else past['cache_v'][j]
            out, k_new, v_new = _sb_mixer(h, p['w_sb_qkv'][j], p['w_sb_out'][j], pk, pv)
            ks.append(k_new)
            vs.append(v_new)
        x = x + g1[:, None] * out
        h = _modulate(_rmsnorm(x, p['norm_ffn'][l]), sh2, sc2)
        x = x + g2[:, None] * _moe(h, p['w_router'], p['b_router'], p['w_e1'][l], p['w_e3'][l], p['w_e2'][l])
    y = _rmsnorm(x, p['norm_out'])
    return y, (jnp.stack(gla_s), jnp.stack(lru_s), jnp.stack(conv_s), jnp.stack(ks), jnp.stack(vs))


def setup_inputs(seed: int = 0) -> dict:
    key = jax.random.key(seed)
    keys = iter(jax.random.split(key, 40))
    f32 = jnp.float32
    nrm = lambda shape, scale: jax.random.normal(next(keys), shape, f32) * scale
    gain = lambda shape: 1.0 + nrm(shape, 0.05)
    a8 = jax.random.uniform(next(keys), (N_AB, LRU_WIDTH), f32, minval=0.9, maxval=0.999)
    a_base = a8 ** (1.0 / LRU_C)
    return {
        'x_prompt': nrm((BATCH, SEQ, D_MODEL), 1.0),
        'x_sample': nrm((DEC_BATCH, DEC_SEQ, D_MODEL), 1.0),
        'state_gla': nrm((N_AB, DEC_BATCH, GLA_HEADS, GLA_DK, GLA_DV), 0.5),
        'state_lru': nrm((N_AB, DEC_BATCH, LRU_WIDTH), 0.5),
        'state_conv': nrm((N_AB, DEC_BATCH, CONV_WIDTH - 1, LRU_WIDTH), 1.0),
        'cache_k': nrm((N_SB, DEC_BATCH, PAST_LEN, SB_HEADS, SB_DH), 1.0),
        'cache_v': nrm((N_SB, DEC_BATCH, PAST_LEN, SB_HEADS, SB_DH), 1.0),
        'c_prompt': nrm((BATCH, D_MODEL), 1.0),
        'c_sample': nrm((DEC_BATCH, D_MODEL), 1.0),
        'w_ada': nrm((DEPTH, D_MODEL, N_MOD * D_MODEL), 0.3 * D_MODEL ** -0.5),
        'b_ada': nrm((DEPTH, N_MOD * D_MODEL), 0.02),
        'norm_mix': gain((DEPTH, D_MODEL)),
        'norm_ffn': gain((DEPTH, D_MODEL)),
        'norm_out': gain((D_MODEL,)),
        'w_ab_in': nrm((N_AB, D_MODEL, AB_IN), D_MODEL ** -0.5),
        'w_gk2': nrm((N_AB, GLA_GATE_RANK, GLA_QK), GLA_GATE_RANK ** -0.5),
        'b_gk': nrm((N_AB, GLA_QK), 0.1),
        'gla_norm': gain((N_AB, GLA_DV)),
        'conv_w': nrm((N_AB, CONV_WIDTH, LRU_WIDTH), CONV_WIDTH ** -0.5),
        'conv_b': nrm((N_AB, LRU_WIDTH), 0.02),
        'lru_wa': nrm((N_AB, LRU_BLOCKS, LRU_BLOCK_DIM, LRU_BLOCK_DIM), LRU_BLOCK_DIM ** -0.5),
        'lru_ba': nrm((N_AB, LRU_WIDTH), 0.02),
        'lru_wi': nrm((N_AB, LRU_BLOCKS, LRU_BLOCK_DIM, LRU_BLOCK_DIM), LRU_BLOCK_DIM ** -0.5),
        'lru_bi': nrm((N_AB, LRU_WIDTH), 0.02),
        'lru_lam': jnp.log(a_base) - jnp.log1p(-a_base),
        'w_ab_out': nrm((N_AB, AB_MIX, D_MODEL), AB_MIX ** -0.5),
        'w_sb_qkv': nrm((N_SB, D_MODEL, 3 * SB_WIDTH), D_MODEL ** -0.5),
        'w_sb_out': nrm((N_SB, SB_WIDTH, D_MODEL), SB_WIDTH ** -0.5),
        'w_router': nrm((D_MODEL, N_EXPERTS), D_MODEL ** -0.5),
        'b_router': nrm((N_EXPERTS,), 0.01),
        'w_e1': nrm((DEPTH, N_EXPERTS, D_MODEL, D_EXPERT), D_MODEL ** -0.5),
        'w_e3': nrm((DEPTH, N_EXPERTS, D_MODEL, D_EXPERT), D_MODEL ** -0.5),
        'w_e2': nrm((DEPTH, N_EXPERTS, D_EXPERT, D_MODEL), D_EXPERT ** -0.5),
    }


def reference(x_prompt, x_sample, state_gla, state_lru, state_conv, cache_k, cache_v, c_prompt, c_sample,
              w_ada, b_ada, norm_mix, norm_ffn, norm_out, w_ab_in, w_gk2, b_gk, gla_norm, conv_w, conv_b,
              lru_wa, lru_ba, lru_wi, lru_bi, lru_lam, w_ab_out, w_sb_qkv, w_sb_out, w_router, b_router,
              w_e1, w_e3, w_e2):
    p = dict(w_ada=w_ada, b_ada=b_ada, norm_mix=norm_mix, norm_ffn=norm_ffn, norm_out=norm_out,
             w_ab_in=w_ab_in, w_gk2=w_gk2, b_gk=b_gk, gla_norm=gla_norm, conv_w=conv_w, conv_b=conv_b,
             lru_wa=lru_wa, lru_ba=lru_ba, lru_wi=lru_wi, lru_bi=lru_bi, lru_lam=lru_lam, w_ab_out=w_ab_out,
             w_sb_qkv=w_sb_qkv, w_sb_out=w_sb_out, w_router=w_router, b_router=b_router,
             w_e1=w_e1, w_e3=w_e3, w_e2=w_e2)
    past = dict(state_gla=state_gla, state_lru=state_lru, state_conv=state_conv, cache_k=cache_k, cache_v=cache_v)
    y_prompt, (p_gla, p_lru, p_conv, p_k, p_v) = _trunk(x_prompt, c_prompt, p, None)
    y_sample, (s_gla, s_lru, s_conv, s_k, s_v) = _trunk(x_sample, c_sample, p, past)
    return (y_prompt, y_sample, p_gla, p_lru, p_conv, p_k, p_v, s_gla, s_lru, s_conv, s_k, s_v)
```

```python
import functools

import jax
import jax.numpy as jnp
from jax import lax
from jax.experimental import pallas as pl
from jax.experimental.pallas import tpu as pltpu

F32 = jnp.float32
BF16 = jnp.bfloat16
I32 = jnp.int32

EPS = 1e-6
N_MOD = 6
GLA_HEADS = 4
GLA_GATE_TAU = 16.0
LRU_C = 8.0
CONV_WIDTH = 4
N_GROUPS = 4
EXPERTS_PER_GROUP = 4
N_PAIRS = 6
N_CLASSES = N_GROUPS * N_PAIRS

LANES = 128
SUBLANES = 8
VMEM_LIMIT = 56 * 1024 * 1024

ROW_TILE = 512
MIX_TILE = 256
GLA_CHUNK = 64
SB_BLOCK = 256
MOE_BLOCK = 256
GATHER_TILE = 256


def _cparams(*sem):
    return pltpu.CompilerParams(dimension_semantics=sem, vmem_limit_bytes=VMEM_LIMIT)


def _log_sigmoid(z):
    return jnp.minimum(z, 0.0) - jnp.log1p(jnp.exp(-jnp.abs(z)))


def _softplus(z):
    return jnp.maximum(z, 0.0) + jnp.log1p(jnp.exp(-jnp.abs(z)))


def _sigmoid(z):
    return 1.0 / (1.0 + jnp.exp(-z))


def _rms_mod(x, gain, shift, scale):
    ms = jnp.mean(x * x, axis=-1, keepdims=True)
    y = x * lax.rsqrt(ms + EPS) * gain
    return y * (1.0 + scale) + shift


def _row_tile(n_rows, t_len, target):
    if t_len % target == 0:
        return target
    return n_rows


def _mod_operand(mod, n_rows, t_len, tile):
    d = mod.shape[-1]
    if t_len % tile == 0:
        per_seq = t_len // tile
        return mod[:, None, :], (lambda nidx: pl.BlockSpec((None, 1, d), lambda *i: (i[nidx] // per_seq, 0, 0)))
    rows = jnp.repeat(mod, t_len, axis=0)
    return rows, (lambda nidx: pl.BlockSpec((tile, d), lambda *i: (i[nidx], 0)))


def _ada_kernel(c_ref, w_ref, b_ref, o_ref):
    c = c_ref[...]
    cond = c * _sigmoid(c)
    o_ref[...] = jnp.dot(cond.astype(BF16), w_ref[...].astype(BF16), preferred_element_type=F32) + b_ref[...]


def _ada(c, w_ada, b_ada):
    depth, d, e = w_ada.shape
    b = c.shape[0]
    tn = d
    return pl.pallas_call(
        _ada_kernel,
        grid=(depth, e // tn),
        in_specs=[pl.BlockSpec((b, d), lambda l, j: (0, 0)),
                  pl.BlockSpec((None, d, tn), lambda l, j: (l, 0, j)),
                  pl.BlockSpec((None, 1, tn), lambda l, j: (l, 0, j))],
        out_specs=pl.BlockSpec((None, b, tn), lambda l, j: (l, 0, j)),
        out_shape=jax.ShapeDtypeStruct((depth, b, e), F32),
        compiler_params=_cparams("parallel", "parallel"),
        name="ada_mod",
    )(c, w_ada, b_ada[:, None, :])


def _norm_proj_kernel(out_plan, x_ref, gain_ref, sh_ref, sc_ref, *refs):
    n_w = len(out_plan)
    w_refs, o_refs = refs[:n_w], refs[n_w:]
    h = _rms_mod(x_ref[...], gain_ref[...], sh_ref[...], sc_ref[...]).astype(BF16)
    k = 0
    for w_ref, (scale, dtypes) in zip(w_refs, out_plan):
        y = jnp.dot(h, w_ref[...], preferred_element_type=F32)
        if scale != 1.0:
            y = y * scale
        for dt in dtypes:
            o_refs[k][...] = y.astype(dt)
            k += 1


def _norm_proj(x, t_len, gain, shift, scale, pieces):
    n, d = x.shape
    tm = _row_tile(n, t_len, ROW_TILE)
    sh_arr, sh_spec = _mod_operand(shift, n, t_len, tm)
    sc_arr, sc_spec = _mod_operand(scale, n, t_len, tm)
    in_specs = [pl.BlockSpec((tm, d), lambda i: (i, 0)),
                pl.BlockSpec((1, d), lambda i: (0, 0)),
                sh_spec(0), sc_spec(0)]
    out_specs, out_shapes, plan, weights = [], [], [], []
    for w, s, dtypes in pieces:
        e = w.shape[1]
        in_specs.append(pl.BlockSpec((d, e), lambda i: (0, 0)))
        weights.append(w)
        plan.append((s, tuple(dtypes)))
        for dt in dtypes:
            out_specs.append(pl.BlockSpec((tm, e), lambda i: (i, 0)))
            out_shapes.append(jax.ShapeDtypeStruct((n, e), dt))
    return pl.pallas_call(
        functools.partial(_norm_proj_kernel, tuple(plan)),
        grid=(n // tm,),
        in_specs=in_specs, out_specs=out_specs, out_shape=out_shapes,
        compiler_params=_cparams("parallel"),
        name="norm_proj",
    )(x, gain[None, :], sh_arr, sc_arr, *weights)


def _shift_rows(x, s, fill):
    rows = lax.broadcasted_iota(I32, x.shape, 0)
    return jnp.where(rows >= s, pltpu.roll(x, s, axis=0), fill)


def _cumsum_rows(x):
    n = x.shape[0]
    s = 1
    while s < n:
        x = x + _shift_rows(x, s, 0.0)
        s *= 2
    return x


def _linear_scan_rows(a, b):
    n = a.shape[0]
    s = 1
    while s < n:
        b = a * _shift_rows(b, s, 0.0) + b
        a = a * _shift_rows(a, s, 1.0)
        s *= 2
    return a, b


def _gelu_tanh(x):
    return 0.5 * x * (1.0 + jnp.tanh(0.7978845608028654 * (x + 0.044715 * (x * x * x))))


def _ab_mixer_kernel(first_pos_zero, tt, chunk,
                     qk_ref, v_ref, g_ref, gk_ref, xr_ref, xg_ref, s0_ref, h0_ref, buf0_ref,
                     wgk_ref, bgk_ref, gnorm_ref, cw_ref, cb_ref, wa_ref, wi_ref, ba_ref, bi_ref, lam_ref,
                     mix_ref, s_out_ref, h_out_ref, buf_out_ref,
                     st_ref, hc_ref, xpad_ref):
    ti = pl.program_id(1)
    n_t = pl.num_programs(1)
    dqk = qk_ref.shape[1] // 2
    dv = v_ref.shape[1]
    hk = dqk // GLA_HEADS
    hv = dv // GLA_HEADS
    w_lru = xr_ref.shape[1]

    @pl.when(ti == 0)
    def _():
        st_ref[...] = s0_ref[...]
        hc_ref[...] = h0_ref[...]
        xpad_ref[0:SUBLANES, :] = buf0_ref[...]

    xpad_ref[SUBLANES:SUBLANES + tt, :] = xr_ref[...]
    xc = cb_ref[...]
    for i in range(CONV_WIDTH):
        off = SUBLANES - (CONV_WIDTH - 1) + i
        xc = xc + cw_ref[i:i + 1, :] * xpad_ref[off:off + tt, :]
    tail = xpad_ref[tt:tt + SUBLANES, :]
    xpad_ref[0:SUBLANES, :] = tail
    xc_b = xc.astype(BF16)
    half = w_lru // 2
    r_lin = jnp.concatenate([jnp.dot(xc_b[:, j * half:(j + 1) * half], wa_ref[j], preferred_element_type=F32)
                             for j in range(2)], axis=-1)
    i_lin = jnp.concatenate([jnp.dot(xc_b[:, j * half:(j + 1) * half], wi_ref[j], preferred_element_type=F32)
                             for j in range(2)], axis=-1)
    r = _sigmoid(r_lin + ba_ref[...])
    i_g = _sigmoid(i_lin + bi_ref[...])
    log_at = (-LRU_C) * r * _softplus(-lam_ref[...])
    a = jnp.exp(log_at)
    mult = jnp.sqrt(1.0 - a * a)
    if first_pos_zero:
        rows = lax.broadcasted_iota(I32, mult.shape, 0)
        mult = jnp.where((rows == 0) & (ti == 0), 1.0, mult)
    bterm = mult * (i_g * xc)
    a_cum, h_loc = _linear_scan_rows(a, bterm)
    hs = a_cum * hc_ref[...] + h_loc
    hc_ref[...] = hs[tt - 1:tt, :]
    y_b = hs * _gelu_tanh(xg_ref[...])
    mix_ref[:, dv:] = y_b.astype(BF16)

    lane_k = lax.broadcasted_iota(I32, (1, dqk), 1)
    st_rows = lax.broadcasted_iota(I32, (dv, dqk), 0)
    st_cols = lax.broadcasted_iota(I32, (dv, dqk), 1)
    st_mask = functools.reduce(
        jnp.logical_or,
        [(st_rows >= h * hv) & (st_rows < (h + 1) * hv) & (st_cols >= h * hk) & (st_cols < (h + 1) * hk)
         for h in range(GLA_HEADS)])
    crow = lax.broadcasted_iota(I32, (chunk, chunk), 0)
    ccol = lax.broadcasted_iota(I32, (chunk, chunk), 1)
    causal = ccol <= crow
    mid = chunk // 2 - 1
    for c in range(tt // chunk):
        rs = slice(c * chunk, (c + 1) * chunk)
        q = qk_ref[rs, 0:dqk] * (hk ** -0.5)
        k = qk_ref[rs, dqk:2 * dqk]
        v_b = v_ref[rs, :].astype(BF16)
        u = jnp.dot(gk_ref[rs, :].astype(BF16), wgk_ref[...], preferred_element_type=F32) + bgk_ref[...]
        b = _cumsum_rows(_log_sigmoid(u) * (1.0 / GLA_GATE_TAU))
        b_mid = b[mid:mid + 1, :]
        b_last = b[chunk - 1:chunk, :]
        q_e = q * jnp.exp(b - b_mid)
        k_e = (k * jnp.exp(b_mid - b)).astype(BF16)
        k_tail = (k * jnp.exp(b_last - b)).astype(BF16)
        q_dec = (q_e * jnp.exp(b_mid)).astype(BF16)
        q_e = q_e.astype(BF16)
        st = st_ref[...]
        o = lax.dot_general(q_dec, st.astype(BF16), (((1,), (1,)), ((), ())), preferred_element_type=F32)
        o_intra = []
        for h in range(GLA_HEADS):
            q_h = jnp.where((lane_k >= h * hk) & (lane_k < (h + 1) * hk), q_e, jnp.zeros_like(q_e))
            att = lax.dot_general(q_h, k_e, (((1,), (1,)), ((), ())), preferred_element_type=F32)
            att = jnp.where(causal, att, 0.0).astype(BF16)
            o_intra.append(jnp.dot(att, v_b[:, h * hv:(h + 1) * hv], preferred_element_type=F32))
        o = o + jnp.concatenate(o_intra, axis=-1)
        kv = lax.dot_general(v_b, k_tail, (((0,), (0,)), ((), ())), preferred_element_type=F32)
        st_ref[...] = st * jnp.exp(b_last) + jnp.where(st_mask, kv, 0.0)
        g = g_ref[rs, :]
        gate = g * _sigmoid(g)
        y_a = []
        for h in range(GLA_HEADS):
            o_h = o[:, h * hv:(h + 1) * hv]
            ms = jnp.mean(o_h * o_h, axis=-1, keepdims=True)
            y_a.append(o_h * lax.rsqrt(ms + EPS) * gnorm_ref[...] * gate[:, h * hv:(h + 1) * hv])
        mix_ref[rs, 0:dv] = jnp.concatenate(y_a, axis=-1).astype(BF16)

    @pl.when(ti == n_t - 1)
    def _():
        s_out_ref[...] = st_ref[...]
        h_out_ref[...] = hc_ref[...]
        buf_out_ref[...] = xpad_ref[0:SUBLANES, :]


def _ab_mixer(proj, bsz, t_len, first_pos_zero, s_gla, h_lru, conv_buf, w_gk2, b_gk, gla_norm, conv_w, conv_b,
              lru_wa, lru_ba, lru_wi, lru_bi, lru_lam):
    qk, v, g, gk, xr, xg = proj
    n = qk.shape[0]
    dqk, dv, w_lru = qk.shape[1] // 2, v.shape[1], xr.shape[1]
    hk, hv = dqk // GLA_HEADS, dv // GLA_HEADS
    tt = MIX_TILE if t_len % MIX_TILE == 0 else t_len
    chunk = GLA_CHUNK if tt % GLA_CHUNK == 0 else tt
    n_t = t_len // tt

    eye = jnp.eye(GLA_HEADS, dtype=F32)
    st0 = jnp.einsum('bhkv,hg->bhvgk', s_gla.astype(F32), eye).reshape(bsz, dv, dqk)
    buf0 = jnp.pad(conv_buf.astype(F32), ((0, 0), (SUBLANES - (CONV_WIDTH - 1), 0), (0, 0)))
    wgk = jnp.pad(w_gk2, ((0, gk.shape[1] - w_gk2.shape[0]), (0, 0))).astype(BF16)
    nb = lru_wa.shape[0]
    half_blocks = nb // 2

    def block_diag(w):
        bd = w.shape[1]
        e = jnp.eye(half_blocks, dtype=w.dtype)
        w2 = w.reshape(2, half_blocks, bd, bd)
        return jnp.einsum('jnio,nm->jnimo', w2, e).reshape(2, half_blocks * bd, half_blocks * bd).astype(BF16)

    row = lambda z: z.astype(F32)[None, :]
    tok = lambda w: pl.BlockSpec((tt, w), lambda b, i: (b * n_t + i, 0))
    per_b = lambda r, c: pl.BlockSpec((None, r, c), lambda b, i: (b, 0, 0))
    full2 = lambda r, c: pl.BlockSpec((r, c), lambda b, i: (0, 0))
    full3 = lambda a, r, c: pl.BlockSpec((a, r, c), lambda b, i: (0, 0, 0))
    mix, st, h_new, buf = pl.pallas_call(
        functools.partial(_ab_mixer_kernel, first_pos_zero, tt, chunk),
        grid=(bsz, n_t),
        in_specs=[tok(2 * dqk), tok(dv), tok(dv), tok(gk.shape[1]), tok(w_lru), tok(w_lru),
                  per_b(dv, dqk), per_b(1, w_lru), per_b(SUBLANES, w_lru),
                  full2(gk.shape[1], dqk), full2(1, dqk), full2(1, hv), full2(CONV_WIDTH, w_lru), full2(1, w_lru),
                  full3(2, w_lru // 2, w_lru // 2), full3(2, w_lru // 2, w_lru // 2),
                  full2(1, w_lru), full2(1, w_lru), full2(1, w_lru)],
        out_specs=[tok(dv + w_lru), per_b(dv, dqk), per_b(1, w_lru), per_b(SUBLANES, w_lru)],
        out_shape=[jax.ShapeDtypeStruct((n, dv + w_lru), BF16),
                   jax.ShapeDtypeStruct((bsz, dv, dqk), F32),
                   jax.ShapeDtypeStruct((bsz, 1, w_lru), F32),
                   jax.ShapeDtypeStruct((bsz, SUBLANES, w_lru), F32)],
        scratch_shapes=[pltpu.VMEM((dv, dqk), F32), pltpu.VMEM((1, w_lru), F32),
                        pltpu.VMEM((tt + SUBLANES, w_lru), F32)],
        compiler_params=_cparams("parallel", "arbitrary"),
        name="ab_mixer",
    )(qk, v, g, gk, xr, xg, st0, h_lru.astype(F32)[:, None, :], buf0,
      wgk, row(b_gk), row(gla_norm), conv_w.astype(F32), row(conv_b),
      block_diag(lru_wa), block_diag(lru_wi), row(lru_ba), row(lru_bi), row(lru_lam))
    st5 = st.reshape(bsz, GLA_HEADS, hv, GLA_HEADS, hk)
    s_new = jnp.stack([st5[:, h, :, h, :] for h in range(GLA_HEADS)], axis=1).swapaxes(-1, -2)
    return mix, s_new, h_new[:, 0], buf[:, SUBLANES - (CONV_WIDTH - 1):]


def _sb_tile(q_heads, k_blk, v_blk, upper, mask, acc, suffix):
    new_acc, new_suffix = [], []
    for q_h, acc_h, suf_h in zip(q_heads, acc, suffix):
        z = lax.dot_general(q_h, k_blk, (((1,), (1,)), ((), ())), preferred_element_type=F32)
        ls = _log_sigmoid(z)
        lk = ls - z
        if mask is not None:
            lk = jnp.where(mask, lk, 0.0)
        lk_hi = lk.astype(BF16)
        lk_lo = (lk - lk_hi.astype(F32)).astype(BF16)
        right = (jnp.dot(lk_hi, upper, preferred_element_type=F32)
                 + jnp.dot(lk_lo, upper, preferred_element_type=F32))
        w = jnp.exp(ls + right + suf_h)
        if mask is not None:
            w = jnp.where(mask, w, 0.0)
        new_acc.append(acc_h + jnp.dot(w.astype(BF16), v_blk, preferred_element_type=F32))
        new_suffix.append(suf_h + jnp.sum(lk, axis=-1, keepdims=True))
    return new_acc, new_suffix


def _split_heads(q):
    lane = lax.broadcasted_iota(I32, (1, LANES), 1)
    zero = jnp.zeros_like(q)
    return [jnp.where(lane < LANES // 2, q, zero), jnp.where(lane >= LANES // 2, q, zero)], lane


def _upper(n):
    r = lax.broadcasted_iota(I32, (n, n), 0)
    c = lax.broadcasted_iota(I32, (n, n), 1)
    return jnp.where(r > c, 1.0, 0.0).astype(BF16)


def _sb_prompt_kernel(blk, q_ref, k_ref, v_ref, o_ref, acc_ref, suf_ref):
    qi = pl.program_id(2)
    q_heads, lane = _split_heads(q_ref[...])
    upper = _upper(blk)
    r = lax.broadcasted_iota(I32, (blk, blk), 0)
    c = lax.broadcasted_iota(I32, (blk, blk), 1)
    zeros = [jnp.zeros((blk, LANES), F32)] * 2
    zsuf = [jnp.zeros((blk, 1), F32)] * 2
    k0 = pl.multiple_of(qi * blk, blk)
    acc, suf = _sb_tile(q_heads, k_ref[pl.ds(k0, blk), :], v_ref[pl.ds(k0, blk), :], upper, c < r, zeros, zsuf)
    for h in range(2):
        acc_ref[h] = acc[h]
        suf_ref[h] = suf[h]

    def body(j, carry):
        kj = pl.multiple_of((qi - 1 - j) * blk, blk)
        acc, suf = _sb_tile(q_heads, k_ref[pl.ds(kj, blk), :], v_ref[pl.ds(kj, blk), :], upper, None,
                            [acc_ref[0], acc_ref[1]], [suf_ref[0], suf_ref[1]])
        for h in range(2):
            acc_ref[h] = acc[h]
            suf_ref[h] = suf[h]
        return carry

    lax.fori_loop(0, qi, body, 0)
    o_ref[...] = jnp.where(lane < LANES // 2, acc_ref[0], acc_ref[1]).astype(o_ref.dtype)


def _sb_prompt(q, k, v, bsz, t_len):
    n, width = q.shape
    blk = SB_BLOCK
    nq = t_len // blk
    groups = width // LANES
    return pl.pallas_call(
        functools.partial(_sb_prompt_kernel, blk),
        grid=(bsz, groups, nq),
        in_specs=[pl.BlockSpec((blk, LANES), lambda b, g, i: (b * nq + i, g)),
                  pl.BlockSpec((t_len, LANES), lambda b, g, i: (b, g)),
                  pl.BlockSpec((t_len, LANES), lambda b, g, i: (b, g))],
        out_specs=pl.BlockSpec((blk, LANES), lambda b, g, i: (b * nq + i, g)),
        out_shape=jax.ShapeDtypeStruct((n, width), BF16),
        scratch_shapes=[pltpu.VMEM((2, blk, LANES), F32), pltpu.VMEM((2, blk, 1), F32)],
        compiler_params=_cparams("parallel", "parallel", "arbitrary"),
        name="sb_prompt",
    )(q, k, v)


def _sb_sample_kernel(t_new, past_len, blk, q_ref, k_ref, v_ref, pk_ref, pv_ref, o_ref):
    q_heads, lane = _split_heads(q_ref[...])
    r = lax.broadcasted_iota(I32, (t_new, t_new), 0)
    c = lax.broadcasted_iota(I32, (t_new, t_new), 1)
    acc = [jnp.zeros((t_new, LANES), F32)] * 2
    suf = [jnp.zeros((t_new, 1), F32)] * 2
    acc, suf = _sb_tile(q_heads, k_ref[...], v_ref[...], _upper(t_new), c < r, acc, suf)
    upper = _upper(blk)
    for j in reversed(range(past_len // blk)):
        k_blk = pk_ref[j * blk:(j + 1) * blk, :].astype(BF16)
        v_blk = pv_ref[j * blk:(j + 1) * blk, :].astype(BF16)
        acc, suf = _sb_tile(q_heads, k_blk, v_blk, upper, None, acc, suf)
    o_ref[...] = jnp.where(lane < LANES // 2, acc[0], acc[1]).astype(o_ref.dtype)


def _sb_sample(q, k, v, past_k, past_v, bsz, t_len):
    n, width = q.shape
    past_len = past_k.shape[0] // bsz
    blk = SB_BLOCK if past_len % SB_BLOCK == 0 else past_len
    groups = width // LANES
    new = pl.BlockSpec((t_len, LANES), lambda b, g: (b, g))
    past = pl.BlockSpec((past_len, LANES), lambda b, g: (b, g))
    return pl.pallas_call(
        functools.partial(_sb_sample_kernel, t_len, past_len, blk),
        grid=(bsz, groups),
        in_specs=[new, new, new, past, past],
        out_specs=new,
        out_shape=jax.ShapeDtypeStruct((n, width), BF16),
        compiler_params=_cparams("parallel", "parallel"),
        name="sb_sample",
    )(q, k, v, past_k, past_v)


def _first_argmax(vals):
    best_v, best_i = vals[0], jnp.zeros(vals[0].shape, I32)
    for i in range(1, len(vals)):
        better = vals[i] > best_v
        best_v = jnp.where(better, vals[i], best_v)
        best_i = jnp.where(better, i, best_i)
    return best_v, best_i


def _route_rows(logits_t, bias_col):
    s = _sigmoid(logits_t)
    sel = s + bias_col
    epg = EXPERTS_PER_GROUP
    scores = []
    for gi in range(N_GROUPS):
        rows = [sel[gi * epg + i:gi * epg + i + 1, :] for i in range(epg)]
        pair_sums = [rows[i] + rows[j] for i in range(epg) for j in range(i + 1, epg)]
        scores.append(functools.reduce(jnp.maximum, pair_sums))
    _, best = _first_argmax(scores)

    def in_best(mat, i):
        out = mat[i:i + 1, :]
        for gi in range(1, N_GROUPS):
            out = jnp.where(best == gi, mat[gi * epg + i:gi * epg + i + 1, :], out)
        return out

    sel_g = [in_best(sel, i) for i in range(epg)]
    s_g = [in_best(s, i) for i in range(epg)]
    _, i1 = _first_argmax(sel_g)
    _, i2 = _first_argmax([jnp.where(i1 == i, -jnp.inf, sel_g[i]) for i in range(epg)])
    lo = jnp.minimum(i1, i2)
    hi = jnp.maximum(i1, i2)
    pick = lambda idx: functools.reduce(lambda acc, i: jnp.where(idx == i, s_g[i], acc), range(1, epg), s_g[0])
    w_lo, w_hi = pick(lo), pick(hi)
    tot = w_lo + w_hi
    pair = jnp.where(lo == 0, 0, jnp.where(lo == 1, epg - 1, 2 * epg - 3)) + (hi - lo - 1)
    cls = best * N_PAIRS + pair
    return cls, best * epg + lo, best * epg + hi, w_lo / tot, w_hi / tot


def _proj_route_kernel(a_ref, w_ref, x_ref, g1_ref, gain_ref, sh_ref, sc_ref, wr_hi_ref, wr_lo_ref, br_ref,
                       xn_ref, h2_ref, rf_ref, ri_ref):
    y = jnp.dot(a_ref[...], w_ref[...], preferred_element_type=F32)
    xn = x_ref[...] + g1_ref[...] * y
    xn_ref[...] = xn
    h2 = _rms_mod(xn, gain_ref[...], sh_ref[...], sc_ref[...])
    h2_ref[...] = h2
    h_hi = h2.astype(BF16)
    h_lo = (h2 - h_hi.astype(F32)).astype(BF16)
    logits = (jnp.dot(h_hi, wr_hi_ref[...], preferred_element_type=F32)
              + jnp.dot(h_lo, wr_hi_ref[...], preferred_element_type=F32)
              + jnp.dot(h_hi, wr_lo_ref[...], preferred_element_type=F32))
    n_e = br_ref.shape[0]
    logits_t = logits.T[0:n_e, :]
    cls, e_lo, e_hi, w_lo, w_hi = _route_rows(logits_t, br_ref[...])
    tm = logits_t.shape[1]
    rf_ref[...] = jnp.concatenate([w_lo, w_hi, jnp.zeros((SUBLANES - 2, tm), F32)], axis=0)
    ri_ref[...] = jnp.concatenate([cls, e_lo, e_hi, jnp.zeros((SUBLANES - 3, tm), I32)], axis=0)


def _proj_route(a, w_out, x, t_len, g1, gain, shift, scale, w_router, b_router):
    n, d = x.shape
    k = a.shape[1]
    n_e = w_router.shape[1]
    tm = _row_tile(n, t_len, ROW_TILE)
    g1_arr, g1_spec = _mod_operand(g1, n, t_len, tm)
    sh_arr, sh_spec = _mod_operand(shift, n, t_len, tm)
    sc_arr, sc_spec = _mod_operand(scale, n, t_len, tm)
    wr = jnp.pad(w_router.astype(F32), ((0, 0), (0, LANES - n_e)))
    wr_hi = wr.astype(BF16)
    wr_lo = (wr - wr_hi.astype(F32)).astype(BF16)
    tokd = pl.BlockSpec((tm, d), lambda i: (i, 0))
    const = lambda r, c: pl.BlockSpec((r, c), lambda i: (0, 0))
    route = pl.BlockSpec((SUBLANES, tm), lambda i: (0, i))
    return pl.pallas_call(
        _proj_route_kernel,
        grid=(n // tm,),
        in_specs=[pl.BlockSpec((tm, k), lambda i: (i, 0)), const(k, d), tokd, g1_spec(0), const(1, d),
                  sh_spec(0), sc_spec(0), const(d, LANES), const(d, LANES), const(n_e, 1)],
        out_specs=[tokd, tokd, route, route],
        out_shape=[jax.ShapeDtypeStruct((n, d), F32), jax.ShapeDtypeStruct((n, d), F32),
                   jax.ShapeDtypeStruct((SUBLANES, n), F32), jax.ShapeDtypeStruct((SUBLANES, n), I32)],
        compiler_params=_cparams("parallel"),
        name="proj_route",
    )(a, w_out, x, g1_arr, gain[None, :], sh_arr, sc_arr, wr_hi, wr_lo, b_router.astype(F32)[:, None])


def _gather_rows(src_hbm, idx_ref, base, n_rows, dst, sem):
    def issue(r, carry):
        pltpu.make_async_copy(src_hbm.at[pl.ds(idx_ref[base + r], 1)], dst.at[pl.ds(r, 1)], sem).start()
        return carry
    lax.fori_loop(0, n_rows, issue, 0, unroll=8)


def _wait_rows(src_hbm, n_rows, dst, sem):
    pltpu.make_async_copy(src_hbm.at[pl.ds(0, n_rows)], dst, sem).wait()


def _moe_kernel(bm, src_ref, e_lo_ref, e_hi_ref, used_ref,
                h_hbm, w_ref, w1a_ref, w3a_ref, w2a_ref, w1b_ref, w3b_ref, w2b_ref,
                y_ref, buf_ref, sem_ref):
    i = pl.program_id(0)
    n_used = used_ref[0]
    slot = i % 2

    @pl.when((i == 0) & (n_used > 0))
    def _():
        _gather_rows(h_hbm, src_ref, 0, bm, buf_ref.at[0], sem_ref.at[0])

    @pl.when(i + 1 < n_used)
    def _():
        _gather_rows(h_hbm, src_ref, (i + 1) * bm, bm, buf_ref.at[1 - slot], sem_ref.at[1 - slot])

    @pl.when(i < n_used)
    def _():
        _wait_rows(h_hbm, bm, buf_ref.at[slot], sem_ref.at[slot])
        xb = buf_ref[slot].astype(BF16)

        def expert(w1_ref, w3_ref, w2_ref):
            u = jnp.dot(xb, w1_ref[...], preferred_element_type=F32)
            t = jnp.dot(xb, w3_ref[...], preferred_element_type=F32)
            mid = (u * _sigmoid(u) * t).astype(BF16)
            return jnp.dot(mid, w2_ref[...], preferred_element_type=F32)

        y_ref[...] = (expert(w1a_ref, w3a_ref, w2a_ref) * w_ref[:, 0:1]
                      + expert(w1b_ref, w3b_ref, w2b_ref) * w_ref[:, 1:2])

    @pl.when(i >= n_used)
    def _():
        y_ref[...] = jnp.zeros_like(y_ref)


def _moe_sorted(h2, src_rows, w_sorted, blk_lo, blk_hi, n_used, w1, w3, w2):
    n, d = h2.shape
    p = src_rows.shape[0]
    bm = MOE_BLOCK
    de = w1.shape[2]
    wspec = lambda which, r, c: pl.BlockSpec(
        (None, r, c), lambda i, src, lo, hi, used: ((lo, hi)[which][jnp.minimum(i, jnp.maximum(used[0] - 1, 0))], 0, 0))
    return pl.pallas_call(
        functools.partial(_moe_kernel, bm),
        grid_spec=pltpu.PrefetchScalarGridSpec(
            num_scalar_prefetch=4,
            grid=(p // bm,),
            in_specs=[pl.BlockSpec(memory_space=pl.ANY),
                      pl.BlockSpec((bm, 2), lambda i, *_: (i, 0)),
                      wspec(0, d, de), wspec(0, d, de), wspec(0, de, d),
                      wspec(1, d, de), wspec(1, d, de), wspec(1, de, d)],
            out_specs=pl.BlockSpec((bm, d), lambda i, *_: (i, 0)),
            scratch_shapes=[pltpu.VMEM((2, bm, d), F32), pltpu.SemaphoreType.DMA((2,))]),
        out_shape=jax.ShapeDtypeStruct((p, d), F32),
        compiler_params=_cparams("arbitrary"),
        name="moe_sorted",
    )(src_rows, blk_lo, blk_hi, n_used, h2, w_sorted, w1, w3, w2, w1, w3, w2)


def _unsort_kernel(tm, final_norm, pos_ref, y_hbm, x_ref, g2_ref, gain_ref, o_ref, buf_ref, sem_ref):
    i = pl.program_id(0)
    n_i = pl.num_programs(0)
    slot = i % 2

    @pl.when(i == 0)
    def _():
        _gather_rows(y_hbm, pos_ref, 0, tm, buf_ref.at[0], sem_ref.at[0])

    @pl.when(i + 1 < n_i)
    def _():
        _gather_rows(y_hbm, pos_ref, (i + 1) * tm, tm, buf_ref.at[1 - slot], sem_ref.at[1 - slot])

    _wait_rows(y_hbm, tm, buf_ref.at[slot], sem_ref.at[slot])
    x = x_ref[...] + g2_ref[...] * buf_ref[slot]
    if final_norm:
        ms = jnp.mean(x * x, axis=-1, keepdims=True)
        x = x * lax.rsqrt(ms + EPS) * gain_ref[...]
    o_ref[...] = x


def _unsort_resid(y_sorted, pos, x, t_len, g2, final_gain):
    n, d = x.shape
    tm = _row_tile(n, t_len, GATHER_TILE)
    g2_arr, g2_spec = _mod_operand(g2, n, t_len, tm)
    final_norm = final_gain is not None
    gain = (final_gain if final_norm else jnp.ones((d,), F32)).astype(F32)[None, :]
    return pl.pallas_call(
        functools.partial(_unsort_kernel, tm, final_norm),
        grid_spec=pltpu.PrefetchScalarGridSpec(
            num_scalar_prefetch=1,
            grid=(n // tm,),
            in_specs=[pl.BlockSpec(memory_space=pl.ANY),
                      pl.BlockSpec((tm, d), lambda i, pos: (i, 0)),
                      g2_spec(0),
                      pl.BlockSpec((1, d), lambda i, pos: (0, 0))],
            out_specs=pl.BlockSpec((tm, d), lambda i, pos: (i, 0)),
            scratch_shapes=[pltpu.VMEM((2, tm, d), F32), pltpu.SemaphoreType.DMA((2,))]),
        out_shape=jax.ShapeDtypeStruct((n, d), F32),
        compiler_params=_cparams("arbitrary"),
        name="unsort_resid",
    )(pos, y_sorted, x, g2_arr, gain)


def _sort_plan(cls, e_lo, e_hi, weights):
    n = cls.shape[0]
    bm = MOE_BLOCK
    n_blocks = -(-n // bm) + N_CLASSES
    p = n_blocks * bm
    order = jnp.argsort(cls, stable=True).astype(I32)
    cls_sorted = cls[order]
    counts = jnp.zeros((N_CLASSES,), I32).at[cls].add(1)
    padded = (counts + bm - 1) // bm * bm
    pad_end = jnp.cumsum(padded)
    pad_start = pad_end - padded
    start = jnp.cumsum(counts) - counts
    dest = pad_start[cls_sorted] + jnp.arange(n, dtype=I32) - start[cls_sorted]
    src_rows = jnp.zeros((p,), I32).at[dest].set(order)
    pos = jnp.zeros((n,), I32).at[order].set(dest)
    w_sorted = jnp.zeros((p, 2), F32).at[dest].set(weights.T[order])
    blk_lo = jnp.zeros((n_blocks,), I32).at[dest // bm].set(e_lo[order])
    blk_hi = jnp.zeros((n_blocks,), I32).at[dest // bm].set(e_hi[order])
    n_used = (pad_end[-1] // bm).astype(I32)[None]
    return src_rows, pos, w_sorted, blk_lo, blk_hi, n_used


def _moe_layer(h2, route_f, route_i, x, t_len, g2, w1, w3, w2, final_gain):
    src_rows, pos, w_sorted, blk_lo, blk_hi, n_used = _sort_plan(route_i[0], route_i[1], route_i[2], route_f[0:2])
    y_sorted = _moe_sorted(h2, src_rows, w_sorted, blk_lo, blk_hi, n_used, w1, w3, w2)
    return _unsort_resid(y_sorted, pos, x, t_len, g2, final_gain)


def _trunk(x3, c, p, past):
    bsz, t_len, d = x3.shape
    n = bsz * t_len
    x = x3.reshape(n, d)
    depth = p['w_ada'].shape[0]
    mod = _ada(c.astype(F32), p['w_ada'], p['b_ada']).reshape(depth, bsz, N_MOD, d)
    gla_s, lru_s, conv_s, ks, vs = [], [], [], [], []
    for l in range(depth):
        sh1, sc1, g1, sh2, sc2, g2 = [mod[l, :, i] for i in range(N_MOD)]
        j = l // 2
        if l % 2 == 0:
            w_in = p['w_ab_in'][j]
            dqk2 = 2 * p['w_gk2'].shape[2]
            dv = GLA_HEADS * p['gla_norm'].shape[1]
            rank = p['w_gk2'].shape[1]
            w_lru = p['lru_lam'].shape[1]
            cuts = [0, dqk2, dqk2 + dv, dqk2 + 2 * dv, dqk2 + 2 * dv + rank,
                    dqk2 + 2 * dv + rank + w_lru, dqk2 + 2 * dv + rank + 2 * w_lru]
            cols = [w_in[:, cuts[i]:cuts[i + 1]] for i in range(6)]
            cols[3] = jnp.pad(cols[3], ((0, 0), (0, LANES - rank)))
            proj = _norm_proj(x, t_len, p['norm_mix'][l], sh1, sc1,
                              [(w.astype(BF16), 1.0, [F32]) for w in cols])
            if past is None:
                s0 = jnp.zeros((bsz, GLA_HEADS, dqk2 // 2 // GLA_HEADS, dv // GLA_HEADS), F32)
                h0 = jnp.zeros((bsz, w_lru), F32)
                buf = jnp.zeros((bsz, CONV_WIDTH - 1, w_lru), F32)
            else:
                s0, h0, buf = past['state_gla'][j], past['state_lru'][j], past['state_conv'][j]
            mix, sg, sl, sc = _ab_mixer(proj, bsz, t_len, past is None, s0, h0, buf, p['w_gk2'][j], p['b_gk'][j],
                                        p['gla_norm'][j], p['conv_w'][j], p['conv_b'][j], p['lru_wa'][j],
                                        p['lru_ba'][j], p['lru_wi'][j], p['lru_bi'][j], p['lru_lam'][j])
            gla_s.append(sg)
            lru_s.append(sl)
            conv_s.append(sc)
            w_out = p['w_ab_out'][j]
        else:
            w_qkv = p['w_sb_qkv'][j]
            width = w_qkv.shape[1] // 3
            heads = past['cache_k'].shape[3] if past is not None else p['sb_heads']
            dh = width // heads
            pieces = [(w_qkv[:, 0:width].astype(BF16), dh ** -0.5, [BF16]),
                      (w_qkv[:, width:2 * width].astype(BF16), 1.0, [F32, BF16]),
                      (w_qkv[:, 2 * width:].astype(BF16), 1.0, [F32, BF16])]
            q_b, k_f, k_b, v_f, v_b = _norm_proj(x, t_len, p['norm_mix'][l], sh1, sc1, pieces)
            if past is None:
                mix = _sb_prompt(q_b, k_b, v_b, bsz, t_len)
            else:
                pk = past['cache_k'][j].reshape(-1, width)
                pv = past['cache_v'][j].reshape(-1, width)
                mix = _sb_sample(q_b, k_b, v_b, pk, pv, bsz, t_len)
            ks.append(k_f.reshape(bsz, t_len, heads, dh))
            vs.append(v_f.reshape(bsz, t_len, heads, dh))
            w_out = p['w_sb_out'][j]
        xn, h2, route_f, route_i = _proj_route(mix, w_out.astype(BF16), x, t_len, g1, p['norm_ffn'][l], sh2, sc2,
                                               p['w_router'], p['b_router'])
        x = _moe_layer(h2, route_f, route_i, xn, t_len, g2,
                       p['w_e1'][l].astype(BF16), p['w_e3'][l].astype(BF16), p['w_e2'][l].astype(BF16),
                       p['norm_out'] if l == depth - 1 else None)
    return x.reshape(bsz, t_len, d), (jnp.stack(gla_s), jnp.stack(lru_s), jnp.stack(conv_s),
                                      jnp.stack(ks), jnp.stack(vs))


def kernel(x_prompt, x_sample, state_gla, state_lru, state_conv, cache_k, cache_v, c_prompt, c_sample,
           w_ada, b_ada, norm_mix, norm_ffn, norm_out, w_ab_in, w_gk2, b_gk, gla_norm, conv_w, conv_b,
           lru_wa, lru_ba, lru_wi, lru_bi, lru_lam, w_ab_out, w_sb_qkv, w_sb_out, w_router, b_router,
           w_e1, w_e3, w_e2):
    p = dict(w_ada=w_ada, b_ada=b_ada, norm_mix=norm_mix, norm_ffn=norm_ffn, norm_out=norm_out,
             w_ab_in=w_ab_in, w_gk2=w_gk2, b_gk=b_gk, gla_norm=gla_norm, conv_w=conv_w, conv_b=conv_b,
             lru_wa=lru_wa, lru_ba=lru_ba, lru_wi=lru_wi, lru_bi=lru_bi, lru_lam=lru_lam, w_ab_out=w_ab_out,
             w_sb_qkv=w_sb_qkv, w_sb_out=w_sb_out, w_router=w_router, b_router=b_router,
             w_e1=w_e1, w_e3=w_e3, w_e2=w_e2, sb_heads=cache_k.shape[3])
    past = dict(state_gla=state_gla, state_lru=state_lru, state_conv=state_conv, cache_k=cache_k, cache_v=cache_v)
    y_prompt, (p_gla, p_lru, p_conv, p_k, p_v) = _trunk(x_prompt, c_prompt, p, None)
    y_sample, (s_gla, s_lru, s_conv, s_k, s_v) = _trunk(x_sample, c_sample, p, past)
    return (y_prompt, y_sample, p_gla, p_lru, p_conv, p_k, p_v, s_gla, s_lru, s_conv, s_k, s_v)
```

```python
import functools

import jax
import jax.numpy as jnp
from jax import lax
from jax.experimental import pallas as pl
from jax.experimental.pallas import tpu as pltpu

F32 = jnp.float32
BF16 = jnp.bfloat16
I32 = jnp.int32

EPS = 1e-6
N_MOD = 6
GLA_HEADS = 4
GLA_GATE_TAU = 16.0
LRU_C = 8.0
CONV_WIDTH = 4
N_GROUPS = 4
EXPERTS_PER_GROUP = 4
N_PAIRS = 6
N_CLASSES = N_GROUPS * N_PAIRS

LANES = 128
SUBLANES = 8
VMEM_LIMIT = 56 * 1024 * 1024

ROW_TILE = 512
MIX_TILE = 256
GLA_CHUNK = 64
SB_BLOCK = 256
MOE_BLOCK = 256
GATHER_TILE = 256


def _cparams(*sem):
    return pltpu.CompilerParams(dimension_semantics=sem, vmem_limit_bytes=VMEM_LIMIT)


def _log_sigmoid(z):
    return jnp.minimum(z, 0.0) - jnp.log(1.0 + jnp.exp(-jnp.abs(z)))


def _softplus(z):
    return jnp.maximum(z, 0.0) + jnp.log1p(jnp.exp(-jnp.abs(z)))


def _sigmoid(z):
    return 1.0 / (1.0 + jnp.exp(-z))


def _rms_mod(x, gain, shift, scale):
    ms = jnp.mean(x * x, axis=-1, keepdims=True)
    y = x * lax.rsqrt(ms + EPS) * gain
    return y * (1.0 + scale) + shift


def _row_tile(n_rows, t_len, target):
    if t_len % target == 0:
        return target
    return n_rows


def _mod_operand(mod, n_rows, t_len, tile):
    d = mod.shape[-1]
    if t_len % tile == 0:
        per_seq = t_len // tile
        return mod[:, None, :], (lambda nidx: pl.BlockSpec((None, 1, d), lambda *i: (i[nidx] // per_seq, 0, 0)))
    rows = jnp.repeat(mod, t_len, axis=0)
    return rows, (lambda nidx: pl.BlockSpec((tile, d), lambda *i: (i[nidx], 0)))


def _ada_kernel(c_ref, w_ref, b_ref, o_ref):
    c = c_ref[...]
    cond = c * _sigmoid(c)
    o_ref[...] = jnp.dot(cond.astype(BF16), w_ref[...].astype(BF16), preferred_element_type=F32) + b_ref[...]


def _ada(c, w_ada, b_ada):
    depth, d, e = w_ada.shape
    b = c.shape[0]
    tn = d
    return pl.pallas_call(
        _ada_kernel,
        grid=(depth, e // tn),
        in_specs=[pl.BlockSpec((b, d), lambda l, j: (0, 0)),
                  pl.BlockSpec((None, d, tn), lambda l, j: (l, 0, j)),
                  pl.BlockSpec((None, 1, tn), lambda l, j: (l, 0, j))],
        out_specs=pl.BlockSpec((None, b, tn), lambda l, j: (l, 0, j)),
        out_shape=jax.ShapeDtypeStruct((depth, b, e), F32),
        compiler_params=_cparams("parallel", "parallel"),
        name="ada_mod",
    )(c, w_ada, b_ada[:, None, :])


def _norm_proj_kernel(out_plan, x_ref, gain_ref, sh_ref, sc_ref, *refs):
    n_w = len(out_plan)
    w_refs, o_refs = refs[:n_w], refs[n_w:]
    h = _rms_mod(x_ref[...], gain_ref[...], sh_ref[...], sc_ref[...]).astype(BF16)
    k = 0
    for w_ref, (scale, dtypes) in zip(w_refs, out_plan):
        y = jnp.dot(h, w_ref[...], preferred_element_type=F32)
        if scale != 1.0:
            y = y * scale
        for dt in dtypes:
            o_refs[k][...] = y.astype(dt)
            k += 1


def _norm_proj(x, t_len, gain, shift, scale, pieces):
    n, d = x.shape
    tm = _row_tile(n, t_len, ROW_TILE)
    sh_arr, sh_spec = _mod_operand(shift, n, t_len, tm)
    sc_arr, sc_spec = _mod_operand(scale, n, t_len, tm)
    in_specs = [pl.BlockSpec((tm, d), lambda i: (i, 0)),
                pl.BlockSpec((1, d), lambda i: (0, 0)),
                sh_spec(0), sc_spec(0)]
    out_specs, out_shapes, plan, weights = [], [], [], []
    for w, s, dtypes in pieces:
        e = w.shape[1]
        in_specs.append(pl.BlockSpec((d, e), lambda i: (0, 0)))
        weights.append(w)
        plan.append((s, tuple(dtypes)))
        for dt in dtypes:
            out_specs.append(pl.BlockSpec((tm, e), lambda i: (i, 0)))
            out_shapes.append(jax.ShapeDtypeStruct((n, e), dt))
    return pl.pallas_call(
        functools.partial(_norm_proj_kernel, tuple(plan)),
        grid=(n // tm,),
        in_specs=in_specs, out_specs=out_specs, out_shape=out_shapes,
        compiler_params=_cparams("parallel"),
        name="norm_proj",
    )(x, gain[None, :], sh_arr, sc_arr, *weights)


def _shift_rows(x, s, fill):
    rows = lax.broadcasted_iota(I32, x.shape, 0)
    return jnp.where(rows >= s, pltpu.roll(x, s, axis=0), fill)


def _cumsum_rows(x):
    n = x.shape[0]
    s = 1
    while s < n:
        x = x + _shift_rows(x, s, 0.0)
        s *= 2
    return x


def _linear_scan_rows(a, b):
    n = a.shape[0]
    s = 1
    while s < n:
        b = a * _shift_rows(b, s, 0.0) + b
        a = a * _shift_rows(a, s, 1.0)
        s *= 2
    return a, b


def _gelu_tanh(x):
    return 0.5 * x * (1.0 + jnp.tanh(0.7978845608028654 * (x + 0.044715 * (x * x * x))))


def _ab_mixer_kernel(first_pos_zero, tt, chunk,
                     qk_ref, v_ref, g_ref, gk_ref, xr_ref, xg_ref, s0_ref, h0_ref, buf0_ref,
                     wgk_ref, bgk_ref, gnorm_ref, cw_ref, cb_ref, wa_ref, wi_ref, ba_ref, bi_ref, lam_ref,
                     mix_ref, s_out_ref, h_out_ref, buf_out_ref,
                     st_ref, hc_ref, xpad_ref):
    ti = pl.program_id(1)
    n_t = pl.num_programs(1)
    dqk = qk_ref.shape[1] // 2
    dv = v_ref.shape[1]
    hk = dqk // GLA_HEADS
    hv = dv // GLA_HEADS
    w_lru = xr_ref.shape[1]

    @pl.when(ti == 0)
    def _():
        st_ref[...] = s0_ref[...]
        hc_ref[...] = h0_ref[...]
        xpad_ref[0:SUBLANES, :] = buf0_ref[...]

    xpad_ref[SUBLANES:SUBLANES + tt, :] = xr_ref[...]
    xc = cb_ref[...]
    for i in range(CONV_WIDTH):
        off = SUBLANES - (CONV_WIDTH - 1) + i
        xc = xc + cw_ref[i:i + 1, :] * xpad_ref[off:off + tt, :]
    tail = xpad_ref[tt:tt + SUBLANES, :]
    xpad_ref[0:SUBLANES, :] = tail
    xc_b = xc.astype(BF16)
    half = w_lru // 2
    r_lin = jnp.concatenate([jnp.dot(xc_b[:, j * half:(j + 1) * half], wa_ref[j], preferred_element_type=F32)
                             for j in range(2)], axis=-1)
    i_lin = jnp.concatenate([jnp.dot(xc_b[:, j * half:(j + 1) * half], wi_ref[j], preferred_element_type=F32)
                             for j in range(2)], axis=-1)
    r = _sigmoid(r_lin + ba_ref[...])
    i_g = _sigmoid(i_lin + bi_ref[...])
    log_at = (-LRU_C) * r * _softplus(-lam_ref[...])
    a = jnp.exp(log_at)
    mult = jnp.sqrt(1.0 - a * a)
    if first_pos_zero:
        rows = lax.broadcasted_iota(I32, mult.shape, 0)
        mult = jnp.where((rows == 0) & (ti == 0), 1.0, mult)
    bterm = mult * (i_g * xc)
    a_cum, h_loc = _linear_scan_rows(a, bterm)
    hs = a_cum * hc_ref[...] + h_loc
    hc_ref[...] = hs[tt - 1:tt, :]
    y_b = hs * _gelu_tanh(xg_ref[...])
    mix_ref[:, dv:] = y_b.astype(BF16)

    lane_k = lax.broadcasted_iota(I32, (1, dqk), 1)
    st_rows = lax.broadcasted_iota(I32, (dv, dqk), 0)
    st_cols = lax.broadcasted_iota(I32, (dv, dqk), 1)
    st_mask = functools.reduce(
        jnp.logical_or,
        [(st_rows >= h * hv) & (st_rows < (h + 1) * hv) & (st_cols >= h * hk) & (st_cols < (h + 1) * hk)
         for h in range(GLA_HEADS)])
    crow = lax.broadcasted_iota(I32, (chunk, chunk), 0)
    ccol = lax.broadcasted_iota(I32, (chunk, chunk), 1)
    causal = ccol <= crow
    mid = chunk // 2 - 1
    for c in range(tt // chunk):
        rs = slice(c * chunk, (c + 1) * chunk)
        q = qk_ref[rs, 0:dqk] * (hk ** -0.5)
        k = qk_ref[rs, dqk:2 * dqk]
        v_b = v_ref[rs, :].astype(BF16)
        u = jnp.dot(gk_ref[rs, :].astype(BF16), wgk_ref[...], preferred_element_type=F32) + bgk_ref[...]
        b = _cumsum_rows(_log_sigmoid(u) * (1.0 / GLA_GATE_TAU))
        b_mid = b[mid:mid + 1, :]
        b_last = b[chunk - 1:chunk, :]
        q_e = q * jnp.exp(b - b_mid)
        k_e = (k * jnp.exp(b_mid - b)).astype(BF16)
        k_tail = (k * jnp.exp(b_last - b)).astype(BF16)
        q_dec = (q_e * jnp.exp(b_mid)).astype(BF16)
        q_e = q_e.astype(BF16)
        st = st_ref[...]
        o = lax.dot_general(q_dec, st.astype(BF16), (((1,), (1,)), ((), ())), preferred_element_type=F32)
        o_intra = []
        for h in range(GLA_HEADS):
            q_h = jnp.where((lane_k >= h * hk) & (lane_k < (h + 1) * hk), q_e, jnp.zeros_like(q_e))
            att = lax.dot_general(q_h, k_e, (((1,), (1,)), ((), ())), preferred_element_type=F32)
            att = jnp.where(causal, att, 0.0).astype(BF16)
            o_intra.append(jnp.dot(att, v_b[:, h * hv:(h + 1) * hv], preferred_element_type=F32))
        o = o + jnp.concatenate(o_intra, axis=-1)
        kv = lax.dot_general(v_b, k_tail, (((0,), (0,)), ((), ())), preferred_element_type=F32)
        st_ref[...] = st * jnp.exp(b_last) + jnp.where(st_mask, kv, 0.0)
        g = g_ref[rs, :]
        gate = g * _sigmoid(g)
        y_a = []
        for h in range(GLA_HEADS):
            o_h = o[:, h * hv:(h + 1) * hv]
            ms = jnp.mean(o_h * o_h, axis=-1, keepdims=True)
            y_a.append(o_h * lax.rsqrt(ms + EPS) * gnorm_ref[...] * gate[:, h * hv:(h + 1) * hv])
        mix_ref[rs, 0:dv] = jnp.concatenate(y_a, axis=-1).astype(BF16)

    @pl.when(ti == n_t - 1)
    def _():
        s_out_ref[...] = st_ref[...]
        h_out_ref[...] = hc_ref[...]
        buf_out_ref[...] = xpad_ref[0:SUBLANES, :]


def _ab_mixer(proj, bsz, t_len, first_pos_zero, s_gla, h_lru, conv_buf, w_gk2, b_gk, gla_norm, conv_w, conv_b,
              lru_wa, lru_ba, lru_wi, lru_bi, lru_lam):
    qk, v, g, gk, xr, xg = proj
    n = qk.shape[0]
    dqk, dv, w_lru = qk.shape[1] // 2, v.shape[1], xr.shape[1]
    hk, hv = dqk // GLA_HEADS, dv // GLA_HEADS
    tt = MIX_TILE if t_len % MIX_TILE == 0 else t_len
    chunk = GLA_CHUNK if tt % GLA_CHUNK == 0 else tt
    n_t = t_len // tt

    eye = jnp.eye(GLA_HEADS, dtype=F32)
    st0 = jnp.einsum('bhkv,hg->bhvgk', s_gla.astype(F32), eye).reshape(bsz, dv, dqk)
    buf0 = jnp.pad(conv_buf.astype(F32), ((0, 0), (SUBLANES - (CONV_WIDTH - 1), 0), (0, 0)))
    wgk = jnp.pad(w_gk2, ((0, gk.shape[1] - w_gk2.shape[0]), (0, 0))).astype(BF16)
    nb = lru_wa.shape[0]
    half_blocks = nb // 2

    def block_diag(w):
        bd = w.shape[1]
        e = jnp.eye(half_blocks, dtype=w.dtype)
        w2 = w.reshape(2, half_blocks, bd, bd)
        return jnp.einsum('jnio,nm->jnimo', w2, e).reshape(2, half_blocks * bd, half_blocks * bd).astype(BF16)

    row = lambda z: z.astype(F32)[None, :]
    tok = lambda w: pl.BlockSpec((tt, w), lambda b, i: (b * n_t + i, 0))
    per_b = lambda r, c: pl.BlockSpec((None, r, c), lambda b, i: (b, 0, 0))
    full2 = lambda r, c: pl.BlockSpec((r, c), lambda b, i: (0, 0))
    full3 = lambda a, r, c: pl.BlockSpec((a, r, c), lambda b, i: (0, 0, 0))
    mix, st, h_new, buf = pl.pallas_call(
        functools.partial(_ab_mixer_kernel, first_pos_zero, tt, chunk),
        grid=(bsz, n_t),
        in_specs=[tok(2 * dqk), tok(dv), tok(dv), tok(gk.shape[1]), tok(w_lru), tok(w_lru),
                  per_b(dv, dqk), per_b(1, w_lru), per_b(SUBLANES, w_lru),
                  full2(gk.shape[1], dqk), full2(1, dqk), full2(1, hv), full2(CONV_WIDTH, w_lru), full2(1, w_lru),
                  full3(2, w_lru // 2, w_lru // 2), full3(2, w_lru // 2, w_lru // 2),
                  full2(1, w_lru), full2(1, w_lru), full2(1, w_lru)],
        out_specs=[tok(dv + w_lru), per_b(dv, dqk), per_b(1, w_lru), per_b(SUBLANES, w_lru)],
        out_shape=[jax.ShapeDtypeStruct((n, dv + w_lru), BF16),
                   jax.ShapeDtypeStruct((bsz, dv, dqk), F32),
                   jax.ShapeDtypeStruct((bsz, 1, w_lru), F32),
                   jax.ShapeDtypeStruct((bsz, SUBLANES, w_lru), F32)],
        scratch_shapes=[pltpu.VMEM((dv, dqk), F32), pltpu.VMEM((1, w_lru), F32),
                        pltpu.VMEM((tt + SUBLANES, w_lru), F32)],
        compiler_params=_cparams("parallel", "arbitrary"),
        name="ab_mixer",
    )(qk, v, g, gk, xr, xg, st0, h_lru.astype(F32)[:, None, :], buf0,
      wgk, row(b_gk), row(gla_norm), conv_w.astype(F32), row(conv_b),
      block_diag(lru_wa), block_diag(lru_wi), row(lru_ba), row(lru_bi), row(lru_lam))
    st5 = st.reshape(bsz, GLA_HEADS, hv, GLA_HEADS, hk)
    s_new = jnp.stack([st5[:, h, :, h, :] for h in range(GLA_HEADS)], axis=1).swapaxes(-1, -2)
    return mix, s_new, h_new[:, 0], buf[:, SUBLANES - (CONV_WIDTH - 1):]


SB_MASKED = -1e30


def _sb_scores(q_h, k_blk, upper, mask):
    z = lax.dot_general(q_h, k_blk, (((1,), (1,)), ((), ())), preferred_element_type=F32)
    ls = _log_sigmoid(z)
    lk = ls - z
    if mask is not None:
        lk = jnp.where(mask, lk, 0.0)
    pre = ls + jnp.dot(lk.astype(BF16), upper, preferred_element_type=F32)
    if mask is not None:
        pre = jnp.where(mask, pre, SB_MASKED)
    return pre, jnp.sum(lk, axis=-1, keepdims=True)


def _sb_values(pre, suffix, v_blk):
    return jnp.dot(jnp.exp(pre + suffix).astype(BF16), v_blk, preferred_element_type=F32)


def _split_heads(q):
    lane = lax.broadcasted_iota(I32, (1, LANES), 1)
    zero = jnp.zeros_like(q)
    return [jnp.where(lane < LANES // 2, q, zero), jnp.where(lane >= LANES // 2, q, zero)], lane


def _upper(n):
    r = lax.broadcasted_iota(I32, (n, n), 0)
    c = lax.broadcasted_iota(I32, (n, n), 1)
    return jnp.where(r > c, 1.0, 0.0).astype(BF16)


def _sb_prompt_kernel(blk, q_ref, k_ref, v_ref, o_ref, acc_ref, suf_ref, pre_ref, tot_ref):
    qi = pl.program_id(2)
    q_heads, lane = _split_heads(q_ref[...])
    upper = _upper(blk)
    r = lax.broadcasted_iota(I32, (blk, blk), 0)
    c = lax.broadcasted_iota(I32, (blk, blk), 1)

    def scores(kb, slot, mask):
        k_blk = k_ref[pl.ds(pl.multiple_of(kb * blk, blk), blk), :]
        for h in range(2):
            pre, tot = _sb_scores(q_heads[h], k_blk, upper, mask)
            pre_ref[slot, h] = pre
            tot_ref[slot, h] = tot

    def values(kb, slot):
        v_blk = v_ref[pl.ds(pl.multiple_of(kb * blk, blk), blk), :]
        for h in range(2):
            acc_ref[h] += _sb_values(pre_ref[slot, h], suf_ref[h], v_blk)
            suf_ref[h] += tot_ref[slot, h]

    acc_ref[...] = jnp.zeros_like(acc_ref)
    suf_ref[...] = jnp.zeros_like(suf_ref)
    scores(qi, 0, c < r)

    def body(j, carry):
        slot = j % 2
        values(qi - j + 1, 1 - slot)
        scores(qi - j, slot, None)
        return carry

    lax.fori_loop(1, qi + 1, body, 0)
    values(0, qi % 2)
    o_ref[...] = jnp.where(lane < LANES // 2, acc_ref[0], acc_ref[1]).astype(o_ref.dtype)


def _sb_prompt(q, k, v, bsz, t_len):
    n, width = q.shape
    blk = SB_BLOCK
    nq = t_len // blk
    groups = width // LANES
    return pl.pallas_call(
        functools.partial(_sb_prompt_kernel, blk),
        grid=(bsz, groups, nq),
        in_specs=[pl.BlockSpec((blk, LANES), lambda b, g, i: (b * nq + i, g)),
                  pl.BlockSpec((t_len, LANES), lambda b, g, i: (b, g)),
                  pl.BlockSpec((t_len, LANES), lambda b, g, i: (b, g))],
        out_specs=pl.BlockSpec((blk, LANES), lambda b, g, i: (b * nq + i, g)),
        out_shape=jax.ShapeDtypeStruct((n, width), BF16),
        scratch_shapes=[pltpu.VMEM((2, blk, LANES), F32), pltpu.VMEM((2, blk, 1), F32),
                        pltpu.VMEM((2, 2, blk, blk), F32), pltpu.VMEM((2, 2, blk, 1), F32)],
        compiler_params=_cparams("parallel", "parallel", "arbitrary"),
        name="sb_prompt",
    )(q, k, v)


def _sb_sample_kernel(t_new, past_len, blk, q_ref, k_ref, v_ref, pk_ref, pv_ref, o_ref):
    q_heads, lane = _split_heads(q_ref[...])
    r = lax.broadcasted_iota(I32, (t_new, t_new), 0)
    c = lax.broadcasted_iota(I32, (t_new, t_new), 1)
    upper = _upper(blk)
    blocks = [(k_ref[...], v_ref[...], _upper(t_new), c < r)]
    for j in reversed(range(past_len // blk)):
        rows = slice(j * blk, (j + 1) * blk)
        blocks.append((pk_ref[rows, :].astype(BF16), pv_ref[rows, :].astype(BF16), upper, None))
    acc = []
    for q_h in q_heads:
        staged = [_sb_scores(q_h, k_blk, up, mask) + (v_blk,) for k_blk, v_blk, up, mask in blocks]
        acc_h = jnp.zeros((t_new, LANES), F32)
        suf = jnp.zeros((t_new, 1), F32)
        for pre, tot, v_blk in staged:
            acc_h = acc_h + _sb_values(pre, suf, v_blk)
            suf = suf + tot
        acc.append(acc_h)
    o_ref[...] = jnp.where(lane < LANES // 2, acc[0], acc[1]).astype(o_ref.dtype)


def _sb_sample(q, k, v, past_k, past_v, bsz, t_len):
    n, width = q.shape
    past_len = past_k.shape[0] // bsz
    blk = SB_BLOCK if past_len % SB_BLOCK == 0 else past_len
    groups = width // LANES
    new = pl.BlockSpec((t_len, LANES), lambda b, g: (b, g))
    past = pl.BlockSpec((past_len, LANES), lambda b, g: (b, g))
    return pl.pallas_call(
        functools.partial(_sb_sample_kernel, t_len, past_len, blk),
        grid=(bsz, groups),
        in_specs=[new, new, new, past, past],
        out_specs=new,
        out_shape=jax.ShapeDtypeStruct((n, width), BF16),
        compiler_params=_cparams("parallel", "parallel"),
        name="sb_sample",
    )(q, k, v, past_k, past_v)


def _first_argmax(vals):
    best_v, best_i = vals[0], jnp.zeros(vals[0].shape, I32)
    for i in range(1, len(vals)):
        better = vals[i] > best_v
        best_v = jnp.where(better, vals[i], best_v)
        best_i = jnp.where(better, i, best_i)
    return best_v, best_i


def _route_rows(logits_t, bias_col):
    s = _sigmoid(logits_t)
    sel = s + bias_col
    epg = EXPERTS_PER_GROUP
    scores = []
    for gi in range(N_GROUPS):
        rows = [sel[gi * epg + i:gi * epg + i + 1, :] for i in range(epg)]
        pair_sums = [rows[i] + rows[j] for i in range(epg) for j in range(i + 1, epg)]
        scores.append(functools.reduce(jnp.maximum, pair_sums))
    _, best = _first_argmax(scores)

    def in_best(mat, i):
        out = mat[i:i + 1, :]
        for gi in range(1, N_GROUPS):
            out = jnp.where(best == gi, mat[gi * epg + i:gi * epg + i + 1, :], out)
        return out

    sel_g = [in_best(sel, i) for i in range(epg)]
    s_g = [in_best(s, i) for i in range(epg)]
    _, i1 = _first_argmax(sel_g)
    _, i2 = _first_argmax([jnp.where(i1 == i, -jnp.inf, sel_g[i]) for i in range(epg)])
    lo = jnp.minimum(i1, i2)
    hi = jnp.maximum(i1, i2)
    pick = lambda idx: functools.reduce(lambda acc, i: jnp.where(idx == i, s_g[i], acc), range(1, epg), s_g[0])
    w_lo, w_hi = pick(lo), pick(hi)
    tot = w_lo + w_hi
    pair = jnp.where(lo == 0, 0, jnp.where(lo == 1, epg - 1, 2 * epg - 3)) + (hi - lo - 1)
    return best * N_PAIRS + pair, w_lo / tot, w_hi / tot


def _proj_route_kernel(a_ref, w_ref, x_ref, g1_ref, gain_ref, sh_ref, sc_ref, wr_hi_ref, wr_lo_ref, br_ref,
                       xn_ref, h2_ref, ri_ref):
    y = jnp.dot(a_ref[...], w_ref[...], preferred_element_type=F32)
    xn = x_ref[...] + g1_ref[...] * y
    xn_ref[...] = xn
    d = xn.shape[1]
    h2 = _rms_mod(xn, gain_ref[...], sh_ref[...], sc_ref[...])
    h2_ref[:, 0:d] = h2
    h_hi = h2.astype(BF16)
    h_lo = (h2 - h_hi.astype(F32)).astype(BF16)
    logits = (jnp.dot(h_hi, wr_hi_ref[...], preferred_element_type=F32)
              + jnp.dot(h_lo, wr_hi_ref[...], preferred_element_type=F32)
              + jnp.dot(h_hi, wr_lo_ref[...], preferred_element_type=F32))
    n_e = br_ref.shape[0]
    logits_t = logits.T[0:n_e, :]
    cls, w_lo, w_hi = _route_rows(logits_t, br_ref[...])
    tm = logits_t.shape[1]
    h2_ref[:, d:] = jnp.concatenate([w_lo, w_hi, jnp.zeros((LANES - 2, tm), F32)], axis=0).T
    ri_ref[...] = jnp.concatenate([cls, jnp.zeros((SUBLANES - 1, tm), I32)], axis=0)


def _proj_route(a, w_out, x, t_len, g1, gain, shift, scale, w_router, b_router):
    n, d = x.shape
    k = a.shape[1]
    n_e = w_router.shape[1]
    tm = _row_tile(n, t_len, ROW_TILE)
    g1_arr, g1_spec = _mod_operand(g1, n, t_len, tm)
    sh_arr, sh_spec = _mod_operand(shift, n, t_len, tm)
    sc_arr, sc_spec = _mod_operand(scale, n, t_len, tm)
    wr = jnp.pad(w_router.astype(F32), ((0, 0), (0, LANES - n_e)))
    wr_hi = wr.astype(BF16)
    wr_lo = (wr - wr_hi.astype(F32)).astype(BF16)
    tokd = pl.BlockSpec((tm, d), lambda i: (i, 0))
    const = lambda r, c: pl.BlockSpec((r, c), lambda i: (0, 0))
    route = pl.BlockSpec((SUBLANES, tm), lambda i: (0, i))
    return pl.pallas_call(
        _proj_route_kernel,
        grid=(n // tm,),
        in_specs=[pl.BlockSpec((tm, k), lambda i: (i, 0)), const(k, d), tokd, g1_spec(0), const(1, d),
                  sh_spec(0), sc_spec(0), const(d, LANES), const(d, LANES), const(n_e, 1)],
        out_specs=[tokd, pl.BlockSpec((tm, d + LANES), lambda i: (i, 0)), route],
        out_shape=[jax.ShapeDtypeStruct((n, d), F32), jax.ShapeDtypeStruct((n, d + LANES), F32),
                   jax.ShapeDtypeStruct((SUBLANES, n), I32)],
        compiler_params=_cparams("parallel"),
        name="proj_route",
    )(a, w_out, x, g1_arr, gain[None, :], sh_arr, sc_arr, wr_hi, wr_lo, b_router.astype(F32)[:, None])


def _gather_rows(src_hbm, idx_ref, base, n_rows, dst, sem):
    last = idx_ref.shape[0] - 1

    def issue(r, carry):
        row = idx_ref[jnp.minimum(base + r, last)]
        pltpu.make_async_copy(src_hbm.at[pl.ds(row, 1)], dst.at[pl.ds(r, 1)], sem).start()
        return carry
    lax.fori_loop(0, n_rows, issue, 0, unroll=8)


def _wait_rows(src_hbm, n_rows, dst, sem):
    pltpu.make_async_copy(src_hbm.at[pl.ds(0, n_rows)], dst, sem).wait()


def _moe_kernel(bm, d, order_ref, base_ref, valid_ref, e_lo_ref, e_hi_ref, used_ref,
                h_hbm, w1a_ref, w3a_ref, w2a_ref, w1b_ref, w3b_ref, w2b_ref,
                y_ref, buf_ref, sem_ref):
    i = pl.program_id(0)
    n_used = used_ref[0]
    slot = i % 2

    @pl.when((i == 0) & (n_used > 0))
    def _():
        _gather_rows(h_hbm, order_ref, base_ref[0], bm, buf_ref.at[0], sem_ref.at[0])

    @pl.when(i + 1 < n_used)
    def _():
        _gather_rows(h_hbm, order_ref, base_ref[i + 1], bm, buf_ref.at[1 - slot], sem_ref.at[1 - slot])

    @pl.when(i < n_used)
    def _():
        _wait_rows(h_hbm, bm, buf_ref.at[slot], sem_ref.at[slot])
        xb = buf_ref[slot, :, 0:d].astype(BF16)
        live = lax.broadcasted_iota(I32, (bm, LANES), 0) < valid_ref[i]
        wts = jnp.where(live, buf_ref[slot, :, d:], 0.0)

        def expert(w1_ref, w3_ref, w2_ref):
            u = jnp.dot(xb, w1_ref[...], preferred_element_type=F32)
            t = jnp.dot(xb, w3_ref[...], preferred_element_type=F32)
            mid = (u * _sigmoid(u) * t).astype(BF16)
            return jnp.dot(mid, w2_ref[...], preferred_element_type=F32)

        y_ref[...] = (expert(w1a_ref, w3a_ref, w2a_ref) * wts[:, 0:1]
                      + expert(w1b_ref, w3b_ref, w2b_ref) * wts[:, 1:2])

    @pl.when(i >= n_used)
    def _():
        y_ref[...] = jnp.zeros_like(y_ref)


def _moe_sorted(h2w, order, blk_base, blk_valid, blk_lo, blk_hi, n_used, w1, w3, w2):
    n, dw = h2w.shape
    d = dw - LANES
    n_blocks = blk_base.shape[0]
    bm = MOE_BLOCK
    de = w1.shape[2]

    def wspec(which, r, c):
        def index(i, order, base, valid, lo, hi, used):
            blk = jnp.minimum(i, jnp.maximum(used[0] - 1, 0))
            return ((lo, hi)[which][blk], 0, 0)
        return pl.BlockSpec((None, r, c), index)

    return pl.pallas_call(
        functools.partial(_moe_kernel, bm, d),
        grid_spec=pltpu.PrefetchScalarGridSpec(
            num_scalar_prefetch=6,
            grid=(n_blocks,),
            in_specs=[pl.BlockSpec(memory_space=pl.ANY),
                      wspec(0, d, de), wspec(0, d, de), wspec(0, de, d),
                      wspec(1, d, de), wspec(1, d, de), wspec(1, de, d)],
            out_specs=pl.BlockSpec((bm, d), lambda i, *_: (i, 0)),
            scratch_shapes=[pltpu.VMEM((2, bm, dw), F32), pltpu.SemaphoreType.DMA((2,))]),
        out_shape=jax.ShapeDtypeStruct((n_blocks * bm, d), F32),
        compiler_params=_cparams("arbitrary"),
        name="moe_sorted",
    )(order, blk_base, blk_valid, blk_lo, blk_hi, n_used, h2w, w1, w3, w2, w1, w3, w2)


def _unsort_kernel(tm, final_norm, pos_ref, y_hbm, x_ref, g2_ref, gain_ref, o_ref, buf_ref, sem_ref):
    i = pl.program_id(0)
    n_i = pl.num_programs(0)
    slot = i % 2

    @pl.when(i == 0)
    def _():
        _gather_rows(y_hbm, pos_ref, 0, tm, buf_ref.at[0], sem_ref.at[0])

    @pl.when(i + 1 < n_i)
    def _():
        _gather_rows(y_hbm, pos_ref, (i + 1) * tm, tm, buf_ref.at[1 - slot], sem_ref.at[1 - slot])

    _wait_rows(y_hbm, tm, buf_ref.at[slot], sem_ref.at[slot])
    x = x_ref[...] + g2_ref[...] * buf_ref[slot]
    if final_norm:
        ms = jnp.mean(x * x, axis=-1, keepdims=True)
        x = x * lax.rsqrt(ms + EPS) * gain_ref[...]
    o_ref[...] = x


def _unsort_resid(y_sorted, pos, x, t_len, g2, final_gain):
    n, d = x.shape
    tm = _row_tile(n, t_len, GATHER_TILE)
    g2_arr, g2_spec = _mod_operand(g2, n, t_len, tm)
    final_norm = final_gain is not None
    gain = (final_gain if final_norm else jnp.ones((d,), F32)).astype(F32)[None, :]
    return pl.pallas_call(
        functools.partial(_unsort_kernel, tm, final_norm),
        grid_spec=pltpu.PrefetchScalarGridSpec(
            num_scalar_prefetch=1,
            grid=(n // tm,),
            in_specs=[pl.BlockSpec(memory_space=pl.ANY),
                      pl.BlockSpec((tm, d), lambda i, pos: (i, 0)),
                      g2_spec(0),
                      pl.BlockSpec((1, d), lambda i, pos: (0, 0))],
            out_specs=pl.BlockSpec((tm, d), lambda i, pos: (i, 0)),
            scratch_shapes=[pltpu.VMEM((2, tm, d), F32), pltpu.SemaphoreType.DMA((2,))]),
        out_shape=jax.ShapeDtypeStruct((n, d), F32),
        compiler_params=_cparams("arbitrary"),
        name="unsort_resid",
    )(pos, y_sorted, x, g2_arr, gain)


def _sort_plan(cls):
    n = cls.shape[0]
    bm = MOE_BLOCK
    n_blocks = -(-n // bm) + N_CLASSES
    classes = jnp.arange(N_CLASSES, dtype=I32)
    cls_sorted, order = lax.sort((cls, jnp.arange(n, dtype=I32)), num_keys=1, is_stable=True)
    counts = jnp.sum((cls[:, None] == classes[None, :]).astype(I32), axis=0)
    padded = (counts + bm - 1) // bm * bm
    pad_end = jnp.cumsum(padded)
    pad_start = pad_end - padded
    start = jnp.cumsum(counts) - counts
    shift = jnp.sum(jnp.where(cls_sorted[:, None] == classes[None, :], (pad_start - start)[None, :], 0), axis=1)
    dest = jnp.arange(n, dtype=I32) + shift
    _, pos = lax.sort((order, dest), num_keys=1)
    blk_first = jnp.arange(n_blocks, dtype=I32) * bm
    blk_cls = jnp.minimum(jnp.sum((pad_end[None, :] <= blk_first[:, None]).astype(I32), axis=1), N_CLASSES - 1)
    onehot = blk_cls[:, None] == classes[None, :]
    pick = lambda table: jnp.sum(jnp.where(onehot, table[None, :], 0), axis=1)
    into = blk_first - pick(pad_start)
    blk_base = pick(start) + into
    blk_valid = jnp.clip(pick(counts) - into, 0, bm)
    pair_lo = jnp.array([i for i in range(EXPERTS_PER_GROUP) for j in range(i + 1, EXPERTS_PER_GROUP)], I32)
    pair_hi = jnp.array([j for i in range(EXPERTS_PER_GROUP) for j in range(i + 1, EXPERTS_PER_GROUP)], I32)
    group = classes // N_PAIRS
    blk_lo = pick(group * EXPERTS_PER_GROUP + pair_lo[classes % N_PAIRS])
    blk_hi = pick(group * EXPERTS_PER_GROUP + pair_hi[classes % N_PAIRS])
    n_used = (pad_end[-1] // bm).astype(I32)[None]
    return order, pos, blk_base.astype(I32), blk_valid.astype(I32), blk_lo, blk_hi, n_used


def _moe_layer(h2w, route_i, x, t_len, g2, w1, w3, w2, final_gain):
    order, pos, blk_base, blk_valid, blk_lo, blk_hi, n_used = _sort_plan(route_i[0])
    y_sorted = _moe_sorted(h2w, order, blk_base, blk_valid, blk_lo, blk_hi, n_used, w1, w3, w2)
    return _unsort_resid(y_sorted, pos, x, t_len, g2, final_gain)


def _trunk(x3, c, p, past):
    bsz, t_len, d = x3.shape
    n = bsz * t_len
    x = x3.reshape(n, d)
    depth = p['w_ada'].shape[0]
    mod = _ada(c.astype(F32), p['w_ada'], p['b_ada']).reshape(depth, bsz, N_MOD, d)
    gla_s, lru_s, conv_s, ks, vs = [], [], [], [], []
    for l in range(depth):
        sh1, sc1, g1, sh2, sc2, g2 = [mod[l, :, i] for i in range(N_MOD)]
        j = l // 2
        if l % 2 == 0:
            w_in = p['w_ab_in'][j]
            dqk2 = 2 * p['w_gk2'].shape[2]
            dv = GLA_HEADS * p['gla_norm'].shape[1]
            rank = p['w_gk2'].shape[1]
            w_lru = p['lru_lam'].shape[1]
            cuts = [0, dqk2, dqk2 + dv, dqk2 + 2 * dv, dqk2 + 2 * dv + rank,
                    dqk2 + 2 * dv + rank + w_lru, dqk2 + 2 * dv + rank + 2 * w_lru]
            cols = [w_in[:, cuts[i]:cuts[i + 1]] for i in range(6)]
            cols[3] = jnp.pad(cols[3], ((0, 0), (0, LANES - rank)))
            proj = _norm_proj(x, t_len, p['norm_mix'][l], sh1, sc1,
                              [(w.astype(BF16), 1.0, [F32]) for w in cols])
            if past is None:
                s0 = jnp.zeros((bsz, GLA_HEADS, dqk2 // 2 // GLA_HEADS, dv // GLA_HEADS), F32)
                h0 = jnp.zeros((bsz, w_lru), F32)
                buf = jnp.zeros((bsz, CONV_WIDTH - 1, w_lru), F32)
            else:
                s0, h0, buf = past['state_gla'][j], past['state_lru'][j], past['state_conv'][j]
            mix, sg, sl, sc = _ab_mixer(proj, bsz, t_len, past is None, s0, h0, buf, p['w_gk2'][j], p['b_gk'][j],
                                        p['gla_norm'][j], p['conv_w'][j], p['conv_b'][j], p['lru_wa'][j],
                                        p['lru_ba'][j], p['lru_wi'][j], p['lru_bi'][j], p['lru_lam'][j])
            gla_s.append(sg)
            lru_s.append(sl)
            conv_s.append(sc)
            w_out = p['w_ab_out'][j]
        else:
            w_qkv = p['w_sb_qkv'][j]
            width = w_qkv.shape[1] // 3
            heads = past['cache_k'].shape[3] if past is not None else p['sb_heads']
            dh = width // heads
            pieces = [(w_qkv[:, 0:width].astype(BF16), dh ** -0.5, [BF16]),
                      (w_qkv[:, width:2 * width].astype(BF16), 1.0, [F32, BF16]),
                      (w_qkv[:, 2 * width:].astype(BF16), 1.0, [F32, BF16])]
            q_b, k_f, k_b, v_f, v_b = _norm_proj(x, t_len, p['norm_mix'][l], sh1, sc1, pieces)
            if past is None:
                mix = _sb_prompt(q_b, k_b, v_b, bsz, t_len)
            else:
                pk = past['cache_k'][j].reshape(-1, width)
                pv = past['cache_v'][j].reshape(-1, width)
                mix = _sb_sample(q_b, k_b, v_b, pk, pv, bsz, t_len)
            ks.append(k_f.reshape(bsz, t_len, heads, dh))
            vs.append(v_f.reshape(bsz, t_len, heads, dh))
            w_out = p['w_sb_out'][j]
        xn, h2w, route_i = _proj_route(mix, w_out.astype(BF16), x, t_len, g1, p['norm_ffn'][l], sh2, sc2,
                                       p['w_router'], p['b_router'])
        x = _moe_layer(h2w, route_i, xn, t_len, g2,
                       p['w_e1'][l].astype(BF16), p['w_e3'][l].astype(BF16), p['w_e2'][l].astype(BF16),
                       p['norm_out'] if l == depth - 1 else None)
    return x.reshape(bsz, t_len, d), (jnp.stack(gla_s), jnp.stack(lru_s), jnp.stack(conv_s),
                                      jnp.stack(ks), jnp.stack(vs))


def kernel(x_prompt, x_sample, state_gla, state_lru, state_conv, cache_k, cache_v, c_prompt, c_sample,
           w_ada, b_ada, norm_mix, norm_ffn, norm_out, w_ab_in, w_gk2, b_gk, gla_norm, conv_w, conv_b,
           lru_wa, lru_ba, lru_wi, lru_bi, lru_lam, w_ab_out, w_sb_qkv, w_sb_out, w_router, b_router,
           w_e1, w_e3, w_e2):
    p = dict(w_ada=w_ada, b_ada=b_ada, norm_mix=norm_mix, norm_ffn=norm_ffn, norm_out=norm_out,
             w_ab_in=w_ab_in, w_gk2=w_gk2, b_gk=b_gk, gla_norm=gla_norm, conv_w=conv_w, conv_b=conv_b,
             lru_wa=lru_wa, lru_ba=lru_ba, lru_wi=lru_wi, lru_bi=lru_bi, lru_lam=lru_lam, w_ab_out=w_ab_out,
             w_sb_qkv=w_sb_qkv, w_sb_out=w_sb_out, w_router=w_router, b_router=b_router,
             w_e1=w_e1, w_e3=w_e3, w_e2=w_e2, sb_heads=cache_k.shape[3])
    past = dict(state_gla=state_gla, state_lru=state_lru, state_conv=state_conv, cache_k=cache_k, cache_v=cache_v)
    y_prompt, (p_gla, p_lru, p_conv, p_k, p_v) = _trunk(x_prompt, c_prompt, p, None)
    y_sample, (s_gla, s_lru, s_conv, s_k, s_v) = _trunk(x_sample, c_sample, p, past)
    return (y_prompt, y_sample, p_gla, p_lru, p_conv, p_k, p_v, s_gla, s_lru, s_conv, s_k, s_v)
```

```python
import functools

import jax
import jax.numpy as jnp
from jax import lax
from jax.experimental import pallas as pl
from jax.experimental.pallas import tpu as pltpu

F32 = jnp.float32
BF16 = jnp.bfloat16
I32 = jnp.int32

EPS = 1e-6
N_MOD = 6
GLA_HEADS = 4
GLA_GATE_TAU = 16.0
LRU_C = 8.0
CONV_WIDTH = 4
N_GROUPS = 4
EXPERTS_PER_GROUP = 4
N_PAIRS = 6
N_CLASSES = N_GROUPS * N_PAIRS

LANES = 128
SUBLANES = 8
VMEM_LIMIT = 56 * 1024 * 1024

ROW_TILE = 512
MIX_TILE = 256
GLA_CHUNK = 64
SB_BLOCK = 256
MOE_BLOCK = 256
GATHER_TILE = 256


def _cparams(*sem):
    return pltpu.CompilerParams(dimension_semantics=sem, vmem_limit_bytes=VMEM_LIMIT)


def _log_sigmoid(z):
    return jnp.minimum(z, 0.0) - jnp.log(1.0 + jnp.exp(-jnp.abs(z)))


def _softplus(z):
    return jnp.maximum(z, 0.0) + jnp.log1p(jnp.exp(-jnp.abs(z)))


def _sigmoid(z):
    return 1.0 / (1.0 + jnp.exp(-z))


def _rms_mod(x, gain, shift, scale):
    ms = jnp.mean(x * x, axis=-1, keepdims=True)
    y = x * lax.rsqrt(ms + EPS) * gain
    return y * (1.0 + scale) + shift


def _row_tile(n_rows, t_len, target):
    if t_len % target == 0:
        return target
    return n_rows


def _mod_operand(mod, n_rows, t_len, tile):
    d = mod.shape[-1]
    if t_len % tile == 0:
        per_seq = t_len // tile
        return mod[:, None, :], (lambda nidx: pl.BlockSpec((None, 1, d), lambda *i: (i[nidx] // per_seq, 0, 0)))
    rows = jnp.repeat(mod, t_len, axis=0)
    return rows, (lambda nidx: pl.BlockSpec((tile, d), lambda *i: (i[nidx], 0)))


def _ada_kernel(c_ref, w_ref, b_ref, o_ref):
    c = c_ref[...]
    cond = c * _sigmoid(c)
    o_ref[...] = jnp.dot(cond.astype(BF16), w_ref[...].astype(BF16), preferred_element_type=F32) + b_ref[...]


def _ada(c, w_ada, b_ada):
    depth, d, e = w_ada.shape
    b = c.shape[0]
    tn = d
    return pl.pallas_call(
        _ada_kernel,
        grid=(depth, e // tn),
        in_specs=[pl.BlockSpec((b, d), lambda l, j: (0, 0)),
                  pl.BlockSpec((None, d, tn), lambda l, j: (l, 0, j)),
                  pl.BlockSpec((None, 1, tn), lambda l, j: (l, 0, j))],
        out_specs=pl.BlockSpec((None, b, tn), lambda l, j: (l, 0, j)),
        out_shape=jax.ShapeDtypeStruct((depth, b, e), F32),
        compiler_params=_cparams("parallel", "parallel"),
        name="ada_mod",
    )(c, w_ada, b_ada[:, None, :])


def _norm_proj_kernel(out_plan, x_ref, gain_ref, sh_ref, sc_ref, *refs):
    n_w = len(out_plan)
    w_refs, o_refs = refs[:n_w], refs[n_w:]
    h = _rms_mod(x_ref[...], gain_ref[...], sh_ref[...], sc_ref[...]).astype(BF16)
    k = 0
    for w_ref, (scale, dtypes) in zip(w_refs, out_plan):
        y = jnp.dot(h, w_ref[...], preferred_element_type=F32)
        if scale != 1.0:
            y = y * scale
        for dt in dtypes:
            o_refs[k][...] = y.astype(dt)
            k += 1


def _norm_proj(x, t_len, gain, shift, scale, pieces):
    n, d = x.shape
    tm = _row_tile(n, t_len, ROW_TILE)
    sh_arr, sh_spec = _mod_operand(shift, n, t_len, tm)
    sc_arr, sc_spec = _mod_operand(scale, n, t_len, tm)
    in_specs = [pl.BlockSpec((tm, d), lambda i: (i, 0)),
                pl.BlockSpec((1, d), lambda i: (0, 0)),
                sh_spec(0), sc_spec(0)]
    out_specs, out_shapes, plan, weights = [], [], [], []
    for w, s, dtypes in pieces:
        e = w.shape[1]
        in_specs.append(pl.BlockSpec((d, e), lambda i: (0, 0)))
        weights.append(w)
        plan.append((s, tuple(dtypes)))
        for dt in dtypes:
            out_specs.append(pl.BlockSpec((tm, e), lambda i: (i, 0)))
            out_shapes.append(jax.ShapeDtypeStruct((n, e), dt))
    return pl.pallas_call(
        functools.partial(_norm_proj_kernel, tuple(plan)),
        grid=(n // tm,),
        in_specs=in_specs, out_specs=out_specs, out_shape=out_shapes,
        compiler_params=_cparams("parallel"),
        name="norm_proj",
    )(x, gain[None, :], sh_arr, sc_arr, *weights)


def _shift_rows(x, s, fill):
    rows = lax.broadcasted_iota(I32, x.shape, 0)
    return jnp.where(rows >= s, pltpu.roll(x, s, axis=0), fill)


def _cumsum_rows(x):
    n = x.shape[0]
    s = 1
    while s < n:
        x = x + _shift_rows(x, s, 0.0)
        s *= 2
    return x


def _linear_scan_rows(a, b):
    n = a.shape[0]
    s = 1
    while s < n:
        b = a * _shift_rows(b, s, 0.0) + b
        a = a * _shift_rows(a, s, 1.0)
        s *= 2
    return a, b


def _gelu_tanh(x):
    return 0.5 * x * (1.0 + jnp.tanh(0.7978845608028654 * (x + 0.044715 * (x * x * x))))


def _ab_mixer_kernel(first_pos_zero, tt, chunk,
                     qk_ref, v_ref, g_ref, gk_ref, xr_ref, xg_ref, s0_ref, h0_ref, buf0_ref,
                     wgk_ref, bgk_ref, gnorm_ref, cw_ref, cb_ref, wa_ref, wi_ref, ba_ref, bi_ref, lam_ref,
                     mix_ref, s_out_ref, h_out_ref, buf_out_ref,
                     st_ref, hc_ref, xpad_ref):
    ti = pl.program_id(1)
    n_t = pl.num_programs(1)
    dqk = qk_ref.shape[1] // 2
    dv = v_ref.shape[1]
    hk = dqk // GLA_HEADS
    hv = dv // GLA_HEADS
    w_lru = xr_ref.shape[1]

    @pl.when(ti == 0)
    def _():
        st_ref[...] = s0_ref[...]
        hc_ref[...] = h0_ref[...]
        xpad_ref[0:SUBLANES, :] = buf0_ref[...]

    xpad_ref[SUBLANES:SUBLANES + tt, :] = xr_ref[...]
    xc = cb_ref[...]
    for i in range(CONV_WIDTH):
        off = SUBLANES - (CONV_WIDTH - 1) + i
        xc = xc + cw_ref[i:i + 1, :] * xpad_ref[off:off + tt, :]
    tail = xpad_ref[tt:tt + SUBLANES, :]
    xpad_ref[0:SUBLANES, :] = tail
    xc_b = xc.astype(BF16)
    half = w_lru // 2
    r_lin = jnp.concatenate([jnp.dot(xc_b[:, j * half:(j + 1) * half], wa_ref[j], preferred_element_type=F32)
                             for j in range(2)], axis=-1)
    i_lin = jnp.concatenate([jnp.dot(xc_b[:, j * half:(j + 1) * half], wi_ref[j], preferred_element_type=F32)
                             for j in range(2)], axis=-1)
    r = _sigmoid(r_lin + ba_ref[...])
    i_g = _sigmoid(i_lin + bi_ref[...])
    log_at = (-LRU_C) * r * _softplus(-lam_ref[...])
    a = jnp.exp(log_at)
    mult = jnp.sqrt(1.0 - a * a)
    if first_pos_zero:
        rows = lax.broadcasted_iota(I32, mult.shape, 0)
        mult = jnp.where((rows == 0) & (ti == 0), 1.0, mult)
    bterm = mult * (i_g * xc)
    a_cum, h_loc = _linear_scan_rows(a, bterm)
    hs = a_cum * hc_ref[...] + h_loc
    hc_ref[...] = hs[tt - 1:tt, :]
    y_b = hs * _gelu_tanh(xg_ref[...])
    mix_ref[:, dv:] = y_b.astype(BF16)

    lane_k = lax.broadcasted_iota(I32, (1, dqk), 1)
    st_rows = lax.broadcasted_iota(I32, (dv, dqk), 0)
    st_cols = lax.broadcasted_iota(I32, (dv, dqk), 1)
    st_mask = functools.reduce(
        jnp.logical_or,
        [(st_rows >= h * hv) & (st_rows < (h + 1) * hv) & (st_cols >= h * hk) & (st_cols < (h + 1) * hk)
         for h in range(GLA_HEADS)])
    crow = lax.broadcasted_iota(I32, (chunk, chunk), 0)
    ccol = lax.broadcasted_iota(I32, (chunk, chunk), 1)
    causal = ccol <= crow
    mid = chunk // 2 - 1
    for c in range(tt // chunk):
        rs = slice(c * chunk, (c + 1) * chunk)
        q = qk_ref[rs, 0:dqk] * (hk ** -0.5)
        k = qk_ref[rs, dqk:2 * dqk]
        v_b = v_ref[rs, :].astype(BF16)
        u = jnp.dot(gk_ref[rs, :].astype(BF16), wgk_ref[...], preferred_element_type=F32) + bgk_ref[...]
        b = _cumsum_rows(_log_sigmoid(u) * (1.0 / GLA_GATE_TAU))
        b_mid = b[mid:mid + 1, :]
        b_last = b[chunk - 1:chunk, :]
        q_e = q * jnp.exp(b - b_mid)
        k_e = (k * jnp.exp(b_mid - b)).astype(BF16)
        k_tail = (k * jnp.exp(b_last - b)).astype(BF16)
        q_dec = (q_e * jnp.exp(b_mid)).astype(BF16)
        q_e = q_e.astype(BF16)
        st = st_ref[...]
        o = lax.dot_general(q_dec, st.astype(BF16), (((1,), (1,)), ((), ())), preferred_element_type=F32)
        o_intra = []
        for h in range(GLA_HEADS):
            q_h = jnp.where((lane_k >= h * hk) & (lane_k < (h + 1) * hk), q_e, jnp.zeros_like(q_e))
            att = lax.dot_general(q_h, k_e, (((1,), (1,)), ((), ())), preferred_element_type=F32)
            att = jnp.where(causal, att, 0.0).astype(BF16)
            o_intra.append(jnp.dot(att, v_b[:, h * hv:(h + 1) * hv], preferred_element_type=F32))
        o = o + jnp.concatenate(o_intra, axis=-1)
        kv = lax.dot_general(v_b, k_tail, (((0,), (0,)), ((), ())), preferred_element_type=F32)
        st_ref[...] = st * jnp.exp(b_last) + jnp.where(st_mask, kv, 0.0)
        g = g_ref[rs, :]
        gate = g * _sigmoid(g)
        y_a = []
        for h in range(GLA_HEADS):
            o_h = o[:, h * hv:(h + 1) * hv]
            ms = jnp.mean(o_h * o_h, axis=-1, keepdims=True)
            y_a.append(o_h * lax.rsqrt(ms + EPS) * gnorm_ref[...] * gate[:, h * hv:(h + 1) * hv])
        mix_ref[rs, 0:dv] = jnp.concatenate(y_a, axis=-1).astype(BF16)

    @pl.when(ti == n_t - 1)
    def _():
        s_out_ref[...] = st_ref[...]
        h_out_ref[...] = hc_ref[...]
        buf_out_ref[...] = xpad_ref[0:SUBLANES, :]


def _ab_mixer(proj, bsz, t_len, first_pos_zero, s_gla, h_lru, conv_buf, w_gk2, b_gk, gla_norm, conv_w, conv_b,
              lru_wa, lru_ba, lru_wi, lru_bi, lru_lam):
    qk, v, g, gk, xr, xg = proj
    n = qk.shape[0]
    dqk, dv, w_lru = qk.shape[1] // 2, v.shape[1], xr.shape[1]
    hk, hv = dqk // GLA_HEADS, dv // GLA_HEADS
    tt = MIX_TILE if t_len % MIX_TILE == 0 else t_len
    chunk = GLA_CHUNK if tt % GLA_CHUNK == 0 else tt
    n_t = t_len // tt

    eye = jnp.eye(GLA_HEADS, dtype=F32)
    st0 = jnp.einsum('bhkv,hg->bhvgk', s_gla.astype(F32), eye).reshape(bsz, dv, dqk)
    buf0 = jnp.pad(conv_buf.astype(F32), ((0, 0), (SUBLANES - (CONV_WIDTH - 1), 0), (0, 0)))
    wgk = jnp.pad(w_gk2, ((0, gk.shape[1] - w_gk2.shape[0]), (0, 0))).astype(BF16)
    nb = lru_wa.shape[0]
    half_blocks = nb // 2

    def block_diag(w):
        bd = w.shape[1]
        e = jnp.eye(half_blocks, dtype=w.dtype)
        w2 = w.reshape(2, half_blocks, bd, bd)
        return jnp.einsum('jnio,nm->jnimo', w2, e).reshape(2, half_blocks * bd, half_blocks * bd).astype(BF16)

    row = lambda z: z.astype(F32)[None, :]
    tok = lambda w: pl.BlockSpec((tt, w), lambda b, i: (b * n_t + i, 0))
    per_b = lambda r, c: pl.BlockSpec((None, r, c), lambda b, i: (b, 0, 0))
    full2 = lambda r, c: pl.BlockSpec((r, c), lambda b, i: (0, 0))
    full3 = lambda a, r, c: pl.BlockSpec((a, r, c), lambda b, i: (0, 0, 0))
    mix, st, h_new, buf = pl.pallas_call(
        functools.partial(_ab_mixer_kernel, first_pos_zero, tt, chunk),
        grid=(bsz, n_t),
        in_specs=[tok(2 * dqk), tok(dv), tok(dv), tok(gk.shape[1]), tok(w_lru), tok(w_lru),
                  per_b(dv, dqk), per_b(1, w_lru), per_b(SUBLANES, w_lru),
                  full2(gk.shape[1], dqk), full2(1, dqk), full2(1, hv), full2(CONV_WIDTH, w_lru), full2(1, w_lru),
                  full3(2, w_lru // 2, w_lru // 2), full3(2, w_lru // 2, w_lru // 2),
                  full2(1, w_lru), full2(1, w_lru), full2(1, w_lru)],
        out_specs=[tok(dv + w_lru), per_b(dv, dqk), per_b(1, w_lru), per_b(SUBLANES, w_lru)],
        out_shape=[jax.ShapeDtypeStruct((n, dv + w_lru), BF16),
                   jax.ShapeDtypeStruct((bsz, dv, dqk), F32),
                   jax.ShapeDtypeStruct((bsz, 1, w_lru), F32),
                   jax.ShapeDtypeStruct((bsz, SUBLANES, w_lru), F32)],
        scratch_shapes=[pltpu.VMEM((dv, dqk), F32), pltpu.VMEM((1, w_lru), F32),
                        pltpu.VMEM((tt + SUBLANES, w_lru), F32)],
        compiler_params=_cparams("parallel", "arbitrary"),
        name="ab_mixer",
    )(qk, v, g, gk, xr, xg, st0, h_lru.astype(F32)[:, None, :], buf0,
      wgk, row(b_gk), row(gla_norm), conv_w.astype(F32), row(conv_b),
      block_diag(lru_wa), block_diag(lru_wi), row(lru_ba), row(lru_bi), row(lru_lam))
    st5 = st.reshape(bsz, GLA_HEADS, hv, GLA_HEADS, hk)
    s_new = jnp.stack([st5[:, h, :, h, :] for h in range(GLA_HEADS)], axis=1).swapaxes(-1, -2)
    return mix, s_new, h_new[:, 0], buf[:, SUBLANES - (CONV_WIDTH - 1):]


SB_MASKED = -1e30
SB_DEAD = -152.0
LOG2_E = 1.4426950408889634


def _sb_logits(q_h, k_blk):
    return lax.dot_general(q_h, k_blk, (((1,), (1,)), ((), ())), preferred_element_type=F32)


def _sb_gates(z, mask):
    neg_abs = pltpu.bitcast(pltpu.bitcast(z, I32) | jnp.int32(-2 ** 31), F32)
    ls = jnp.minimum(z, 0.0) - jnp.log2(1.0 + jnp.exp2(neg_abs))
    lk = ls - z
    if mask is not None:
        lk = jnp.where(mask, lk, 0.0)
    return ls, lk.astype(BF16), jnp.sum(lk, axis=-1, keepdims=True)


def _sb_prefix(lk, upper, suffix, mask):
    pre = jnp.dot(lk, upper, preferred_element_type=F32) + suffix
    if mask is not None:
        pre = jnp.where(mask, pre, SB_MASKED)
    return pre


def _sb_values(ls, pre, v_blk):
    return jnp.dot(jnp.exp2(ls + pre).astype(BF16), v_blk, preferred_element_type=F32)


def _split_heads(q):
    lane = lax.broadcasted_iota(I32, (1, LANES), 1)
    zero = jnp.zeros_like(q)
    return [jnp.where(lane < LANES // 2, q, zero), jnp.where(lane >= LANES // 2, q, zero)], lane


def _upper(n):
    r = lax.broadcasted_iota(I32, (n, n), 0)
    c = lax.broadcasted_iota(I32, (n, n), 1)
    return jnp.where(r > c, 1.0, 0.0).astype(BF16)


def _sb_prompt_kernel(blk, q_ref, k_ref, v_ref, o_ref, acc_ref, suf_ref):
    qi = pl.program_id(2)
    n_blk = qi + 1
    q_heads, lane = _split_heads(q_ref[...])
    upper = _upper(blk)
    diag = lax.broadcasted_iota(I32, (blk, blk), 1) < lax.broadcasted_iota(I32, (blk, blk), 0)

    def key_rows(s):
        return pl.ds(pl.multiple_of((qi - s) * blk, blk), blk)

    def block(s, mask):
        k_blk = k_ref[key_rows(s), :]
        v_blk = v_ref[key_rows(s), :]
        for h in range(2):
            ls, lk, tot = _sb_gates(_sb_logits(q_heads[h], k_blk), mask)
            acc_ref[h] += _sb_values(ls, _sb_prefix(lk, upper, suf_ref[h], mask), v_blk)
            suf_ref[h] += tot

    def live():
        return (jnp.max(suf_ref[...]) > SB_DEAD).astype(I32)

    acc_ref[...] = jnp.zeros_like(acc_ref)
    suf_ref[...] = jnp.zeros_like(suf_ref)
    block(0, diag)

    def more(carry):
        s, go = carry
        return (s < n_blk) & (go > 0)

    def step(carry):
        s, _ = carry
        block(s, None)
        return s + 1, live()

    lax.while_loop(more, step, (jnp.int32(1), live()))
    o_ref[...] = jnp.where(lane < LANES // 2, acc_ref[0], acc_ref[1]).astype(o_ref.dtype)


def _sb_prompt(q, k, v, bsz, t_len):
    n, width = q.shape
    blk = SB_BLOCK
    nq = t_len // blk
    groups = width // LANES
    return pl.pallas_call(
        functools.partial(_sb_prompt_kernel, blk),
        grid=(bsz, groups, nq),
        in_specs=[pl.BlockSpec((blk, LANES), lambda b, g, i: (b * nq + i, g)),
                  pl.BlockSpec((t_len, LANES), lambda b, g, i: (b, g)),
                  pl.BlockSpec((t_len, LANES), lambda b, g, i: (b, g))],
        out_specs=pl.BlockSpec((blk, LANES), lambda b, g, i: (b * nq + i, g)),
        out_shape=jax.ShapeDtypeStruct((n, width), BF16),
        scratch_shapes=[pltpu.VMEM((2, blk, LANES), F32), pltpu.VMEM((2, blk, 1), F32)],
        compiler_params=_cparams("parallel", "parallel", "arbitrary"),
        name="sb_prompt",
    )(q, k, v)


def _sb_sample_kernel(t_new, past_len, blk, q_ref, k_ref, v_ref, pk_ref, pv_ref, o_ref):
    q_heads, lane = _split_heads(q_ref[...])
    r = lax.broadcasted_iota(I32, (t_new, t_new), 0)
    c = lax.broadcasted_iota(I32, (t_new, t_new), 1)
    upper = _upper(blk)
    blocks = [(k_ref[...], v_ref[...], _upper(t_new), c < r)]
    for j in reversed(range(past_len // blk)):
        rows = slice(j * blk, (j + 1) * blk)
        blocks.append((pk_ref[rows, :].astype(BF16), pv_ref[rows, :].astype(BF16), upper, None))
    acc = []
    for q_h in q_heads:
        staged = [_sb_gates(_sb_logits(q_h, k_blk), mask) for k_blk, _, _, mask in blocks]
        acc_h = jnp.zeros((t_new, LANES), F32)
        suf = jnp.zeros((t_new, 1), F32)
        for (ls, lk, tot), (_, v_blk, up, mask) in zip(staged, blocks):
            acc_h = acc_h + _sb_values(ls, _sb_prefix(lk, up, suf, mask), v_blk)
            suf = suf + tot
        acc.append(acc_h)
    o_ref[...] = jnp.where(lane < LANES // 2, acc[0], acc[1]).astype(o_ref.dtype)


def _sb_sample(q, k, v, past_k, past_v, bsz, t_len):
    n, width = q.shape
    past_len = past_k.shape[0] // bsz
    blk = SB_BLOCK if past_len % SB_BLOCK == 0 else past_len
    groups = width // LANES
    new = pl.BlockSpec((t_len, LANES), lambda b, g: (b, g))
    past = pl.BlockSpec((past_len, LANES), lambda b, g: (b, g))
    return pl.pallas_call(
        functools.partial(_sb_sample_kernel, t_len, past_len, blk),
        grid=(bsz, groups),
        in_specs=[new, new, new, past, past],
        out_specs=new,
        out_shape=jax.ShapeDtypeStruct((n, width), BF16),
        compiler_params=_cparams("parallel", "parallel"),
        name="sb_sample",
    )(q, k, v, past_k, past_v)


def _first_argmax(vals):
    best_v, best_i = vals[0], jnp.zeros(vals[0].shape, I32)
    for i in range(1, len(vals)):
        better = vals[i] > best_v
        best_v = jnp.where(better, vals[i], best_v)
        best_i = jnp.where(better, i, best_i)
    return best_v, best_i


def _route_rows(logits_t, bias_col):
    s = _sigmoid(logits_t)
    sel = s + bias_col
    epg = EXPERTS_PER_GROUP
    scores = []
    for gi in range(N_GROUPS):
        rows = [sel[gi * epg + i:gi * epg + i + 1, :] for i in range(epg)]
        pair_sums = [rows[i] + rows[j] for i in range(epg) for j in range(i + 1, epg)]
        scores.append(functools.reduce(jnp.maximum, pair_sums))
    _, best = _first_argmax(scores)

    def in_best(mat, i):
        out = mat[i:i + 1, :]
        for gi in range(1, N_GROUPS):
            out = jnp.where(best == gi, mat[gi * epg + i:gi * epg + i + 1, :], out)
        return out

    sel_g = [in_best(sel, i) for i in range(epg)]
    s_g = [in_best(s, i) for i in range(epg)]
    _, i1 = _first_argmax(sel_g)
    _, i2 = _first_argmax([jnp.where(i1 == i, -jnp.inf, sel_g[i]) for i in range(epg)])
    lo = jnp.minimum(i1, i2)
    hi = jnp.maximum(i1, i2)
    pick = lambda idx: functools.reduce(lambda acc, i: jnp.where(idx == i, s_g[i], acc), range(1, epg), s_g[0])
    w_lo, w_hi = pick(lo), pick(hi)
    tot = w_lo + w_hi
    pair = jnp.where(lo == 0, 0, jnp.where(lo == 1, epg - 1, 2 * epg - 3)) + (hi - lo - 1)
    return best * N_PAIRS + pair, w_lo / tot, w_hi / tot


def _proj_route_kernel(a_ref, w_ref, x_ref, g1_ref, gain_ref, sh_ref, sc_ref, wr_hi_ref, wr_lo_ref, br_ref,
                       xn_ref, h2_ref, ri_ref):
    y = jnp.dot(a_ref[...], w_ref[...], preferred_element_type=F32)
    xn = x_ref[...] + g1_ref[...] * y
    xn_ref[...] = xn
    d = xn.shape[1]
    h2 = _rms_mod(xn, gain_ref[...], sh_ref[...], sc_ref[...])
    h2_ref[:, 0:d] = h2
    h_hi = h2.astype(BF16)
    h_lo = (h2 - h_hi.astype(F32)).astype(BF16)
    logits = (jnp.dot(h_hi, wr_hi_ref[...], preferred_element_type=F32)
              + jnp.dot(h_lo, wr_hi_ref[...], preferred_element_type=F32)
              + jnp.dot(h_hi, wr_lo_ref[...], preferred_element_type=F32))
    n_e = br_ref.shape[0]
    logits_t = logits.T[0:n_e, :]
    cls, w_lo, w_hi = _route_rows(logits_t, br_ref[...])
    tm = logits_t.shape[1]
    h2_ref[:, d:] = jnp.concatenate([w_lo, w_hi, jnp.zeros((LANES - 2, tm), F32)], axis=0).T
    ri_ref[...] = jnp.concatenate([cls, jnp.zeros((SUBLANES - 1, tm), I32)], axis=0)


def _proj_route(a, w_out, x, t_len, g1, gain, shift, scale, w_router, b_router):
    n, d = x.shape
    k = a.shape[1]
    n_e = w_router.shape[1]
    tm = _row_tile(n, t_len, ROW_TILE)
    g1_arr, g1_spec = _mod_operand(g1, n, t_len, tm)
    sh_arr, sh_spec = _mod_operand(shift, n, t_len, tm)
    sc_arr, sc_spec = _mod_operand(scale, n, t_len, tm)
    wr = jnp.pad(w_router.astype(F32), ((0, 0), (0, LANES - n_e)))
    wr_hi = wr.astype(BF16)
    wr_lo = (wr - wr_hi.astype(F32)).astype(BF16)
    tokd = pl.BlockSpec((tm, d), lambda i: (i, 0))
    const = lambda r, c: pl.BlockSpec((r, c), lambda i: (0, 0))
    route = pl.BlockSpec((SUBLANES, tm), lambda i: (0, i))
    return pl.pallas_call(
        _proj_route_kernel,
        grid=(n // tm,),
        in_specs=[pl.BlockSpec((tm, k), lambda i: (i, 0)), const(k, d), tokd, g1_spec(0), const(1, d),
                  sh_spec(0), sc_spec(0), const(d, LANES), const(d, LANES), const(n_e, 1)],
        out_specs=[tokd, pl.BlockSpec((tm, d + LANES), lambda i: (i, 0)), route],
        out_shape=[jax.ShapeDtypeStruct((n, d), F32), jax.ShapeDtypeStruct((n, d + LANES), F32),
                   jax.ShapeDtypeStruct((SUBLANES, n), I32)],
        compiler_params=_cparams("parallel"),
        name="proj_route",
    )(a, w_out, x, g1_arr, gain[None, :], sh_arr, sc_arr, wr_hi, wr_lo, b_router.astype(F32)[:, None])


def _gather_rows(src_hbm, idx_ref, base, n_rows, dst, sem):
    last = idx_ref.shape[0] - 1

    def issue(r, carry):
        row = idx_ref[jnp.minimum(base + r, last)]
        pltpu.make_async_copy(src_hbm.at[pl.ds(row, 1)], dst.at[pl.ds(r, 1)], sem).start()
        return carry
    lax.fori_loop(0, n_rows, issue, 0, unroll=8)


def _wait_rows(src_hbm, n_rows, dst, sem):
    pltpu.make_async_copy(src_hbm.at[pl.ds(0, n_rows)], dst, sem).wait()


def _moe_kernel(bm, d, order_ref, base_ref, valid_ref, e_lo_ref, e_hi_ref, used_ref,
                h_hbm, w1a_ref, w3a_ref, w2a_ref, w1b_ref, w3b_ref, w2b_ref,
                y_ref, buf_ref, sem_ref):
    i = pl.program_id(0)
    n_used = used_ref[0]
    slot = i % 2

    @pl.when((i == 0) & (n_used > 0))
    def _():
        _gather_rows(h_hbm, order_ref, base_ref[0], bm, buf_ref.at[0], sem_ref.at[0])

    @pl.when(i + 1 < n_used)
    def _():
        _gather_rows(h_hbm, order_ref, base_ref[i + 1], bm, buf_ref.at[1 - slot], sem_ref.at[1 - slot])

    @pl.when(i < n_used)
    def _():
        _wait_rows(h_hbm, bm, buf_ref.at[slot], sem_ref.at[slot])
        xb = buf_ref[slot, :, 0:d].astype(BF16)
        live = lax.broadcasted_iota(I32, (bm, LANES), 0) < valid_ref[i]
        wts = jnp.where(live, buf_ref[slot, :, d:], 0.0)

        def expert(w1_ref, w3_ref, w2_ref):
            u = jnp.dot(xb, w1_ref[...], preferred_element_type=F32)
            t = jnp.dot(xb, w3_ref[...], preferred_element_type=F32)
            mid = (u * _sigmoid(u) * t).astype(BF16)
            return jnp.dot(mid, w2_ref[...], preferred_element_type=F32)

        y_ref[...] = (expert(w1a_ref, w3a_ref, w2a_ref) * wts[:, 0:1]
                      + expert(w1b_ref, w3b_ref, w2b_ref) * wts[:, 1:2])

    @pl.when(i >= n_used)
    def _():
        y_ref[...] = jnp.zeros_like(y_ref)


def _moe_sorted(h2w, order, blk_base, blk_valid, blk_lo, blk_hi, n_used, w1, w3, w2):
    n, dw = h2w.shape
    d = dw - LANES
    n_blocks = blk_base.shape[0]
    bm = MOE_BLOCK
    de = w1.shape[2]

    def wspec(which, r, c):
        def index(i, order, base, valid, lo, hi, used):
            blk = jnp.minimum(i, jnp.maximum(used[0] - 1, 0))
            return ((lo, hi)[which][blk], 0, 0)
        return pl.BlockSpec((None, r, c), index)

    return pl.pallas_call(
        functools.partial(_moe_kernel, bm, d),
        grid_spec=pltpu.PrefetchScalarGridSpec(
            num_scalar_prefetch=6,
            grid=(n_blocks,),
            in_specs=[pl.BlockSpec(memory_space=pl.ANY),
                      wspec(0, d, de), wspec(0, d, de), wspec(0, de, d),
                      wspec(1, d, de), wspec(1, d, de), wspec(1, de, d)],
            out_specs=pl.BlockSpec((bm, d), lambda i, *_: (i, 0)),
            scratch_shapes=[pltpu.VMEM((2, bm, dw), F32), pltpu.SemaphoreType.DMA((2,))]),
        out_shape=jax.ShapeDtypeStruct((n_blocks * bm, d), F32),
        compiler_params=_cparams("arbitrary"),
        name="moe_sorted",
    )(order, blk_base, blk_valid, blk_lo, blk_hi, n_used, h2w, w1, w3, w2, w1, w3, w2)


def _unsort_kernel(tm, final_norm, pos_ref, y_hbm, x_ref, g2_ref, gain_ref, o_ref, buf_ref, sem_ref):
    i = pl.program_id(0)
    n_i = pl.num_programs(0)
    slot = i % 2

    @pl.when(i == 0)
    def _():
        _gather_rows(y_hbm, pos_ref, 0, tm, buf_ref.at[0], sem_ref.at[0])

    @pl.when(i + 1 < n_i)
    def _():
        _gather_rows(y_hbm, pos_ref, (i + 1) * tm, tm, buf_ref.at[1 - slot], sem_ref.at[1 - slot])

    _wait_rows(y_hbm, tm, buf_ref.at[slot], sem_ref.at[slot])
    x = x_ref[...] + g2_ref[...] * buf_ref[slot]
    if final_norm:
        ms = jnp.mean(x * x, axis=-1, keepdims=True)
        x = x * lax.rsqrt(ms + EPS) * gain_ref[...]
    o_ref[...] = x


def _unsort_resid(y_sorted, pos, x, t_len, g2, final_gain):
    n, d = x.shape
    tm = _row_tile(n, t_len, GATHER_TILE)
    g2_arr, g2_spec = _mod_operand(g2, n, t_len, tm)
    final_norm = final_gain is not None
    gain = (final_gain if final_norm else jnp.ones((d,), F32)).astype(F32)[None, :]
    return pl.pallas_call(
        functools.partial(_unsort_kernel, tm, final_norm),
        grid_spec=pltpu.PrefetchScalarGridSpec(
            num_scalar_prefetch=1,
            grid=(n // tm,),
            in_specs=[pl.BlockSpec(memory_space=pl.ANY),
                      pl.BlockSpec((tm, d), lambda i, pos: (i, 0)),
                      g2_spec(0),
                      pl.BlockSpec((1, d), lambda i, pos: (0, 0))],
            out_specs=pl.BlockSpec((tm, d), lambda i, pos: (i, 0)),
            scratch_shapes=[pltpu.VMEM((2, tm, d), F32), pltpu.SemaphoreType.DMA((2,))]),
        out_shape=jax.ShapeDtypeStruct((n, d), F32),
        compiler_params=_cparams("arbitrary"),
        name="unsort_resid",
    )(pos, y_sorted, x, g2_arr, gain)


def _sort_plan(cls):
    n = cls.shape[0]
    bm = MOE_BLOCK
    n_blocks = -(-n // bm) + N_CLASSES
    classes = jnp.arange(N_CLASSES, dtype=I32)
    cls_sorted, order = lax.sort((cls, jnp.arange(n, dtype=I32)), num_keys=1, is_stable=True)
    counts = jnp.sum((cls[:, None] == classes[None, :]).astype(I32), axis=0)
    padded = (counts + bm - 1) // bm * bm
    pad_end = jnp.cumsum(padded)
    pad_start = pad_end - padded
    start = jnp.cumsum(counts) - counts
    shift = jnp.sum(jnp.where(cls_sorted[:, None] == classes[None, :], (pad_start - start)[None, :], 0), axis=1)
    dest = jnp.arange(n, dtype=I32) + shift
    _, pos = lax.sort((order, dest), num_keys=1)
    blk_first = jnp.arange(n_blocks, dtype=I32) * bm
    blk_cls = jnp.minimum(jnp.sum((pad_end[None, :] <= blk_first[:, None]).astype(I32), axis=1), N_CLASSES - 1)
    onehot = blk_cls[:, None] == classes[None, :]
    pick = lambda table: jnp.sum(jnp.where(onehot, table[None, :], 0), axis=1)
    into = blk_first - pick(pad_start)
    blk_base = pick(start) + into
    blk_valid = jnp.clip(pick(counts) - into, 0, bm)
    pair_lo = jnp.array([i for i in range(EXPERTS_PER_GROUP) for j in range(i + 1, EXPERTS_PER_GROUP)], I32)
    pair_hi = jnp.array([j for i in range(EXPERTS_PER_GROUP) for j in range(i + 1, EXPERTS_PER_GROUP)], I32)
    group = classes // N_PAIRS
    blk_lo = pick(group * EXPERTS_PER_GROUP + pair_lo[classes % N_PAIRS])
    blk_hi = pick(group * EXPERTS_PER_GROUP + pair_hi[classes % N_PAIRS])
    n_used = (pad_end[-1] // bm).astype(I32)[None]
    return order, pos, blk_base.astype(I32), blk_valid.astype(I32), blk_lo, blk_hi, n_used


def _moe_layer(h2w, route_i, x, t_len, g2, w1, w3, w2, final_gain):
    order, pos, blk_base, blk_valid, blk_lo, blk_hi, n_used = _sort_plan(route_i[0])
    y_sorted = _moe_sorted(h2w, order, blk_base, blk_valid, blk_lo, blk_hi, n_used, w1, w3, w2)
    return _unsort_resid(y_sorted, pos, x, t_len, g2, final_gain)


def _trunk(x3, c, p, past):
    bsz, t_len, d = x3.shape
    n = bsz * t_len
    x = x3.reshape(n, d)
    depth = p['w_ada'].shape[0]
    mod = _ada(c.astype(F32), p['w_ada'], p['b_ada']).reshape(depth, bsz, N_MOD, d)
    gla_s, lru_s, conv_s, ks, vs = [], [], [], [], []
    for l in range(depth):
        sh1, sc1, g1, sh2, sc2, g2 = [mod[l, :, i] for i in range(N_MOD)]
        j = l // 2
        if l % 2 == 0:
            w_in = p['w_ab_in'][j]
            dqk2 = 2 * p['w_gk2'].shape[2]
            dv = GLA_HEADS * p['gla_norm'].shape[1]
            rank = p['w_gk2'].shape[1]
            w_lru = p['lru_lam'].shape[1]
            cuts = [0, dqk2, dqk2 + dv, dqk2 + 2 * dv, dqk2 + 2 * dv + rank,
                    dqk2 + 2 * dv + rank + w_lru, dqk2 + 2 * dv + rank + 2 * w_lru]
            cols = [w_in[:, cuts[i]:cuts[i + 1]] for i in range(6)]
            cols[3] = jnp.pad(cols[3], ((0, 0), (0, LANES - rank)))
            proj = _norm_proj(x, t_len, p['norm_mix'][l], sh1, sc1,
                              [(w.astype(BF16), 1.0, [F32]) for w in cols])
            if past is None:
                s0 = jnp.zeros((bsz, GLA_HEADS, dqk2 // 2 // GLA_HEADS, dv // GLA_HEADS), F32)
                h0 = jnp.zeros((bsz, w_lru), F32)
                buf = jnp.zeros((bsz, CONV_WIDTH - 1, w_lru), F32)
            else:
                s0, h0, buf = past['state_gla'][j], past['state_lru'][j], past['state_conv'][j]
            mix, sg, sl, sc = _ab_mixer(proj, bsz, t_len, past is None, s0, h0, buf, p['w_gk2'][j], p['b_gk'][j],
                                        p['gla_norm'][j], p['conv_w'][j], p['conv_b'][j], p['lru_wa'][j],
                                        p['lru_ba'][j], p['lru_wi'][j], p['lru_bi'][j], p['lru_lam'][j])
            gla_s.append(sg)
            lru_s.append(sl)
            conv_s.append(sc)
            w_out = p['w_ab_out'][j]
        else:
            w_qkv = p['w_sb_qkv'][j]
            width = w_qkv.shape[1] // 3
            heads = past['cache_k'].shape[3] if past is not None else p['sb_heads']
            dh = width // heads
            pieces = [(w_qkv[:, 0:width].astype(BF16), LOG2_E * dh ** -0.5, [BF16]),
                      (w_qkv[:, width:2 * width].astype(BF16), 1.0, [F32, BF16]),
                      (w_qkv[:, 2 * width:].astype(BF16), 1.0, [F32, BF16])]
            q_b, k_f, k_b, v_f, v_b = _norm_proj(x, t_len, p['norm_mix'][l], sh1, sc1, pieces)
            if past is None:
                mix = _sb_prompt(q_b, k_b, v_b, bsz, t_len)
            else:
                pk = past['cache_k'][j].reshape(-1, width)
                pv = past['cache_v'][j].reshape(-1, width)
                mix = _sb_sample(q_b, k_b, v_b, pk, pv, bsz, t_len)
            ks.append(k_f.reshape(bsz, t_len, heads, dh))
            vs.append(v_f.reshape(bsz, t_len, heads, dh))
            w_out = p['w_sb_out'][j]
        xn, h2w, route_i = _proj_route(mix, w_out.astype(BF16), x, t_len, g1, p['norm_ffn'][l], sh2, sc2,
                                       p['w_router'], p['b_router'])
        x = _moe_layer(h2w, route_i, xn, t_len, g2,
                       p['w_e1'][l].astype(BF16), p['w_e3'][l].astype(BF16), p['w_e2'][l].astype(BF16),
                       p['norm_out'] if l == depth - 1 else None)
    return x.reshape(bsz, t_len, d), (jnp.stack(gla_s), jnp.stack(lru_s), jnp.stack(conv_s),
                                      jnp.stack(ks), jnp.stack(vs))


def kernel(x_prompt, x_sample, state_gla, state_lru, state_conv, cache_k, cache_v, c_prompt, c_sample,
           w_ada, b_ada, norm_mix, norm_ffn, norm_out, w_ab_in, w_gk2, b_gk, gla_norm, conv_w, conv_b,
           lru_wa, lru_ba, lru_wi, lru_bi, lru_lam, w_ab_out, w_sb_qkv, w_sb_out, w_router, b_router,
           w_e1, w_e3, w_e2):
    p = dict(w_ada=w_ada, b_ada=b_ada, norm_mix=norm_mix, norm_ffn=norm_ffn, norm_out=norm_out,
             w_ab_in=w_ab_in, w_gk2=w_gk2, b_gk=b_gk, gla_norm=gla_norm, conv_w=conv_w, conv_b=conv_b,
             lru_wa=lru_wa, lru_ba=lru_ba, lru_wi=lru_wi, lru_bi=lru_bi, lru_lam=lru_lam, w_ab_out=w_ab_out,
             w_sb_qkv=w_sb_qkv, w_sb_out=w_sb_out, w_router=w_router, b_router=b_router,
             w_e1=w_e1, w_e3=w_e3, w_e2=w_e2, sb_heads=cache_k.shape[3])
    past = dict(state_gla=state_gla, state_lru=state_lru, state_conv=state_conv, cache_k=cache_k, cache_v=cache_v)
    y_prompt, (p_gla, p_lru, p_conv, p_k, p_v) = _trunk(x_prompt, c_prompt, p, None)
    y_sample, (s_gla, s_lru, s_conv, s_k, s_v) = _trunk(x_sample, c_sample, p, past)
    return (y_prompt, y_sample, p_gla, p_lru, p_conv, p_k, p_v, s_gla, s_lru, s_conv, s_k, s_v)
```

```python
import functools

import jax
import jax.numpy as jnp
from jax import lax
from jax.experimental import pallas as pl
from jax.experimental.pallas import tpu as pltpu

F32 = jnp.float32
BF16 = jnp.bfloat16
I32 = jnp.int32

EPS = 1e-6
N_MOD = 6
GLA_HEADS = 4
GLA_GATE_TAU = 16.0
LRU_C = 8.0
CONV_WIDTH = 4
N_GROUPS = 4
EXPERTS_PER_GROUP = 4
N_PAIRS = 6
N_CLASSES = N_GROUPS * N_PAIRS

LANES = 128
SUBLANES = 8
VMEM_LIMIT = 56 * 1024 * 1024

ROW_TILE = 512
MIX_TILE = 256
GLA_CHUNK = 64
SB_BLOCK = 256
SB_PROMPT_LANES = 256
MOE_BLOCK = 256
GATHER_TILE = 256


def _cparams(*sem):
    return pltpu.CompilerParams(dimension_semantics=sem, vmem_limit_bytes=VMEM_LIMIT)


def _log_sigmoid(z):
    return jnp.minimum(z, 0.0) - jnp.log(1.0 + jnp.exp(-jnp.abs(z)))


def _softplus(z):
    return jnp.maximum(z, 0.0) + jnp.log1p(jnp.exp(-jnp.abs(z)))


def _sigmoid(z):
    return 1.0 / (1.0 + jnp.exp(-z))


def _rms_mod(x, gain, shift, scale):
    ms = jnp.mean(x * x, axis=-1, keepdims=True)
    y = x * lax.rsqrt(ms + EPS) * gain
    return y * (1.0 + scale) + shift


def _row_tile(n_rows, t_len, target):
    if t_len % target == 0:
        return target
    return n_rows


def _mod_operand(mod, n_rows, t_len, tile):
    d = mod.shape[-1]
    if t_len % tile == 0:
        per_seq = t_len // tile
        return mod[:, None, :], (lambda nidx: pl.BlockSpec((None, 1, d), lambda *i: (i[nidx] // per_seq, 0, 0)))
    rows = jnp.repeat(mod, t_len, axis=0)
    return rows, (lambda nidx: pl.BlockSpec((tile, d), lambda *i: (i[nidx], 0)))


def _ada_kernel(c_ref, w_ref, b_ref, o_ref):
    c = c_ref[...]
    cond = c * _sigmoid(c)
    o_ref[...] = jnp.dot(cond.astype(BF16), w_ref[...].astype(BF16), preferred_element_type=F32) + b_ref[...]


def _ada(c, w_ada, b_ada):
    depth, d, e = w_ada.shape
    b = c.shape[0]
    tn = d
    return pl.pallas_call(
        _ada_kernel,
        grid=(depth, e // tn),
        in_specs=[pl.BlockSpec((b, d), lambda l, j: (0, 0)),
                  pl.BlockSpec((None, d, tn), lambda l, j: (l, 0, j)),
                  pl.BlockSpec((None, 1, tn), lambda l, j: (l, 0, j))],
        out_specs=pl.BlockSpec((None, b, tn), lambda l, j: (l, 0, j)),
        out_shape=jax.ShapeDtypeStruct((depth, b, e), F32),
        compiler_params=_cparams("parallel", "parallel"),
        name="ada_mod",
    )(c, w_ada, b_ada[:, None, :])


def _norm_proj_kernel(out_plan, x_ref, gain_ref, sh_ref, sc_ref, *refs):
    n_w = len(out_plan)
    w_refs, o_refs = refs[:n_w], refs[n_w:]
    h = _rms_mod(x_ref[...], gain_ref[...], sh_ref[...], sc_ref[...]).astype(BF16)
    k = 0
    for w_ref, (scale, dtypes) in zip(w_refs, out_plan):
        y = jnp.dot(h, w_ref[...], preferred_element_type=F32)
        if scale != 1.0:
            y = y * scale
        for dt in dtypes:
            o_refs[k][...] = y.astype(dt)
            k += 1


def _norm_proj(x, t_len, gain, shift, scale, pieces):
    n, d = x.shape
    tm = _row_tile(n, t_len, ROW_TILE)
    sh_arr, sh_spec = _mod_operand(shift, n, t_len, tm)
    sc_arr, sc_spec = _mod_operand(scale, n, t_len, tm)
    in_specs = [pl.BlockSpec((tm, d), lambda i: (i, 0)),
                pl.BlockSpec((1, d), lambda i: (0, 0)),
                sh_spec(0), sc_spec(0)]
    out_specs, out_shapes, plan, weights = [], [], [], []
    for w, s, dtypes in pieces:
        e = w.shape[1]
        in_specs.append(pl.BlockSpec((d, e), lambda i: (0, 0)))
        weights.append(w)
        plan.append((s, tuple(dtypes)))
        for dt in dtypes:
            out_specs.append(pl.BlockSpec((tm, e), lambda i: (i, 0)))
            out_shapes.append(jax.ShapeDtypeStruct((n, e), dt))
    return pl.pallas_call(
        functools.partial(_norm_proj_kernel, tuple(plan)),
        grid=(n // tm,),
        in_specs=in_specs, out_specs=out_specs, out_shape=out_shapes,
        compiler_params=_cparams("parallel"),
        name="norm_proj",
    )(x, gain[None, :], sh_arr, sc_arr, *weights)


def _shift_rows(x, s, fill):
    rows = lax.broadcasted_iota(I32, x.shape, 0)
    return jnp.where(rows >= s, pltpu.roll(x, s, axis=0), fill)


def _cumsum_rows(x):
    n = x.shape[0]
    s = 1
    while s < n:
        x = x + _shift_rows(x, s, 0.0)
        s *= 2
    return x


def _linear_scan_rows(a, b, h_init):
    n, w = a.shape
    a = a.reshape(n // SUBLANES, SUBLANES, w)
    b = b.reshape(n // SUBLANES, SUBLANES, w)
    in_group = lax.broadcasted_iota(I32, a.shape, 1)
    s = 1
    while s < SUBLANES:
        keep = in_group >= s
        b = a * jnp.where(keep, pltpu.roll(b, s, axis=1), 0.0) + b
        a = a * jnp.where(keep, pltpu.roll(a, s, axis=1), 1.0)
        s *= 2
    groups, carry = [], h_init
    for g in range(n // SUBLANES):
        h = a[g] * carry + b[g]
        groups.append(h)
        carry = h[SUBLANES - 1:SUBLANES, :]
    return jnp.concatenate(groups, axis=0)


def _gelu_tanh(x):
    return 0.5 * x * (1.0 + jnp.tanh(0.7978845608028654 * (x + 0.044715 * (x * x * x))))


def _ab_mixer_kernel(first_pos_zero, tt, chunk,
                     qk_ref, v_ref, g_ref, gk_ref, xr_ref, xg_ref, s0_ref, h0_ref, buf0_ref,
                     wgk_ref, bgk_ref, gnorm_ref, cw_ref, cb_ref, wa_ref, wi_ref, ba_ref, bi_ref, lam_ref,
                     mix_ref, s_out_ref, h_out_ref, buf_out_ref,
                     st_ref, hc_ref, xpad_ref):
    ti = pl.program_id(1)
    n_t = pl.num_programs(1)
    dqk = qk_ref.shape[1] // 2
    dv = v_ref.shape[1]
    hk = dqk // GLA_HEADS
    hv = dv // GLA_HEADS
    w_lru = xr_ref.shape[1]

    @pl.when(ti == 0)
    def _():
        st_ref[...] = s0_ref[...]
        hc_ref[...] = h0_ref[...]
        xpad_ref[0:SUBLANES, :] = buf0_ref[...]

    xpad_ref[SUBLANES:SUBLANES + tt, :] = xr_ref[...]
    xc = cb_ref[...]
    for i in range(CONV_WIDTH):
        off = SUBLANES - (CONV_WIDTH - 1) + i
        xc = xc + cw_ref[i:i + 1, :] * xpad_ref[off:off + tt, :]
    tail = xpad_ref[tt:tt + SUBLANES, :]
    xpad_ref[0:SUBLANES, :] = tail
    xc_b = xc.astype(BF16)
    half = w_lru // 2
    r_lin = jnp.concatenate([jnp.dot(xc_b[:, j * half:(j + 1) * half], wa_ref[j], preferred_element_type=F32)
                             for j in range(2)], axis=-1)
    i_lin = jnp.concatenate([jnp.dot(xc_b[:, j * half:(j + 1) * half], wi_ref[j], preferred_element_type=F32)
                             for j in range(2)], axis=-1)
    r = _sigmoid(r_lin + ba_ref[...])
    i_g = _sigmoid(i_lin + bi_ref[...])
    log_at = (-LRU_C) * r * _softplus(-lam_ref[...])
    a = jnp.exp(log_at)
    om = 1.0 - a * a
    mult = jnp.where(om > 0.0, om * lax.rsqrt(om), 0.0)
    if first_pos_zero:
        rows = lax.broadcasted_iota(I32, mult.shape, 0)
        mult = jnp.where((rows == 0) & (ti == 0), 1.0, mult)
    bterm = mult * (i_g * xc)
    hs = _linear_scan_rows(a, bterm, hc_ref[...])
    hc_ref[...] = hs[tt - 1:tt, :]
    y_b = hs * _gelu_tanh(xg_ref[...])
    mix_ref[:, dv:] = y_b.astype(BF16)

    lane_k = lax.broadcasted_iota(I32, (1, dqk), 1)
    st_rows = lax.broadcasted_iota(I32, (dv, dqk), 0)
    st_cols = lax.broadcasted_iota(I32, (dv, dqk), 1)
    st_mask = functools.reduce(
        jnp.logical_or,
        [(st_rows >= h * hv) & (st_rows < (h + 1) * hv) & (st_cols >= h * hk) & (st_cols < (h + 1) * hk)
         for h in range(GLA_HEADS)])
    crow = lax.broadcasted_iota(I32, (chunk, chunk), 0)
    ccol = lax.broadcasted_iota(I32, (chunk, chunk), 1)
    causal = ccol <= crow
    mid = chunk // 2 - 1
    for c in range(tt // chunk):
        rs = slice(c * chunk, (c + 1) * chunk)
        q = qk_ref[rs, 0:dqk] * (hk ** -0.5)
        k = qk_ref[rs, dqk:2 * dqk]
        v_b = v_ref[rs, :].astype(BF16)
        u = jnp.dot(gk_ref[rs, :].astype(BF16), wgk_ref[...], preferred_element_type=F32) + bgk_ref[...]
        b = _cumsum_rows(_log_sigmoid(u) * (1.0 / GLA_GATE_TAU))
        b_mid = b[mid:mid + 1, :]
        b_last = b[chunk - 1:chunk, :]
        q_e = q * jnp.exp(b - b_mid)
        k_e = (k * jnp.exp(b_mid - b)).astype(BF16)
        k_tail = (k * jnp.exp(b_last - b)).astype(BF16)
        q_dec = (q_e * jnp.exp(b_mid)).astype(BF16)
        q_e = q_e.astype(BF16)
        st = st_ref[...]
        o = lax.dot_general(q_dec, st.astype(BF16), (((1,), (1,)), ((), ())), preferred_element_type=F32)
        o_intra = []
        for h in range(GLA_HEADS):
            q_h = jnp.where((lane_k >= h * hk) & (lane_k < (h + 1) * hk), q_e, jnp.zeros_like(q_e))
            att = lax.dot_general(q_h, k_e, (((1,), (1,)), ((), ())), preferred_element_type=F32)
            att = jnp.where(causal, att, 0.0).astype(BF16)
            o_intra.append(jnp.dot(att, v_b[:, h * hv:(h + 1) * hv], preferred_element_type=F32))
        o = o + jnp.concatenate(o_intra, axis=-1)
        kv = lax.dot_general(v_b, k_tail, (((0,), (0,)), ((), ())), preferred_element_type=F32)
        st_ref[...] = st * jnp.exp(b_last) + jnp.where(st_mask, kv, 0.0)
        g = g_ref[rs, :]
        gate = g * _sigmoid(g)
        y_a = []
        for h in range(GLA_HEADS):
            o_h = o[:, h * hv:(h + 1) * hv]
            ms = jnp.mean(o_h * o_h, axis=-1, keepdims=True)
            y_a.append(o_h * lax.rsqrt(ms + EPS) * gnorm_ref[...] * gate[:, h * hv:(h + 1) * hv])
        mix_ref[rs, 0:dv] = jnp.concatenate(y_a, axis=-1).astype(BF16)

    @pl.when(ti == n_t - 1)
    def _():
        s_out_ref[...] = st_ref[...]
        h_out_ref[...] = hc_ref[...]
        buf_out_ref[...] = xpad_ref[0:SUBLANES, :]


def _ab_mixer(proj, bsz, t_len, first_pos_zero, s_gla, h_lru, conv_buf, w_gk2, b_gk, gla_norm, conv_w, conv_b,
              lru_wa, lru_ba, lru_wi, lru_bi, lru_lam):
    qk, v, g, gk, xr, xg = proj
    n = qk.shape[0]
    dqk, dv, w_lru = qk.shape[1] // 2, v.shape[1], xr.shape[1]
    hk, hv = dqk // GLA_HEADS, dv // GLA_HEADS
    tt = MIX_TILE if t_len % MIX_TILE == 0 else t_len
    chunk = GLA_CHUNK if tt % GLA_CHUNK == 0 else tt
    n_t = t_len // tt

    eye = jnp.eye(GLA_HEADS, dtype=F32)
    st0 = jnp.einsum('bhkv,hg->bhvgk', s_gla.astype(F32), eye).reshape(bsz, dv, dqk)
    buf0 = jnp.pad(conv_buf.astype(F32), ((0, 0), (SUBLANES - (CONV_WIDTH - 1), 0), (0, 0)))
    wgk = jnp.pad(w_gk2, ((0, gk.shape[1] - w_gk2.shape[0]), (0, 0))).astype(BF16)
    nb = lru_wa.shape[0]
    half_blocks = nb // 2

    def block_diag(w):
        bd = w.shape[1]
        e = jnp.eye(half_blocks, dtype=w.dtype)
        w2 = w.reshape(2, half_blocks, bd, bd)
        return jnp.einsum('jnio,nm->jnimo', w2, e).reshape(2, half_blocks * bd, half_blocks * bd).astype(BF16)

    row = lambda z: z.astype(F32)[None, :]
    tok = lambda w: pl.BlockSpec((tt, w), lambda b, i: (b * n_t + i, 0))
    per_b = lambda r, c: pl.BlockSpec((None, r, c), lambda b, i: (b, 0, 0))
    full2 = lambda r, c: pl.BlockSpec((r, c), lambda b, i: (0, 0))
    full3 = lambda a, r, c: pl.BlockSpec((a, r, c), lambda b, i: (0, 0, 0))
    mix, st, h_new, buf = pl.pallas_call(
        functools.partial(_ab_mixer_kernel, first_pos_zero, tt, chunk),
        grid=(bsz, n_t),
        in_specs=[tok(2 * dqk), tok(dv), tok(dv), tok(gk.shape[1]), tok(w_lru), tok(w_lru),
                  per_b(dv, dqk), per_b(1, w_lru), per_b(SUBLANES, w_lru),
                  full2(gk.shape[1], dqk), full2(1, dqk), full2(1, hv), full2(CONV_WIDTH, w_lru), full2(1, w_lru),
                  full3(2, w_lru // 2, w_lru // 2), full3(2, w_lru // 2, w_lru // 2),
                  full2(1, w_lru), full2(1, w_lru), full2(1, w_lru)],
        out_specs=[tok(dv + w_lru), per_b(dv, dqk), per_b(1, w_lru), per_b(SUBLANES, w_lru)],
        out_shape=[jax.ShapeDtypeStruct((n, dv + w_lru), BF16),
                   jax.ShapeDtypeStruct((bsz, dv, dqk), F32),
                   jax.ShapeDtypeStruct((bsz, 1, w_lru), F32),
                   jax.ShapeDtypeStruct((bsz, SUBLANES, w_lru), F32)],
        scratch_shapes=[pltpu.VMEM((dv, dqk), F32), pltpu.VMEM((1, w_lru), F32),
                        pltpu.VMEM((tt + SUBLANES, w_lru), F32)],
        compiler_params=_cparams("parallel", "arbitrary"),
        name="ab_mixer",
    )(qk, v, g, gk, xr, xg, st0, h_lru.astype(F32)[:, None, :], buf0,
      wgk, row(b_gk), row(gla_norm), conv_w.astype(F32), row(conv_b),
      block_diag(lru_wa), block_diag(lru_wi), row(lru_ba), row(lru_bi), row(lru_lam))
    st5 = st.reshape(bsz, GLA_HEADS, hv, GLA_HEADS, hk)
    s_new = jnp.stack([st5[:, h, :, h, :] for h in range(GLA_HEADS)], axis=1).swapaxes(-1, -2)
    return mix, s_new, h_new[:, 0], buf[:, SUBLANES - (CONV_WIDTH - 1):]


SB_MASKED = -1e30
SB_DEAD = -152.0
LOG2_E = 1.4426950408889634


def _sb_logits(q_h, k_blk):
    return lax.dot_general(q_h, k_blk, (((1,), (1,)), ((), ())), preferred_element_type=F32)


def _sb_gates(z, mask):
    neg_abs = pltpu.bitcast(pltpu.bitcast(z, I32) | jnp.int32(-2 ** 31), F32)
    ls = jnp.minimum(z, 0.0) - jnp.log2(1.0 + jnp.exp2(neg_abs))
    lk = ls - z
    if mask is not None:
        lk = jnp.where(mask, lk, 0.0)
    return ls, lk.astype(BF16), jnp.sum(lk, axis=-1, keepdims=True)


def _sb_prefix(lk, upper, suffix, mask):
    pre = jnp.dot(lk, upper, preferred_element_type=F32) + suffix
    if mask is not None:
        pre = jnp.where(mask, pre, SB_MASKED)
    return pre


def _sb_values(ls, pre, v_blk):
    return jnp.dot(jnp.exp2(ls + pre).astype(BF16), v_blk, preferred_element_type=F32)


def _split_heads(q, dh):
    lane = lax.broadcasted_iota(I32, (1, q.shape[1]), 1)
    zero = jnp.zeros_like(q)
    return [jnp.where((lane >= h * dh) & (lane < (h + 1) * dh), q, zero) for h in range(q.shape[1] // dh)], lane


def _merge_heads(acc, lane, dh):
    out = acc[0]
    for h in range(1, len(acc)):
        out = jnp.where(lane >= h * dh, acc[h], out)
    return out


def _upper(n):
    r = lax.broadcasted_iota(I32, (n, n), 0)
    c = lax.broadcasted_iota(I32, (n, n), 1)
    return jnp.where(r > c, 1.0, 0.0).astype(BF16)


def _sb_prompt_kernel(blk, dh, q_ref, k_ref, v_ref, o_ref, acc_ref, suf_ref):
    qi = pl.program_id(2)
    n_blk = qi + 1
    q_heads, lane = _split_heads(q_ref[...], dh)
    upper = _upper(blk)
    diag = lax.broadcasted_iota(I32, (blk, blk), 1) < lax.broadcasted_iota(I32, (blk, blk), 0)

    def key_rows(s):
        return pl.ds(pl.multiple_of((qi - s) * blk, blk), blk)

    def block(s, mask):
        k_blk = k_ref[key_rows(s), :]
        v_blk = v_ref[key_rows(s), :]
        for h in range(len(q_heads)):
            ls, lk, tot = _sb_gates(_sb_logits(q_heads[h], k_blk), mask)
            acc_ref[h] += _sb_values(ls, _sb_prefix(lk, upper, suf_ref[h], mask), v_blk)
            suf_ref[h] += tot

    def live():
        return (jnp.max(suf_ref[...]) > SB_DEAD).astype(I32)

    acc_ref[...] = jnp.zeros_like(acc_ref)
    suf_ref[...] = jnp.zeros_like(suf_ref)
    block(0, diag)

    def more(carry):
        s, go = carry
        return (s < n_blk) & (go > 0)

    def step(carry):
        s, _ = carry
        block(s, None)
        return s + 1, live()

    lax.while_loop(more, step, (jnp.int32(1), live()))
    o_ref[...] = _merge_heads([acc_ref[h] for h in range(len(q_heads))], lane, dh).astype(o_ref.dtype)


def _sb_prompt(q, k, v, bsz, t_len, dh):
    n, width = q.shape
    blk = SB_BLOCK
    nq = t_len // blk
    lanes = SB_PROMPT_LANES
    groups = width // lanes
    heads = lanes // dh
    return pl.pallas_call(
        functools.partial(_sb_prompt_kernel, blk, dh),
        grid=(bsz, groups, nq),
        in_specs=[pl.BlockSpec((blk, lanes), lambda b, g, i: (b * nq + i, g)),
                  pl.BlockSpec((t_len, lanes), lambda b, g, i: (b, g)),
                  pl.BlockSpec((t_len, lanes), lambda b, g, i: (b, g))],
        out_specs=pl.BlockSpec((blk, lanes), lambda b, g, i: (b * nq + i, g)),
        out_shape=jax.ShapeDtypeStruct((n, width), BF16),
        scratch_shapes=[pltpu.VMEM((heads, blk, lanes), F32), pltpu.VMEM((heads, blk, 1), F32)],
        compiler_params=_cparams("parallel", "parallel", "arbitrary"),
        name="sb_prompt",
    )(q, k, v)


def _sb_sample_kernel(t_new, past_len, blk, dh, q_ref, k_ref, v_ref, pk_ref, pv_ref, o_ref):
    q_heads, lane = _split_heads(q_ref[...], dh)
    r = lax.broadcasted_iota(I32, (t_new, t_new), 0)
    c = lax.broadcasted_iota(I32, (t_new, t_new), 1)
    upper = _upper(blk)
    blocks = [(k_ref[...], v_ref[...], _upper(t_new), c < r)]
    for j in reversed(range(past_len // blk)):
        rows = slice(j * blk, (j + 1) * blk)
        blocks.append((pk_ref[rows, :].astype(BF16), pv_ref[rows, :].astype(BF16), upper, None))
    acc = []
    for q_h in q_heads:
        staged = [_sb_gates(_sb_logits(q_h, k_blk), mask) for k_blk, _, _, mask in blocks]
        acc_h = jnp.zeros((t_new, LANES), F32)
        suf = jnp.zeros((t_new, 1), F32)
        for (ls, lk, tot), (_, v_blk, up, mask) in zip(staged, blocks):
            acc_h = acc_h + _sb_values(ls, _sb_prefix(lk, up, suf, mask), v_blk)
            suf = suf + tot
        acc.append(acc_h)
    o_ref[...] = _merge_heads(acc, lane, dh).astype(o_ref.dtype)


def _sb_sample(q, k, v, past_k, past_v, bsz, t_len, dh):
    n, width = q.shape
    past_len = past_k.shape[0] // bsz
    blk = SB_BLOCK if past_len % SB_BLOCK == 0 else past_len
    groups = width // LANES
    new = pl.BlockSpec((t_len, LANES), lambda b, g: (b, g))
    past = pl.BlockSpec((past_len, LANES), lambda b, g: (b, g))
    return pl.pallas_call(
        functools.partial(_sb_sample_kernel, t_len, past_len, blk, dh),
        grid=(bsz, groups),
        in_specs=[new, new, new, past, past],
        out_specs=new,
        out_shape=jax.ShapeDtypeStruct((n, width), BF16),
        compiler_params=_cparams("parallel", "parallel"),
        name="sb_sample",
    )(q, k, v, past_k, past_v)


def _first_argmax(vals):
    best_v, best_i = vals[0], jnp.zeros(vals[0].shape, I32)
    for i in range(1, len(vals)):
        better = vals[i] > best_v
        best_v = jnp.where(better, vals[i], best_v)
        best_i = jnp.where(better, i, best_i)
    return best_v, best_i


def _route_rows(logits_t, bias_col):
    s = _sigmoid(logits_t)
    sel = s + bias_col
    epg = EXPERTS_PER_GROUP
    scores = []
    for gi in range(N_GROUPS):
        rows = [sel[gi * epg + i:gi * epg + i + 1, :] for i in range(epg)]
        pair_sums = [rows[i] + rows[j] for i in range(epg) for j in range(i + 1, epg)]
        scores.append(functools.reduce(jnp.maximum, pair_sums))
    _, best = _first_argmax(scores)

    def in_best(mat, i):
        out = mat[i:i + 1, :]
        for gi in range(1, N_GROUPS):
            out = jnp.where(best == gi, mat[gi * epg + i:gi * epg + i + 1, :], out)
        return out

    sel_g = [in_best(sel, i) for i in range(epg)]
    s_g = [in_best(s, i) for i in range(epg)]
    _, i1 = _first_argmax(sel_g)
    _, i2 = _first_argmax([jnp.where(i1 == i, -jnp.inf, sel_g[i]) for i in range(epg)])
    lo = jnp.minimum(i1, i2)
    hi = jnp.maximum(i1, i2)
    pick = lambda idx: functools.reduce(lambda acc, i: jnp.where(idx == i, s_g[i], acc), range(1, epg), s_g[0])
    w_lo, w_hi = pick(lo), pick(hi)
    tot = w_lo + w_hi
    pair = jnp.where(lo == 0, 0, jnp.where(lo == 1, epg - 1, 2 * epg - 3)) + (hi - lo - 1)
    return best * N_PAIRS + pair, w_lo / tot, w_hi / tot


def _proj_route_kernel(a_ref, w_ref, x_ref, g1_ref, gain_ref, sh_ref, sc_ref, wr_hi_ref, wr_lo_ref, br_ref,
                       xn_ref, h2_ref, ri_ref):
    y = jnp.dot(a_ref[...], w_ref[...], preferred_element_type=F32)
    xn = x_ref[...] + g1_ref[...] * y
    xn_ref[...] = xn
    d = xn.shape[1]
    h2 = _rms_mod(xn, gain_ref[...], sh_ref[...], sc_ref[...])
    h2_ref[:, 0:d] = h2
    h_hi = h2.astype(BF16)
    h_lo = (h2 - h_hi.astype(F32)).astype(BF16)
    logits = (jnp.dot(h_hi, wr_hi_ref[...], preferred_element_type=F32)
              + jnp.dot(h_lo, wr_hi_ref[...], preferred_element_type=F32)
              + jnp.dot(h_hi, wr_lo_ref[...], preferred_element_type=F32))
    n_e = br_ref.shape[0]
    logits_t = logits.T[0:n_e, :]
    cls, w_lo, w_hi = _route_rows(logits_t, br_ref[...])
    tm = logits_t.shape[1]
    h2_ref[:, d:] = jnp.concatenate([w_lo, w_hi, jnp.zeros((LANES - 2, tm), F32)], axis=0).T
    ri_ref[...] = jnp.concatenate([cls, jnp.zeros((SUBLANES - 1, tm), I32)], axis=0)


def _proj_route(a, w_out, x, t_len, g1, gain, shift, scale, w_router, b_router):
    n, d = x.shape
    k = a.shape[1]
    n_e = w_router.shape[1]
    tm = _row_tile(n, t_len, ROW_TILE)
    g1_arr, g1_spec = _mod_operand(g1, n, t_len, tm)
    sh_arr, sh_spec = _mod_operand(shift, n, t_len, tm)
    sc_arr, sc_spec = _mod_operand(scale, n, t_len, tm)
    wr = jnp.pad(w_router.astype(F32), ((0, 0), (0, LANES - n_e)))
    wr_hi = wr.astype(BF16)
    wr_lo = (wr - wr_hi.astype(F32)).astype(BF16)
    tokd = pl.BlockSpec((tm, d), lambda i: (i, 0))
    const = lambda r, c: pl.BlockSpec((r, c), lambda i: (0, 0))
    route = pl.BlockSpec((SUBLANES, tm), lambda i: (0, i))
    return pl.pallas_call(
        _proj_route_kernel,
        grid=(n // tm,),
        in_specs=[pl.BlockSpec((tm, k), lambda i: (i, 0)), const(k, d), tokd, g1_spec(0), const(1, d),
                  sh_spec(0), sc_spec(0), const(d, LANES), const(d, LANES), const(n_e, 1)],
        out_specs=[tokd, pl.BlockSpec((tm, d + LANES), lambda i: (i, 0)), route],
        out_shape=[jax.ShapeDtypeStruct((n, d), F32), jax.ShapeDtypeStruct((n, d + LANES), F32),
                   jax.ShapeDtypeStruct((SUBLANES, n), I32)],
        compiler_params=_cparams("parallel"),
        name="proj_route",
    )(a, w_out, x, g1_arr, gain[None, :], sh_arr, sc_arr, wr_hi, wr_lo, b_router.astype(F32)[:, None])


def _gather_rows(src_hbm, idx_ref, base, n_rows, dst, sem):
    last = idx_ref.shape[0] - 1

    for r in range(n_rows):
        row = idx_ref[jnp.minimum(base + r, last)]
        pltpu.make_async_copy(src_hbm.at[pl.ds(row, 1)], dst.at[pl.ds(r, 1)], sem).start()


def _wait_rows(src_hbm, n_rows, dst, sem):
    pltpu.make_async_copy(src_hbm.at[pl.ds(0, n_rows)], dst, sem).wait()


def _moe_kernel(bm, d, order_ref, base_ref, valid_ref, e_lo_ref, e_hi_ref, used_ref,
                h_hbm, w1a_ref, w3a_ref, w2a_ref, w1b_ref, w3b_ref, w2b_ref,
                y_ref, buf0_ref, buf1_ref, sem_ref):
    i = pl.program_id(0)
    n_used = used_ref[0]

    @pl.when((i == 0) & (n_used > 0))
    def _():
        _gather_rows(h_hbm, order_ref, base_ref[0], bm, buf0_ref, sem_ref.at[0])

    def block(cur, cur_sem, nxt, nxt_sem):
        _wait_rows(h_hbm, bm, cur, cur_sem)
        xb = cur[:, 0:d].astype(BF16)
        live = lax.broadcasted_iota(I32, (bm, LANES), 0) < valid_ref[i]
        wts = jnp.where(live, cur[:, d:], 0.0)
        _gather_rows(h_hbm, order_ref, base_ref[i + 1], bm, nxt, nxt_sem)

        def expert(w1_ref, w3_ref, w2_ref):
            u = jnp.dot(xb, w1_ref[...], preferred_element_type=F32)
            t = jnp.dot(xb, w3_ref[...], preferred_element_type=F32)
            mid = (u * _sigmoid(u) * t).astype(BF16)
            return jnp.dot(mid, w2_ref[...], preferred_element_type=F32)

        y_ref[...] = (expert(w1a_ref, w3a_ref, w2a_ref) * wts[:, 0:1]
                      + expert(w1b_ref, w3b_ref, w2b_ref) * wts[:, 1:2])

        @pl.when(i == n_used - 1)
        def _():
            _wait_rows(h_hbm, bm, nxt, nxt_sem)

    @pl.when((i < n_used) & (i % 2 == 0))
    def _():
        block(buf0_ref, sem_ref.at[0], buf1_ref, sem_ref.at[1])

    @pl.when((i < n_used) & (i % 2 == 1))
    def _():
        block(buf1_ref, sem_ref.at[1], buf0_ref, sem_ref.at[0])

    @pl.when(i >= n_used)
    def _():
        y_ref[...] = jnp.zeros_like(y_ref)


def _moe_sorted(h2w, order, blk_base, blk_valid, blk_lo, blk_hi, n_used, w1, w3, w2):
    n, dw = h2w.shape
    d = dw - LANES
    n_blocks = blk_valid.shape[0]
    bm = MOE_BLOCK
    de = w1.shape[2]

    def wspec(which, r, c):
        def index(i, order, base, valid, lo, hi, used):
            blk = jnp.minimum(i, jnp.maximum(used[0] - 1, 0))
            return ((lo, hi)[which][blk], 0, 0)
        return pl.BlockSpec((None, r, c), index)

    return pl.pallas_call(
        functools.partial(_moe_kernel, bm, d),
        grid_spec=pltpu.PrefetchScalarGridSpec(
            num_scalar_prefetch=6,
            grid=(n_blocks,),
            in_specs=[pl.BlockSpec(memory_space=pl.ANY),
                      wspec(0, d, de), wspec(0, d, de), wspec(0, de, d),
                      wspec(1, d, de), wspec(1, d, de), wspec(1, de, d)],
            out_specs=pl.BlockSpec((bm, d), lambda i, *_: (i, 0)),
            scratch_shapes=[pltpu.VMEM((bm, dw), F32), pltpu.VMEM((bm, dw), F32), pltpu.SemaphoreType.DMA((2,))]),
        out_shape=jax.ShapeDtypeStruct((n_blocks * bm, d), F32),
        compiler_params=_cparams("arbitrary"),
        name="moe_sorted",
    )(order, blk_base, blk_valid, blk_lo, blk_hi, n_used, h2w, w1, w3, w2, w1, w3, w2)


def _unsort_kernel(tm, final_norm, pos_ref, y_hbm, x_ref, g2_ref, gain_ref, o_ref, buf0_ref, buf1_ref, sem_ref):
    i = pl.program_id(0)
    n_i = pl.num_programs(0)

    @pl.when(i == 0)
    def _():
        _gather_rows(y_hbm, pos_ref, 0, tm, buf0_ref, sem_ref.at[0])

    def step(cur, cur_sem, nxt, nxt_sem):
        @pl.when(i + 1 < n_i)
        def _():
            _gather_rows(y_hbm, pos_ref, (i + 1) * tm, tm, nxt, nxt_sem)

        _wait_rows(y_hbm, tm, cur, cur_sem)
        x = x_ref[...] + g2_ref[...] * cur[...]
        if final_norm:
            ms = jnp.mean(x * x, axis=-1, keepdims=True)
            x = x * lax.rsqrt(ms + EPS) * gain_ref[...]
        o_ref[...] = x

    @pl.when(i % 2 == 0)
    def _():
        step(buf0_ref, sem_ref.at[0], buf1_ref, sem_ref.at[1])

    @pl.when(i % 2 == 1)
    def _():
        step(buf1_ref, sem_ref.at[1], buf0_ref, sem_ref.at[0])


def _unsort_resid(y_sorted, pos, x, t_len, g2, final_gain):
    n, d = x.shape
    tm = _row_tile(n, t_len, GATHER_TILE)
    g2_arr, g2_spec = _mod_operand(g2, n, t_len, tm)
    final_norm = final_gain is not None
    gain = (final_gain if final_norm else jnp.ones((d,), F32)).astype(F32)[None, :]
    return pl.pallas_call(
        functools.partial(_unsort_kernel, tm, final_norm),
        grid_spec=pltpu.PrefetchScalarGridSpec(
            num_scalar_prefetch=1,
            grid=(n // tm,),
            in_specs=[pl.BlockSpec(memory_space=pl.ANY),
                      pl.BlockSpec((tm, d), lambda i, pos: (i, 0)),
                      g2_spec(0),
                      pl.BlockSpec((1, d), lambda i, pos: (0, 0))],
            out_specs=pl.BlockSpec((tm, d), lambda i, pos: (i, 0)),
            scratch_shapes=[pltpu.VMEM((tm, d), F32), pltpu.VMEM((tm, d), F32), pltpu.SemaphoreType.DMA((2,))]),
        out_shape=jax.ShapeDtypeStruct((n, d), F32),
        compiler_params=_cparams("arbitrary"),
        name="unsort_resid",
    )(pos, y_sorted, x, g2_arr, gain)


def _sort_plan(cls):
    n = cls.shape[0]
    bm = MOE_BLOCK
    n_blocks = -(-n // bm) + N_CLASSES
    classes = jnp.arange(N_CLASSES, dtype=I32)
    cls_sorted, order = lax.sort((cls, jnp.arange(n, dtype=I32)), num_keys=1, is_stable=True)
    counts = jnp.sum((cls[:, None] == classes[None, :]).astype(I32), axis=0)
    padded = (counts + bm - 1) // bm * bm
    pad_end = jnp.cumsum(padded)
    pad_start = pad_end - padded
    start = jnp.cumsum(counts) - counts
    shift = jnp.sum(jnp.where(cls_sorted[:, None] == classes[None, :], (pad_start - start)[None, :], 0), axis=1)
    dest = jnp.arange(n, dtype=I32) + shift
    _, pos = lax.sort((order, dest), num_keys=1)
    blk_first = jnp.arange(n_blocks + 1, dtype=I32) * bm
    blk_cls = jnp.minimum(jnp.sum((pad_end[None, :] <= blk_first[:, None]).astype(I32), axis=1), N_CLASSES - 1)
    onehot = blk_cls[:, None] == classes[None, :]
    pick = lambda table: jnp.sum(jnp.where(onehot, table[None, :], 0), axis=1)
    into = blk_first - pick(pad_start)
    blk_base = jnp.clip(pick(start) + into, 0, n)
    blk_valid = jnp.clip(pick(counts) - into, 0, bm)[:n_blocks]
    pair_lo = jnp.array([i for i in range(EXPERTS_PER_GROUP) for j in range(i + 1, EXPERTS_PER_GROUP)], I32)
    pair_hi = jnp.array([j for i in range(EXPERTS_PER_GROUP) for j in range(i + 1, EXPERTS_PER_GROUP)], I32)
    group = classes // N_PAIRS
    blk_lo = pick(group * EXPERTS_PER_GROUP + pair_lo[classes % N_PAIRS])
    blk_hi = pick(group * EXPERTS_PER_GROUP + pair_hi[classes % N_PAIRS])
    n_used = (pad_end[-1] // bm).astype(I32)[None]
    return order, pos, blk_base.astype(I32), blk_valid.astype(I32), blk_lo, blk_hi, n_used


def _moe_layer(h2w, route_i, x, t_len, g2, w1, w3, w2, final_gain):
    order, pos, blk_base, blk_valid, blk_lo, blk_hi, n_used = _sort_plan(route_i[0])
    y_sorted = _moe_sorted(h2w, order, blk_base, blk_valid, blk_lo, blk_hi, n_used, w1, w3, w2)
    return _unsort_resid(y_sorted, pos, x, t_len, g2, final_gain)


def _trunk(x3, c, p, past):
    bsz, t_len, d = x3.shape
    n = bsz * t_len
    x = x3.reshape(n, d)
    depth = p['w_ada'].shape[0]
    mod = _ada(c.astype(F32), p['w_ada'], p['b_ada']).reshape(depth, bsz, N_MOD, d)
    gla_s, lru_s, conv_s, ks, vs = [], [], [], [], []
    for l in range(depth):
        sh1, sc1, g1, sh2, sc2, g2 = [mod[l, :, i] for i in range(N_MOD)]
        j = l // 2
        if l % 2 == 0:
            w_in = p['w_ab_in'][j]
            dqk2 = 2 * p['w_gk2'].shape[2]
            dv = GLA_HEADS * p['gla_norm'].shape[1]
            rank = p['w_gk2'].shape[1]
            w_lru = p['lru_lam'].shape[1]
            cuts = [0, dqk2, dqk2 + dv, dqk2 + 2 * dv, dqk2 + 2 * dv + rank,
                    dqk2 + 2 * dv + rank + w_lru, dqk2 + 2 * dv + rank + 2 * w_lru]
            cols = [w_in[:, cuts[i]:cuts[i + 1]] for i in range(6)]
            cols[3] = jnp.pad(cols[3], ((0, 0), (0, LANES - rank)))
            proj = _norm_proj(x, t_len, p['norm_mix'][l], sh1, sc1,
                              [(w.astype(BF16), 1.0, [F32]) for w in cols])
            if past is None:
                s0 = jnp.zeros((bsz, GLA_HEADS, dqk2 // 2 // GLA_HEADS, dv // GLA_HEADS), F32)
                h0 = jnp.zeros((bsz, w_lru), F32)
                buf = jnp.zeros((bsz, CONV_WIDTH - 1, w_lru), F32)
            else:
                s0, h0, buf = past['state_gla'][j], past['state_lru'][j], past['state_conv'][j]
            mix, sg, sl, sc = _ab_mixer(proj, bsz, t_len, past is None, s0, h0, buf, p['w_gk2'][j], p['b_gk'][j],
                                        p['gla_norm'][j], p['conv_w'][j], p['conv_b'][j], p['lru_wa'][j],
                                        p['lru_ba'][j], p['lru_wi'][j], p['lru_bi'][j], p['lru_lam'][j])
            gla_s.append(sg)
            lru_s.append(sl)
            conv_s.append(sc)
            w_out = p['w_ab_out'][j]
        else:
            w_qkv = p['w_sb_qkv'][j]
            width = w_qkv.shape[1] // 3
            heads = past['cache_k'].shape[3] if past is not None else p['sb_heads']
            dh = width // heads
            pieces = [(w_qkv[:, 0:width].astype(BF16), LOG2_E * dh ** -0.5, [BF16]),
                      (w_qkv[:, width:2 * width].astype(BF16), 1.0, [F32, BF16]),
                      (w_qkv[:, 2 * width:].astype(BF16), 1.0, [F32, BF16])]
            q_b, k_f, k_b, v_f, v_b = _norm_proj(x, t_len, p['norm_mix'][l], sh1, sc1, pieces)
            if past is None:
                mix = _sb_prompt(q_b, k_b, v_b, bsz, t_len, dh)
            else:
                pk = past['cache_k'][j].reshape(-1, width)
                pv = past['cache_v'][j].reshape(-1, width)
                mix = _sb_sample(q_b, k_b, v_b, pk, pv, bsz, t_len, dh)
            ks.append(k_f.reshape(bsz, t_len, heads, dh))
            vs.append(v_f.reshape(bsz, t_len, heads, dh))
            w_out = p['w_sb_out'][j]
        xn, h2w, route_i = _proj_route(mix, w_out.astype(BF16), x, t_len, g1, p['norm_ffn'][l], sh2, sc2,
                                       p['w_router'], p['b_router'])
        x = _moe_layer(h2w, route_i, xn, t_len, g2,
                       p['w_e1'][l].astype(BF16), p['w_e3'][l].astype(BF16), p['w_e2'][l].astype(BF16),
                       p['norm_out'] if l == depth - 1 else None)
    return x.reshape(bsz, t_len, d), (jnp.stack(gla_s), jnp.stack(lru_s), jnp.stack(conv_s),
                                      jnp.stack(ks), jnp.stack(vs))


def kernel(x_prompt, x_sample, state_gla, state_lru, state_conv, cache_k, cache_v, c_prompt, c_sample,
           w_ada, b_ada, norm_mix, norm_ffn, norm_out, w_ab_in, w_gk2, b_gk, gla_norm, conv_w, conv_b,
           lru_wa, lru_ba, lru_wi, lru_bi, lru_lam, w_ab_out, w_sb_qkv, w_sb_out, w_router, b_router,
           w_e1, w_e3, w_e2):
    p = dict(w_ada=w_ada, b_ada=b_ada, norm_mix=norm_mix, norm_ffn=norm_ffn, norm_out=norm_out,
             w_ab_in=w_ab_in, w_gk2=w_gk2, b_gk=b_gk, gla_norm=gla_norm, conv_w=conv_w, conv_b=conv_b,
             lru_wa=lru_wa, lru_ba=lru_ba, lru_wi=lru_wi, lru_bi=lru_bi, lru_lam=lru_lam, w_ab_out=w_ab_out,
             w_sb_qkv=w_sb_qkv, w_sb_out=w_sb_out, w_router=w_router, b_router=b_router,
             w_e1=w_e1, w_e3=w_e3, w_e2=w_e2, sb_heads=cache_k.shape[3])
    past = dict(state_gla=state_gla, state_lru=state_lru, state_conv=state_conv, cache_k=cache_k, cache_v=cache_v)
    y_prompt, (p_gla, p_lru, p_conv, p_k, p_v) = _trunk(x_prompt, c_prompt, p, None)
    y_sample, (s_gla, s_lru, s_conv, s_k, s_v) = _trunk(x_sample, c_sample, p, past)
    return (y_prompt, y_sample, p_gla, p_lru, p_conv, p_k, p_v, s_gla, s_lru, s_conv, s_k, s_v)
```

```python
import functools

import jax
import jax.numpy as jnp
from jax import lax
from jax.experimental import pallas as pl
from jax.experimental.pallas import tpu as pltpu

F32 = jnp.float32
BF16 = jnp.bfloat16
I32 = jnp.int32

EPS = 1e-6
N_MOD = 6
GLA_HEADS = 4
GLA_GATE_TAU = 16.0
LRU_C = 8.0
CONV_WIDTH = 4
N_GROUPS = 4
EXPERTS_PER_GROUP = 4
N_PAIRS = 6
N_CLASSES = N_GROUPS * N_PAIRS

LANES = 128
SUBLANES = 8
VMEM_LIMIT = 56 * 1024 * 1024

ROW_TILE = 512
MIX_TILE = 256
GLA_CHUNK = 64
SB_BLOCK = 256
SB_PROMPT_LANES = 256
MOE_BLOCK = 256
GATHER_TILE = 256


def _cparams(*sem):
    return pltpu.CompilerParams(dimension_semantics=sem, vmem_limit_bytes=VMEM_LIMIT)


def _log_sigmoid(z):
    return jnp.minimum(z, 0.0) - jnp.log(1.0 + jnp.exp(-jnp.abs(z)))


def _softplus(z):
    return jnp.maximum(z, 0.0) + jnp.log1p(jnp.exp(-jnp.abs(z)))


def _sigmoid(z):
    return 1.0 / (1.0 + jnp.exp(-z))


def _rms_mod(x, gain, shift, scale):
    ms = jnp.mean(x * x, axis=-1, keepdims=True)
    y = x * lax.rsqrt(ms + EPS) * gain
    return y * (1.0 + scale) + shift


def _row_tile(n_rows, t_len, target):
    if t_len % target == 0:
        return target
    return n_rows


def _mod_operand(mod, n_rows, t_len, tile):
    d = mod.shape[-1]
    if t_len % tile == 0:
        per_seq = t_len // tile
        return mod[:, None, :], (lambda nidx: pl.BlockSpec((None, 1, d), lambda *i: (i[nidx] // per_seq, 0, 0)))
    rows = jnp.repeat(mod, t_len, axis=0)
    return rows, (lambda nidx: pl.BlockSpec((tile, d), lambda *i: (i[nidx], 0)))


def _ada_kernel(c_ref, w_ref, b_ref, o_ref):
    c = c_ref[...]
    cond = c * _sigmoid(c)
    o_ref[...] = jnp.dot(cond.astype(BF16), w_ref[...].astype(BF16), preferred_element_type=F32) + b_ref[...]


def _ada(c, w_ada, b_ada):
    depth, d, e = w_ada.shape
    b = c.shape[0]
    tn = d
    return pl.pallas_call(
        _ada_kernel,
        grid=(depth, e // tn),
        in_specs=[pl.BlockSpec((b, d), lambda l, j: (0, 0)),
                  pl.BlockSpec((None, d, tn), lambda l, j: (l, 0, j)),
                  pl.BlockSpec((None, 1, tn), lambda l, j: (l, 0, j))],
        out_specs=pl.BlockSpec((None, b, tn), lambda l, j: (l, 0, j)),
        out_shape=jax.ShapeDtypeStruct((depth, b, e), F32),
        compiler_params=_cparams("parallel", "parallel"),
        name="ada_mod",
    )(c, w_ada, b_ada[:, None, :])


def _norm_proj_kernel(out_plan, x_ref, gain_ref, sh_ref, sc_ref, *refs):
    n_w = len(out_plan)
    w_refs, o_refs = refs[:n_w], refs[n_w:]
    h = _rms_mod(x_ref[...], gain_ref[...], sh_ref[...], sc_ref[...]).astype(BF16)
    k = 0
    for w_ref, (scale, dtypes) in zip(w_refs, out_plan):
        y = jnp.dot(h, w_ref[...], preferred_element_type=F32)
        if scale != 1.0:
            y = y * scale
        for dt, heads in dtypes:
            out = y.astype(dt)
            o_refs[k][...] = out if heads is None else out.reshape(o_refs[k].shape)
            k += 1


def _norm_proj(x, t_len, gain, shift, scale, pieces):
    n, d = x.shape
    tm = _row_tile(n, t_len, ROW_TILE)
    sh_arr, sh_spec = _mod_operand(shift, n, t_len, tm)
    sc_arr, sc_spec = _mod_operand(scale, n, t_len, tm)
    in_specs = [pl.BlockSpec((tm, d), lambda i: (i, 0)),
                pl.BlockSpec((1, d), lambda i: (0, 0)),
                sh_spec(0), sc_spec(0)]
    out_specs, out_shapes, plan, weights = [], [], [], []
    for w, s, dtypes in pieces:
        e = w.shape[1]
        in_specs.append(pl.BlockSpec((d, e), lambda i: (0, 0)))
        weights.append(w)
        plan.append((s, tuple(dtypes)))
        for dt, heads in dtypes:
            if heads is None:
                out_specs.append(pl.BlockSpec((tm, e), lambda i: (i, 0)))
                out_shapes.append(jax.ShapeDtypeStruct((n, e), dt))
            else:
                out_specs.append(pl.BlockSpec((tm, heads, e // heads), lambda i: (i, 0, 0)))
                out_shapes.append(jax.ShapeDtypeStruct((n, heads, e // heads), dt))
    return pl.pallas_call(
        functools.partial(_norm_proj_kernel, tuple(plan)),
        grid=(n // tm,),
        in_specs=in_specs, out_specs=out_specs, out_shape=out_shapes,
        compiler_params=_cparams("parallel"),
        name="norm_proj",
    )(x, gain[None, :], sh_arr, sc_arr, *weights)


def _shift_rows(x, s, fill):
    rows = lax.broadcasted_iota(I32, x.shape, 0)
    return jnp.where(rows >= s, pltpu.roll(x, s, axis=0), fill)


def _cumsum_rows(x):
    n = x.shape[0]
    s = 1
    while s < n:
        x = x + _shift_rows(x, s, 0.0)
        s *= 2
    return x


def _linear_scan_rows(a, b, h_init):
    n, w = a.shape
    a = a.reshape(n // SUBLANES, SUBLANES, w)
    b = b.reshape(n // SUBLANES, SUBLANES, w)
    in_group = lax.broadcasted_iota(I32, a.shape, 1)
    s = 1
    while s < SUBLANES:
        keep = in_group >= s
        b = a * jnp.where(keep, pltpu.roll(b, s, axis=1), 0.0) + b
        a = a * jnp.where(keep, pltpu.roll(a, s, axis=1), 1.0)
        s *= 2
    groups, carry = [], h_init
    for g in range(n // SUBLANES):
        h = a[g] * carry + b[g]
        groups.append(h)
        carry = h[SUBLANES - 1:SUBLANES, :]
    return jnp.concatenate(groups, axis=0)


def _gelu_tanh(x):
    return 0.5 * x * (1.0 + jnp.tanh(0.7978845608028654 * (x + 0.044715 * (x * x * x))))


def _ab_mixer_kernel(first_pos_zero, tt, chunk,
                     qk_ref, v_ref, g_ref, gk_ref, xr_ref, xg_ref, s0_ref, h0_ref, buf0_ref,
                     wgk_ref, bgk_ref, gnorm_ref, cw_ref, cb_ref, wa_ref, wi_ref, ba_ref, bi_ref, lam_ref,
                     mix_ref, s_out_ref, h_out_ref, buf_out_ref,
                     st_ref, hc_ref, xpad_ref):
    ti = pl.program_id(1)
    n_t = pl.num_programs(1)
    dqk = qk_ref.shape[1] // 2
    dv = v_ref.shape[1]
    hk = dqk // GLA_HEADS
    hv = dv // GLA_HEADS
    w_lru = xr_ref.shape[1]

    @pl.when(ti == 0)
    def _():
        st_ref[...] = s0_ref[...]
        hc_ref[...] = h0_ref[...]
        xpad_ref[0:SUBLANES, :] = buf0_ref[...]

    xpad_ref[SUBLANES:SUBLANES + tt, :] = xr_ref[...]
    xc = cb_ref[...]
    for i in range(CONV_WIDTH):
        off = SUBLANES - (CONV_WIDTH - 1) + i
        xc = xc + cw_ref[i:i + 1, :] * xpad_ref[off:off + tt, :]
    tail = xpad_ref[tt:tt + SUBLANES, :]
    xpad_ref[0:SUBLANES, :] = tail
    xc_b = xc.astype(BF16)
    half = w_lru // 2
    r_lin = jnp.concatenate([jnp.dot(xc_b[:, j * half:(j + 1) * half], wa_ref[j], preferred_element_type=F32)
                             for j in range(2)], axis=-1)
    i_lin = jnp.concatenate([jnp.dot(xc_b[:, j * half:(j + 1) * half], wi_ref[j], preferred_element_type=F32)
                             for j in range(2)], axis=-1)
    r = _sigmoid(r_lin + ba_ref[...])
    i_g = _sigmoid(i_lin + bi_ref[...])
    log_at = (-LRU_C) * r * _softplus(-lam_ref[...])
    a = jnp.exp(log_at)
    om = 1.0 - a * a
    mult = jnp.where(om > 0.0, om * lax.rsqrt(om), 0.0)
    if first_pos_zero:
        rows = lax.broadcasted_iota(I32, mult.shape, 0)
        mult = jnp.where((rows == 0) & (ti == 0), 1.0, mult)
    bterm = mult * (i_g * xc)
    hs = _linear_scan_rows(a, bterm, hc_ref[...])
    hc_ref[...] = hs[tt - 1:tt, :]
    y_b = hs * _gelu_tanh(xg_ref[...])
    mix_ref[:, dv:] = y_b.astype(BF16)

    lane_k = lax.broadcasted_iota(I32, (1, dqk), 1)
    st_rows = lax.broadcasted_iota(I32, (dv, dqk), 0)
    st_cols = lax.broadcasted_iota(I32, (dv, dqk), 1)
    st_mask = functools.reduce(
        jnp.logical_or,
        [(st_rows >= h * hv) & (st_rows < (h + 1) * hv) & (st_cols >= h * hk) & (st_cols < (h + 1) * hk)
         for h in range(GLA_HEADS)])
    crow = lax.broadcasted_iota(I32, (chunk, chunk), 0)
    ccol = lax.broadcasted_iota(I32, (chunk, chunk), 1)
    causal = ccol <= crow
    mid = chunk // 2 - 1
    for c in range(tt // chunk):
        rs = slice(c * chunk, (c + 1) * chunk)
        q = qk_ref[rs, 0:dqk] * (hk ** -0.5)
        k = qk_ref[rs, dqk:2 * dqk]
        v_b = v_ref[rs, :].astype(BF16)
        u = jnp.dot(gk_ref[rs, :].astype(BF16), wgk_ref[...], preferred_element_type=F32) + bgk_ref[...]
        b = _cumsum_rows(_log_sigmoid(u) * (1.0 / GLA_GATE_TAU))
        b_mid = b[mid:mid + 1, :]
        b_last = b[chunk - 1:chunk, :]
        q_e = q * jnp.exp(b - b_mid)
        k_e = (k * jnp.exp(b_mid - b)).astype(BF16)
        k_tail = (k * jnp.exp(b_last - b)).astype(BF16)
        q_dec = (q_e * jnp.exp(b_mid)).astype(BF16)
        q_e = q_e.astype(BF16)
        st = st_ref[...]
        o = lax.dot_general(q_dec, st.astype(BF16), (((1,), (1,)), ((), ())), preferred_element_type=F32)
        o_intra = []
        for h in range(GLA_HEADS):
            q_h = jnp.where((lane_k >= h * hk) & (lane_k < (h + 1) * hk), q_e, jnp.zeros_like(q_e))
            att = lax.dot_general(q_h, k_e, (((1,), (1,)), ((), ())), preferred_element_type=F32)
            att = jnp.where(causal, att, 0.0).astype(BF16)
            o_intra.append(jnp.dot(att, v_b[:, h * hv:(h + 1) * hv], preferred_element_type=F32))
        o = o + jnp.concatenate(o_intra, axis=-1)
        kv = lax.dot_general(v_b, k_tail, (((0,), (0,)), ((), ())), preferred_element_type=F32)
        st_ref[...] = st * jnp.exp(b_last) + jnp.where(st_mask, kv, 0.0)
        g = g_ref[rs, :]
        gate = g * _sigmoid(g)
        y_a = []
        for h in range(GLA_HEADS):
            o_h = o[:, h * hv:(h + 1) * hv]
            ms = jnp.mean(o_h * o_h, axis=-1, keepdims=True)
            y_a.append(o_h * lax.rsqrt(ms + EPS) * gnorm_ref[...] * gate[:, h * hv:(h + 1) * hv])
        mix_ref[rs, 0:dv] = jnp.concatenate(y_a, axis=-1).astype(BF16)

    @pl.when(ti == n_t - 1)
    def _():
        s_out_ref[...] = st_ref[...]
        h_out_ref[...] = hc_ref[...]
        buf_out_ref[...] = xpad_ref[0:SUBLANES, :]


def _ab_mixer(proj, bsz, t_len, first_pos_zero, s_gla, h_lru, conv_buf, w_gk2, b_gk, gla_norm, conv_w, conv_b,
              lru_wa, lru_ba, lru_wi, lru_bi, lru_lam):
    qk, v, g, gk, xr, xg = proj
    n = qk.shape[0]
    dqk, dv, w_lru = qk.shape[1] // 2, v.shape[1], xr.shape[1]
    hk, hv = dqk // GLA_HEADS, dv // GLA_HEADS
    tt = MIX_TILE if t_len % MIX_TILE == 0 else t_len
    chunk = GLA_CHUNK if tt % GLA_CHUNK == 0 else tt
    n_t = t_len // tt

    eye = jnp.eye(GLA_HEADS, dtype=F32)
    st0 = jnp.einsum('bhkv,hg->bhvgk', s_gla.astype(F32), eye).reshape(bsz, dv, dqk)
    buf0 = jnp.pad(conv_buf.astype(F32), ((0, 0), (SUBLANES - (CONV_WIDTH - 1), 0), (0, 0)))
    wgk = jnp.pad(w_gk2, ((0, gk.shape[1] - w_gk2.shape[0]), (0, 0))).astype(BF16)
    nb = lru_wa.shape[0]
    half_blocks = nb // 2

    def block_diag(w):
        bd = w.shape[1]
        e = jnp.eye(half_blocks, dtype=w.dtype)
        w2 = w.reshape(2, half_blocks, bd, bd)
        return jnp.einsum('jnio,nm->jnimo', w2, e).reshape(2, half_blocks * bd, half_blocks * bd).astype(BF16)

    row = lambda z: z.astype(F32)[None, :]
    tok = lambda w: pl.BlockSpec((tt, w), lambda b, i: (b * n_t + i, 0))
    per_b = lambda r, c: pl.BlockSpec((None, r, c), lambda b, i: (b, 0, 0))
    full2 = lambda r, c: pl.BlockSpec((r, c), lambda b, i: (0, 0))
    full3 = lambda a, r, c: pl.BlockSpec((a, r, c), lambda b, i: (0, 0, 0))
    mix, st, h_new, buf = pl.pallas_call(
        functools.partial(_ab_mixer_kernel, first_pos_zero, tt, chunk),
        grid=(bsz, n_t),
        in_specs=[tok(2 * dqk), tok(dv), tok(dv), tok(gk.shape[1]), tok(w_lru), tok(w_lru),
                  per_b(dv, dqk), per_b(1, w_lru), per_b(SUBLANES, w_lru),
                  full2(gk.shape[1], dqk), full2(1, dqk), full2(1, hv), full2(CONV_WIDTH, w_lru), full2(1, w_lru),
                  full3(2, w_lru // 2, w_lru // 2), full3(2, w_lru // 2, w_lru // 2),
                  full2(1, w_lru), full2(1, w_lru), full2(1, w_lru)],
        out_specs=[tok(dv + w_lru), per_b(dv, dqk), per_b(1, w_lru), per_b(SUBLANES, w_lru)],
        out_shape=[jax.ShapeDtypeStruct((n, dv + w_lru), BF16),
                   jax.ShapeDtypeStruct((bsz, dv, dqk), F32),
                   jax.ShapeDtypeStruct((bsz, 1, w_lru), F32),
                   jax.ShapeDtypeStruct((bsz, SUBLANES, w_lru), F32)],
        scratch_shapes=[pltpu.VMEM((dv, dqk), F32), pltpu.VMEM((1, w_lru), F32),
                        pltpu.VMEM((tt + SUBLANES, w_lru), F32)],
        compiler_params=_cparams("parallel", "arbitrary"),
        name="ab_mixer",
    )(qk, v, g, gk, xr, xg, st0, h_lru.astype(F32)[:, None, :], buf0,
      wgk, row(b_gk), row(gla_norm), conv_w.astype(F32), row(conv_b),
      block_diag(lru_wa), block_diag(lru_wi), row(lru_ba), row(lru_bi), row(lru_lam))
    st5 = st.reshape(bsz, GLA_HEADS, hv, GLA_HEADS, hk)
    s_new = jnp.stack([st5[:, h, :, h, :] for h in range(GLA_HEADS)], axis=1).swapaxes(-1, -2)
    return mix, s_new, h_new[:, 0], buf[:, SUBLANES - (CONV_WIDTH - 1):]


SB_MASKED = -1e30
SB_DEAD = -152.0
LOG2_E = 1.4426950408889634


def _sb_logits(q_h, k_blk):
    return lax.dot_general(q_h, k_blk, (((1,), (1,)), ((), ())), preferred_element_type=F32)


def _sb_gates(z, mask):
    neg_abs = pltpu.bitcast(pltpu.bitcast(z, I32) | jnp.int32(-2 ** 31), F32)
    ls = jnp.minimum(z, 0.0) - jnp.log2(1.0 + jnp.exp2(neg_abs))
    lk = ls - z
    if mask is not None:
        lk = jnp.where(mask, lk, 0.0)
    return ls, lk.astype(BF16), jnp.sum(lk, axis=-1, keepdims=True)


def _sb_prefix(lk, upper, suffix, mask):
    pre = jnp.dot(lk, upper, preferred_element_type=F32) + suffix
    if mask is not None:
        pre = jnp.where(mask, pre, SB_MASKED)
    return pre


def _sb_values(ls, pre, v_blk):
    return jnp.dot(jnp.exp2(ls + pre).astype(BF16), v_blk, preferred_element_type=F32)


def _split_heads(q, dh):
    lane = lax.broadcasted_iota(I32, (1, q.shape[1]), 1)
    zero = jnp.zeros_like(q)
    return [jnp.where((lane >= h * dh) & (lane < (h + 1) * dh), q, zero) for h in range(q.shape[1] // dh)], lane


def _merge_heads(acc, lane, dh):
    out = acc[0]
    for h in range(1, len(acc)):
        out = jnp.where(lane >= h * dh, acc[h], out)
    return out


def _upper(n):
    r = lax.broadcasted_iota(I32, (n, n), 0)
    c = lax.broadcasted_iota(I32, (n, n), 1)
    return jnp.where(r > c, 1.0, 0.0).astype(BF16)


def _sb_prompt_kernel(blk, dh, q_ref, k_ref, v_ref, o_ref, acc_ref, suf_ref):
    qi = pl.program_id(2)
    n_blk = qi + 1
    q_heads, lane = _split_heads(q_ref[...], dh)
    upper = _upper(blk)
    diag = lax.broadcasted_iota(I32, (blk, blk), 1) < lax.broadcasted_iota(I32, (blk, blk), 0)

    def key_rows(s):
        return pl.ds(pl.multiple_of((qi - s) * blk, blk), blk)

    def block(s, mask):
        k_blk = k_ref[key_rows(s), :]
        v_blk = v_ref[key_rows(s), :]
        for h in range(len(q_heads)):
            ls, lk, tot = _sb_gates(_sb_logits(q_heads[h], k_blk), mask)
            acc_ref[h] += _sb_values(ls, _sb_prefix(lk, upper, suf_ref[h], mask), v_blk)
            suf_ref[h] += tot

    def live():
        return (jnp.max(suf_ref[...]) > SB_DEAD).astype(I32)

    acc_ref[...] = jnp.zeros_like(acc_ref)
    suf_ref[...] = jnp.zeros_like(suf_ref)
    block(0, diag)

    def more(carry):
        s, go = carry
        return (s < n_blk) & (go > 0)

    def step(carry):
        s, _ = carry
        block(s, None)
        return s + 1, live()

    lax.while_loop(more, step, (jnp.int32(1), live()))
    o_ref[...] = _merge_heads([acc_ref[h] for h in range(len(q_heads))], lane, dh).astype(o_ref.dtype)


def _sb_prompt(q, k, v, bsz, t_len, dh):
    n, width = q.shape
    blk = SB_BLOCK
    nq = t_len // blk
    lanes = SB_PROMPT_LANES
    groups = width // lanes
    heads = lanes // dh
    return pl.pallas_call(
        functools.partial(_sb_prompt_kernel, blk, dh),
        grid=(bsz, groups, nq),
        in_specs=[pl.BlockSpec((blk, lanes), lambda b, g, i: (b * nq + i, g)),
                  pl.BlockSpec((t_len, lanes), lambda b, g, i: (b, g)),
                  pl.BlockSpec((t_len, lanes), lambda b, g, i: (b, g))],
        out_specs=pl.BlockSpec((blk, lanes), lambda b, g, i: (b * nq + i, g)),
        out_shape=jax.ShapeDtypeStruct((n, width), BF16),
        scratch_shapes=[pltpu.VMEM((heads, blk, lanes), F32), pltpu.VMEM((heads, blk, 1), F32)],
        compiler_params=_cparams("parallel", "parallel", "arbitrary"),
        name="sb_prompt",
    )(q, k, v)


def _sb_sample_kernel(t_new, past_len, blk, dh, lanes, q_ref, k_ref, v_ref, pk_ref, pv_ref, o_ref, acc_ref, suf_ref):
    width = q_ref.shape[1]
    per_group = lanes // dh
    q = q_ref[...]
    q_heads, lane = [], None
    for g in range(width // lanes):
        heads_g, lane = _split_heads(q[:, g * lanes:(g + 1) * lanes], dh)
        q_heads.append(heads_g)
    diag = lax.broadcasted_iota(I32, (t_new, t_new), 1) < lax.broadcasted_iota(I32, (t_new, t_new), 0)

    def block(k_rows, v_rows, upper, mask):
        for g, heads_g in enumerate(q_heads):
            k_blk = k_rows[:, g * lanes:(g + 1) * lanes]
            v_blk = v_rows[:, g * lanes:(g + 1) * lanes]
            for h, q_h in enumerate(heads_g):
                i = g * per_group + h
                ls, lk, tot = _sb_gates(_sb_logits(q_h, k_blk), mask)
                acc_ref[i] += _sb_values(ls, _sb_prefix(lk, upper, suf_ref[i], mask), v_blk)
                suf_ref[i] += tot

    def live():
        return (jnp.max(suf_ref[...]) > SB_DEAD).astype(I32)

    acc_ref[...] = jnp.zeros_like(acc_ref)
    suf_ref[...] = jnp.zeros_like(suf_ref)
    block(k_ref[...], v_ref[...], _upper(t_new), diag)
    upper = _upper(blk)

    def more(carry):
        j, go = carry
        return (j >= 0) & (go > 0)

    def step(carry):
        j, _ = carry
        rows = pl.ds(pl.multiple_of(j * blk, blk), blk)
        block(pk_ref[rows].reshape(blk, width).astype(BF16), pv_ref[rows].reshape(blk, width).astype(BF16),
              upper, None)
        return j - 1, live()

    lax.while_loop(more, step, (jnp.int32(past_len // blk - 1), live()))
    o_ref[...] = jnp.concatenate(
        [_merge_heads([acc_ref[g * per_group + h] for h in range(per_group)], lane, dh)
         for g in range(width // lanes)], axis=-1).astype(o_ref.dtype)


def _sb_sample(q, k, v, past_k, past_v, bsz, t_len):
    n, width = q.shape
    _, past_len, heads, dh = past_k.shape
    blk = SB_BLOCK if past_len % SB_BLOCK == 0 else past_len
    lanes = SB_PROMPT_LANES
    new = pl.BlockSpec((t_len, width), lambda b: (b, 0))
    past = pl.BlockSpec((None, past_len, heads, dh), lambda b: (b, 0, 0, 0))
    return pl.pallas_call(
        functools.partial(_sb_sample_kernel, t_len, past_len, blk, dh, lanes),
        grid=(bsz,),
        in_specs=[new, new, new, past, past],
        out_specs=new,
        out_shape=jax.ShapeDtypeStruct((n, width), BF16),
        scratch_shapes=[pltpu.VMEM((heads, t_len, lanes), F32), pltpu.VMEM((heads, t_len, 1), F32)],
        compiler_params=_cparams("parallel"),
        name="sb_sample",
    )(q, k, v, past_k, past_v)


def _first_argmax(vals):
    best_v, best_i = vals[0], jnp.zeros(vals[0].shape, I32)
    for i in range(1, len(vals)):
        better = vals[i] > best_v
        best_v = jnp.where(better, vals[i], best_v)
        best_i = jnp.where(better, i, best_i)
    return best_v, best_i


def _route_rows(logits_t, bias_col):
    s = _sigmoid(logits_t)
    sel = s + bias_col
    epg = EXPERTS_PER_GROUP
    scores = []
    for gi in range(N_GROUPS):
        rows = [sel[gi * epg + i:gi * epg + i + 1, :] for i in range(epg)]
        pair_sums = [rows[i] + rows[j] for i in range(epg) for j in range(i + 1, epg)]
        scores.append(functools.reduce(jnp.maximum, pair_sums))
    _, best = _first_argmax(scores)

    def in_best(mat, i):
        out = mat[i:i + 1, :]
        for gi in range(1, N_GROUPS):
            out = jnp.where(best == gi, mat[gi * epg + i:gi * epg + i + 1, :], out)
        return out

    sel_g = [in_best(sel, i) for i in range(epg)]
    s_g = [in_best(s, i) for i in range(epg)]
    _, i1 = _first_argmax(sel_g)
    _, i2 = _first_argmax([jnp.where(i1 == i, -jnp.inf, sel_g[i]) for i in range(epg)])
    lo = jnp.minimum(i1, i2)
    hi = jnp.maximum(i1, i2)
    pick = lambda idx: functools.reduce(lambda acc, i: jnp.where(idx == i, s_g[i], acc), range(1, epg), s_g[0])
    w_lo, w_hi = pick(lo), pick(hi)
    tot = w_lo + w_hi
    pair = jnp.where(lo == 0, 0, jnp.where(lo == 1, epg - 1, 2 * epg - 3)) + (hi - lo - 1)
    return best * N_PAIRS + pair, w_lo / tot, w_hi / tot


def _proj_route_kernel(a_ref, w_ref, x_ref, g1_ref, gain_ref, sh_ref, sc_ref, wr_hi_ref, wr_lo_ref, br_ref,
                       xn_ref, h2_ref, ri_ref):
    y = jnp.dot(a_ref[...], w_ref[...], preferred_element_type=F32)
    xn = x_ref[...] + g1_ref[...] * y
    xn_ref[...] = xn
    d = xn.shape[1]
    h2 = _rms_mod(xn, gain_ref[...], sh_ref[...], sc_ref[...])
    h2_ref[:, 0:d] = h2
    h_hi = h2.astype(BF16)
    h_lo = (h2 - h_hi.astype(F32)).astype(BF16)
    logits = (jnp.dot(h_hi, wr_hi_ref[...], preferred_element_type=F32)
              + jnp.dot(h_lo, wr_hi_ref[...], preferred_element_type=F32)
              + jnp.dot(h_hi, wr_lo_ref[...], preferred_element_type=F32))
    n_e = br_ref.shape[0]
    logits_t = logits.T[0:n_e, :]
    cls, w_lo, w_hi = _route_rows(logits_t, br_ref[...])
    tm = logits_t.shape[1]
    h2_ref[:, d:] = jnp.concatenate([w_lo, w_hi, jnp.zeros((LANES - 2, tm), F32)], axis=0).T
    ri_ref[...] = jnp.concatenate([cls, jnp.zeros((SUBLANES - 1, tm), I32)], axis=0)


def _proj_route(a, w_out, x, t_len, g1, gain, shift, scale, w_router, b_router):
    n, d = x.shape
    k = a.shape[1]
    n_e = w_router.shape[1]
    tm = _row_tile(n, t_len, ROW_TILE)
    g1_arr, g1_spec = _mod_operand(g1, n, t_len, tm)
    sh_arr, sh_spec = _mod_operand(shift, n, t_len, tm)
    sc_arr, sc_spec = _mod_operand(scale, n, t_len, tm)
    wr = jnp.pad(w_router.astype(F32), ((0, 0), (0, LANES - n_e)))
    wr_hi = wr.astype(BF16)
    wr_lo = (wr - wr_hi.astype(F32)).astype(BF16)
    tokd = pl.BlockSpec((tm, d), lambda i: (i, 0))
    const = lambda r, c: pl.BlockSpec((r, c), lambda i: (0, 0))
    route = pl.BlockSpec((SUBLANES, tm), lambda i: (0, i))
    return pl.pallas_call(
        _proj_route_kernel,
        grid=(n // tm,),
        in_specs=[pl.BlockSpec((tm, k), lambda i: (i, 0)), const(k, d), tokd, g1_spec(0), const(1, d),
                  sh_spec(0), sc_spec(0), const(d, LANES), const(d, LANES), const(n_e, 1)],
        out_specs=[tokd, pl.BlockSpec((tm, d + LANES), lambda i: (i, 0)), route],
        out_shape=[jax.ShapeDtypeStruct((n, d), F32), jax.ShapeDtypeStruct((n, d + LANES), F32),
                   jax.ShapeDtypeStruct((SUBLANES, n), I32)],
        compiler_params=_cparams("parallel"),
        name="proj_route",
    )(a, w_out, x, g1_arr, gain[None, :], sh_arr, sc_arr, wr_hi, wr_lo, b_router.astype(F32)[:, None])


def _gather_rows(src_hbm, idx_ref, base, n_rows, dst, sem):
    last = idx_ref.shape[0] - 1

    for r in range(n_rows):
        row = idx_ref[jnp.minimum(base + r, last)]
        pltpu.make_async_copy(src_hbm.at[pl.ds(row, 1)], dst.at[pl.ds(r, 1)], sem).start()


def _wait_rows(src_hbm, n_rows, dst, sem):
    pltpu.make_async_copy(src_hbm.at[pl.ds(0, n_rows)], dst, sem).wait()


def _moe_kernel(bm, d, order_ref, base_ref, valid_ref, e_lo_ref, e_hi_ref, used_ref,
                h_hbm, w1a_ref, w3a_ref, w2a_ref, w1b_ref, w3b_ref, w2b_ref,
                y_ref, buf0_ref, buf1_ref, sem_ref):
    i = pl.program_id(0)
    n_used = used_ref[0]

    @pl.when((i == 0) & (n_used > 0))
    def _():
        _gather_rows(h_hbm, order_ref, base_ref[0], bm, buf0_ref, sem_ref.at[0])

    def block(cur, cur_sem, nxt, nxt_sem):
        _wait_rows(h_hbm, bm, cur, cur_sem)
        xb = cur[:, 0:d].astype(BF16)
        live = lax.broadcasted_iota(I32, (bm, LANES), 0) < valid_ref[i]
        wts = jnp.where(live, cur[:, d:], 0.0)
        _gather_rows(h_hbm, order_ref, base_ref[i + 1], bm, nxt, nxt_sem)

        def expert(w1_ref, w3_ref, w2_ref):
            u = jnp.dot(xb, w1_ref[...], preferred_element_type=F32)
            t = jnp.dot(xb, w3_ref[...], preferred_element_type=F32)
            mid = (u * _sigmoid(u) * t).astype(BF16)
            return jnp.dot(mid, w2_ref[...], preferred_element_type=F32)

        y_ref[...] = (expert(w1a_ref, w3a_ref, w2a_ref) * wts[:, 0:1]
                      + expert(w1b_ref, w3b_ref, w2b_ref) * wts[:, 1:2])

        @pl.when(i == n_used - 1)
        def _():
            _wait_rows(h_hbm, bm, nxt, nxt_sem)

    @pl.when((i < n_used) & (i % 2 == 0))
    def _():
        block(buf0_ref, sem_ref.at[0], buf1_ref, sem_ref.at[1])

    @pl.when((i < n_used) & (i % 2 == 1))
    def _():
        block(buf1_ref, sem_ref.at[1], buf0_ref, sem_ref.at[0])

    @pl.when(i >= n_used)
    def _():
        y_ref[...] = jnp.zeros_like(y_ref)


def _moe_sorted(h2w, order, blk_base, blk_valid, blk_lo, blk_hi, n_used, w1, w3, w2):
    n, dw = h2w.shape
    d = dw - LANES
    n_blocks = blk_valid.shape[0]
    bm = MOE_BLOCK
    de = w1.shape[2]

    def wspec(which, r, c):
        def index(i, order, base, valid, lo, hi, used):
            blk = jnp.minimum(i, jnp.maximum(used[0] - 1, 0))
            return ((lo, hi)[which][blk], 0, 0)
        return pl.BlockSpec((None, r, c), index)

    return pl.pallas_call(
        functools.partial(_moe_kernel, bm, d),
        grid_spec=pltpu.PrefetchScalarGridSpec(
            num_scalar_prefetch=6,
            grid=(n_blocks,),
            in_specs=[pl.BlockSpec(memory_space=pl.ANY),
                      wspec(0, d, de), wspec(0, d, de), wspec(0, de, d),
                      wspec(1, d, de), wspec(1, d, de), wspec(1, de, d)],
            out_specs=pl.BlockSpec((bm, d), lambda i, *_: (i, 0)),
            scratch_shapes=[pltpu.VMEM((bm, dw), F32), pltpu.VMEM((bm, dw), F32), pltpu.SemaphoreType.DMA((2,))]),
        out_shape=jax.ShapeDtypeStruct((n_blocks * bm, d), F32),
        compiler_params=_cparams("arbitrary"),
        name="moe_sorted",
    )(order, blk_base, blk_valid, blk_lo, blk_hi, n_used, h2w, w1, w3, w2, w1, w3, w2)


def _unsort_kernel(tm, final_norm, pos_ref, y_hbm, x_ref, g2_ref, gain_ref, o_ref, buf0_ref, buf1_ref, sem_ref):
    i = pl.program_id(0)
    n_i = pl.num_programs(0)

    @pl.when(i == 0)
    def _():
        _gather_rows(y_hbm, pos_ref, 0, tm, buf0_ref, sem_ref.at[0])

    def step(cur, cur_sem, nxt, nxt_sem):
        @pl.when(i + 1 < n_i)
        def _():
            _gather_rows(y_hbm, pos_ref, (i + 1) * tm, tm, nxt, nxt_sem)

        _wait_rows(y_hbm, tm, cur, cur_sem)
        x = x_ref[...] + g2_ref[...] * cur[...]
        if final_norm:
            ms = jnp.mean(x * x, axis=-1, keepdims=True)
            x = x * lax.rsqrt(ms + EPS) * gain_ref[...]
        o_ref[...] = x

    @pl.when(i % 2 == 0)
    def _():
        step(buf0_ref, sem_ref.at[0], buf1_ref, sem_ref.at[1])

    @pl.when(i % 2 == 1)
    def _():
        step(buf1_ref, sem_ref.at[1], buf0_ref, sem_ref.at[0])


def _unsort_resid(y_sorted, pos, x, t_len, g2, final_gain):
    n, d = x.shape
    tm = _row_tile(n, t_len, GATHER_TILE)
    g2_arr, g2_spec = _mod_operand(g2, n, t_len, tm)
    final_norm = final_gain is not None
    gain = (final_gain if final_norm else jnp.ones((d,), F32)).astype(F32)[None, :]
    return pl.pallas_call(
        functools.partial(_unsort_kernel, tm, final_norm),
        grid_spec=pltpu.PrefetchScalarGridSpec(
            num_scalar_prefetch=1,
            grid=(n // tm,),
            in_specs=[pl.BlockSpec(memory_space=pl.ANY),
                      pl.BlockSpec((tm, d), lambda i, pos: (i, 0)),
                      g2_spec(0),
                      pl.BlockSpec((1, d), lambda i, pos: (0, 0))],
            out_specs=pl.BlockSpec((tm, d), lambda i, pos: (i, 0)),
            scratch_shapes=[pltpu.VMEM((tm, d), F32), pltpu.VMEM((tm, d), F32), pltpu.SemaphoreType.DMA((2,))]),
        out_shape=jax.ShapeDtypeStruct((n, d), F32),
        compiler_params=_cparams("arbitrary"),
        name="unsort_resid",
    )(pos, y_sorted, x, g2_arr, gain)


def _sort_plan(cls):
    n = cls.shape[0]
    bm = MOE_BLOCK
    n_blocks = -(-n // bm) + N_CLASSES
    classes = jnp.arange(N_CLASSES, dtype=I32)
    cls_sorted, order = lax.sort((cls, jnp.arange(n, dtype=I32)), num_keys=1, is_stable=True)
    counts = jnp.sum((cls[:, None] == classes[None, :]).astype(I32), axis=0)
    padded = (counts + bm - 1) // bm * bm
    pad_end = jnp.cumsum(padded)
    pad_start = pad_end - padded
    start = jnp.cumsum(counts) - counts
    shift = jnp.sum(jnp.where(cls_sorted[:, None] == classes[None, :], (pad_start - start)[None, :], 0), axis=1)
    dest = jnp.arange(n, dtype=I32) + shift
    _, pos = lax.sort((order, dest), num_keys=1)
    blk_first = jnp.arange(n_blocks + 1, dtype=I32) * bm
    blk_cls = jnp.minimum(jnp.sum((pad_end[None, :] <= blk_first[:, None]).astype(I32), axis=1), N_CLASSES - 1)
    onehot = blk_cls[:, None] == classes[None, :]
    pick = lambda table: jnp.sum(jnp.where(onehot, table[None, :], 0), axis=1)
    into = blk_first - pick(pad_start)
    blk_base = jnp.clip(pick(start) + into, 0, n)
    blk_valid = jnp.clip(pick(counts) - into, 0, bm)[:n_blocks]
    pair_lo = jnp.array([i for i in range(EXPERTS_PER_GROUP) for j in range(i + 1, EXPERTS_PER_GROUP)], I32)
    pair_hi = jnp.array([j for i in range(EXPERTS_PER_GROUP) for j in range(i + 1, EXPERTS_PER_GROUP)], I32)
    group = classes // N_PAIRS
    blk_lo = pick(group * EXPERTS_PER_GROUP + pair_lo[classes % N_PAIRS])
    blk_hi = pick(group * EXPERTS_PER_GROUP + pair_hi[classes % N_PAIRS])
    n_used = (pad_end[-1] // bm).astype(I32)[None]
    return order, pos, blk_base.astype(I32), blk_valid.astype(I32), blk_lo, blk_hi, n_used


def _moe_layer(h2w, route_i, x, t_len, g2, w1, w3, w2, final_gain):
    order, pos, blk_base, blk_valid, blk_lo, blk_hi, n_used = _sort_plan(route_i[0])
    y_sorted = _moe_sorted(h2w, order, blk_base, blk_valid, blk_lo, blk_hi, n_used, w1, w3, w2)
    return _unsort_resid(y_sorted, pos, x, t_len, g2, final_gain)


def _trunk(x3, c, p, past):
    bsz, t_len, d = x3.shape
    n = bsz * t_len
    x = x3.reshape(n, d)
    depth = p['w_ada'].shape[0]
    mod = _ada(c.astype(F32), p['w_ada'], p['b_ada']).reshape(depth, bsz, N_MOD, d)
    gla_s, lru_s, conv_s, ks, vs = [], [], [], [], []
    for l in range(depth):
        sh1, sc1, g1, sh2, sc2, g2 = [mod[l, :, i] for i in range(N_MOD)]
        j = l // 2
        if l % 2 == 0:
            w_in = p['w_ab_in'][j]
            dqk2 = 2 * p['w_gk2'].shape[2]
            dv = GLA_HEADS * p['gla_norm'].shape[1]
            rank = p['w_gk2'].shape[1]
            w_lru = p['lru_lam'].shape[1]
            cuts = [0, dqk2, dqk2 + dv, dqk2 + 2 * dv, dqk2 + 2 * dv + rank,
                    dqk2 + 2 * dv + rank + w_lru, dqk2 + 2 * dv + rank + 2 * w_lru]
            cols = [w_in[:, cuts[i]:cuts[i + 1]] for i in range(6)]
            cols[3] = jnp.pad(cols[3], ((0, 0), (0, LANES - rank)))
            proj = _norm_proj(x, t_len, p['norm_mix'][l], sh1, sc1,
                              [(w.astype(BF16), 1.0, [(F32, None)]) for w in cols])
            if past is None:
                s0 = jnp.zeros((bsz, GLA_HEADS, dqk2 // 2 // GLA_HEADS, dv // GLA_HEADS), F32)
                h0 = jnp.zeros((bsz, w_lru), F32)
                buf = jnp.zeros((bsz, CONV_WIDTH - 1, w_lru), F32)
            else:
                s0, h0, buf = past['state_gla'][j], past['state_lru'][j], past['state_conv'][j]
            mix, sg, sl, sc = _ab_mixer(proj, bsz, t_len, past is None, s0, h0, buf, p['w_gk2'][j], p['b_gk'][j],
                                        p['gla_norm'][j], p['conv_w'][j], p['conv_b'][j], p['lru_wa'][j],
                                        p['lru_ba'][j], p['lru_wi'][j], p['lru_bi'][j], p['lru_lam'][j])
            gla_s.append(sg)
            lru_s.append(sl)
            conv_s.append(sc)
            w_out = p['w_ab_out'][j]
        else:
            w_qkv = p['w_sb_qkv'][j]
            width = w_qkv.shape[1] // 3
            heads = past['cache_k'].shape[3] if past is not None else p['sb_heads']
            dh = width // heads
            pieces = [(w_qkv[:, 0:width].astype(BF16), LOG2_E * dh ** -0.5, [(BF16, None)]),
                      (w_qkv[:, width:2 * width].astype(BF16), 1.0, [(F32, heads), (BF16, None)]),
                      (w_qkv[:, 2 * width:].astype(BF16), 1.0, [(F32, heads), (BF16, None)])]
            q_b, k_f, k_b, v_f, v_b = _norm_proj(x, t_len, p['norm_mix'][l], sh1, sc1, pieces)
            if past is None:
                mix = _sb_prompt(q_b, k_b, v_b, bsz, t_len, dh)
            else:
                mix = _sb_sample(q_b, k_b, v_b, past['cache_k'][j].astype(F32), past['cache_v'][j].astype(F32),
                                 bsz, t_len)
            ks.append(k_f.reshape(bsz, t_len, heads, dh))
            vs.append(v_f.reshape(bsz, t_len, heads, dh))
            w_out = p['w_sb_out'][j]
        xn, h2w, route_i = _proj_route(mix, w_out.astype(BF16), x, t_len, g1, p['norm_ffn'][l], sh2, sc2,
                                       p['w_router'], p['b_router'])
        x = _moe_layer(h2w, route_i, xn, t_len, g2,
                       p['w_e1'][l].astype(BF16), p['w_e3'][l].astype(BF16), p['w_e2'][l].astype(BF16),
                       p['norm_out'] if l == depth - 1 else None)
    return x.reshape(bsz, t_len, d), (jnp.stack(gla_s), jnp.stack(lru_s), jnp.stack(conv_s),
                                      jnp.stack(ks), jnp.stack(vs))


def kernel(x_prompt, x_sample, state_gla, state_lru, state_conv, cache_k, cache_v, c_prompt, c_sample,
           w_ada, b_ada, norm_mix, norm_ffn, norm_out, w_ab_in, w_gk2, b_gk, gla_norm, conv_w, conv_b,
           lru_wa, lru_ba, lru_wi, lru_bi, lru_lam, w_ab_out, w_sb_qkv, w_sb_out, w_router, b_router,
           w_e1, w_e3, w_e2):
    p = dict(w_ada=w_ada, b_ada=b_ada, norm_mix=norm_mix, norm_ffn=norm_ffn, norm_out=norm_out,
             w_ab_in=w_ab_in, w_gk2=w_gk2, b_gk=b_gk, gla_norm=gla_norm, conv_w=conv_w, conv_b=conv_b,
             lru_wa=lru_wa, lru_ba=lru_ba, lru_wi=lru_wi, lru_bi=lru_bi, lru_lam=lru_lam, w_ab_out=w_ab_out,
             w_sb_qkv=w_sb_qkv, w_sb_out=w_sb_out, w_router=w_router, b_router=b_router,
             w_e1=w_e1, w_e3=w_e3, w_e2=w_e2, sb_heads=cache_k.shape[3])
    past = dict(state_gla=state_gla, state_lru=state_lru, state_conv=state_conv, cache_k=cache_k, cache_v=cache_v)
    y_prompt, (p_gla, p_lru, p_conv, p_k, p_v) = _trunk(x_prompt, c_prompt, p, None)
    y_sample, (s_gla, s_lru, s_conv, s_k, s_v) = _trunk(x_sample, c_sample, p, past)
    return (y_prompt, y_sample, p_gla, p_lru, p_conv, p_k, p_v, s_gla, s_lru, s_conv, s_k, s_v)
```

```python
import functools

import jax
import jax.numpy as jnp
from jax import lax
from jax.experimental import pallas as pl
from jax.experimental.pallas import tpu as pltpu

F32 = jnp.float32
BF16 = jnp.bfloat16
I32 = jnp.int32

EPS = 1e-6
N_MOD = 6
GLA_HEADS = 4
GLA_GATE_TAU = 16.0
LRU_C = 8.0
CONV_WIDTH = 4
N_GROUPS = 4
EXPERTS_PER_GROUP = 4
N_PAIRS = 6
N_CLASSES = N_GROUPS * N_PAIRS

LANES = 128
SUBLANES = 8
VMEM_LIMIT = 56 * 1024 * 1024

ROW_TILE = 512
MIX_TILE = 256
GLA_CHUNK = 64
SB_BLOCK = 256
SB_GROUP_LANES = 256
SB_PROMPT_WIDTH = 512
MOE_BLOCK = 256
GATHER_TILE = 256


def _cparams(*sem):
    return pltpu.CompilerParams(dimension_semantics=sem, vmem_limit_bytes=VMEM_LIMIT)


def _log_sigmoid(z):
    return jnp.minimum(z, 0.0) - jnp.log(1.0 + jnp.exp(-jnp.abs(z)))


def _softplus(z):
    return jnp.maximum(z, 0.0) + jnp.log1p(jnp.exp(-jnp.abs(z)))


def _sigmoid(z):
    return 1.0 / (1.0 + jnp.exp(-z))


def _rms_mod(x, gain, shift, scale):
    ms = jnp.mean(x * x, axis=-1, keepdims=True)
    y = x * lax.rsqrt(ms + EPS) * gain
    return y * (1.0 + scale) + shift


def _row_tile(n_rows, t_len, target):
    if t_len % target == 0:
        return target
    return n_rows


def _mod_operand(mod, n_rows, t_len, tile):
    d = mod.shape[-1]
    if t_len % tile == 0:
        per_seq = t_len // tile
        return mod[:, None, :], (lambda nidx: pl.BlockSpec((None, 1, d), lambda *i: (i[nidx] // per_seq, 0, 0)))
    rows = jnp.repeat(mod, t_len, axis=0)
    return rows, (lambda nidx: pl.BlockSpec((tile, d), lambda *i: (i[nidx], 0)))


def _ada_kernel(c_ref, w_ref, b_ref, o_ref):
    c = c_ref[...]
    cond = c * _sigmoid(c)
    o_ref[...] = jnp.dot(cond.astype(BF16), w_ref[...].astype(BF16), preferred_element_type=F32) + b_ref[...]


def _ada(c, w_ada, b_ada):
    depth, d, e = w_ada.shape
    b = c.shape[0]
    tn = d
    return pl.pallas_call(
        _ada_kernel,
        grid=(depth, e // tn),
        in_specs=[pl.BlockSpec((b, d), lambda l, j: (0, 0)),
                  pl.BlockSpec((None, d, tn), lambda l, j: (l, 0, j)),
                  pl.BlockSpec((None, 1, tn), lambda l, j: (l, 0, j))],
        out_specs=pl.BlockSpec((None, b, tn), lambda l, j: (l, 0, j)),
        out_shape=jax.ShapeDtypeStruct((depth, b, e), F32),
        compiler_params=_cparams("parallel", "parallel"),
        name="ada_mod",
    )(c, w_ada, b_ada[:, None, :])


def _norm_proj_kernel(out_plan, x_ref, gain_ref, sh_ref, sc_ref, *refs):
    n_w = len(out_plan)
    w_refs, o_refs = refs[:n_w], refs[n_w:]
    h = _rms_mod(x_ref[...], gain_ref[...], sh_ref[...], sc_ref[...]).astype(BF16)
    k = 0
    for w_ref, (scale, dtypes) in zip(w_refs, out_plan):
        y = jnp.dot(h, w_ref[...], preferred_element_type=F32)
        if scale != 1.0:
            y = y * scale
        for dt, heads in dtypes:
            out = y.astype(dt)
            o_refs[k][...] = out if heads is None else out.reshape(o_refs[k].shape)
            k += 1


def _norm_proj(x, t_len, gain, shift, scale, pieces):
    n, d = x.shape
    tm = _row_tile(n, t_len, ROW_TILE)
    sh_arr, sh_spec = _mod_operand(shift, n, t_len, tm)
    sc_arr, sc_spec = _mod_operand(scale, n, t_len, tm)
    in_specs = [pl.BlockSpec((tm, d), lambda i: (i, 0)),
                pl.BlockSpec((1, d), lambda i: (0, 0)),
                sh_spec(0), sc_spec(0)]
    out_specs, out_shapes, plan, weights = [], [], [], []
    for w, s, dtypes in pieces:
        e = w.shape[1]
        in_specs.append(pl.BlockSpec((d, e), lambda i: (0, 0)))
        weights.append(w)
        plan.append((s, tuple(dtypes)))
        for dt, heads in dtypes:
            if heads is None:
                out_specs.append(pl.BlockSpec((tm, e), lambda i: (i, 0)))
                out_shapes.append(jax.ShapeDtypeStruct((n, e), dt))
            else:
                out_specs.append(pl.BlockSpec((tm, heads, e // heads), lambda i: (i, 0, 0)))
                out_shapes.append(jax.ShapeDtypeStruct((n, heads, e // heads), dt))
    return pl.pallas_call(
        functools.partial(_norm_proj_kernel, tuple(plan)),
        grid=(n // tm,),
        in_specs=in_specs, out_specs=out_specs, out_shape=out_shapes,
        compiler_params=_cparams("parallel"),
        name="norm_proj",
    )(x, gain[None, :], sh_arr, sc_arr, *weights)


def _shift_rows(x, s, fill):
    rows = lax.broadcasted_iota(I32, x.shape, 0)
    return jnp.where(rows >= s, pltpu.roll(x, s, axis=0), fill)


def _cumsum_rows(x):
    n = x.shape[0]
    s = 1
    while s < n:
        x = x + _shift_rows(x, s, 0.0)
        s *= 2
    return x


def _linear_scan_rows(a, b, h_init):
    n, w = a.shape
    a = a.reshape(n // SUBLANES, SUBLANES, w)
    b = b.reshape(n // SUBLANES, SUBLANES, w)
    in_group = lax.broadcasted_iota(I32, a.shape, 1)
    s = 1
    while s < SUBLANES:
        keep = in_group >= s
        b = a * jnp.where(keep, pltpu.roll(b, s, axis=1), 0.0) + b
        a = a * jnp.where(keep, pltpu.roll(a, s, axis=1), 1.0)
        s *= 2
    groups, carry = [], h_init
    for g in range(n // SUBLANES):
        h = a[g] * carry + b[g]
        groups.append(h)
        carry = h[SUBLANES - 1:SUBLANES, :]
    return jnp.concatenate(groups, axis=0)


def _gelu_tanh(x):
    return 0.5 * x * (1.0 + jnp.tanh(0.7978845608028654 * (x + 0.044715 * (x * x * x))))


def _ab_mixer_kernel(first_pos_zero, tt, chunk,
                     qk_ref, v_ref, g_ref, gk_ref, xr_ref, xg_ref, s0_ref, h0_ref, buf0_ref,
                     wgk_ref, bgk_ref, gnorm_ref, cw_ref, cb_ref, wa_ref, wi_ref, ba_ref, bi_ref, lam_ref,
                     mix_ref, s_out_ref, h_out_ref, buf_out_ref,
                     st_ref, hc_ref, xpad_ref):
    ti = pl.program_id(1)
    n_t = pl.num_programs(1)
    dqk = qk_ref.shape[1] // 2
    dv = v_ref.shape[1]
    hk = dqk // GLA_HEADS
    hv = dv // GLA_HEADS
    w_lru = xr_ref.shape[1]

    @pl.when(ti == 0)
    def _():
        st_ref[...] = s0_ref[...]
        hc_ref[...] = h0_ref[...]
        xpad_ref[0:SUBLANES, :] = buf0_ref[...]

    xpad_ref[SUBLANES:SUBLANES + tt, :] = xr_ref[...]
    xc = cb_ref[...]
    for i in range(CONV_WIDTH):
        off = SUBLANES - (CONV_WIDTH - 1) + i
        xc = xc + cw_ref[i:i + 1, :] * xpad_ref[off:off + tt, :]
    tail = xpad_ref[tt:tt + SUBLANES, :]
    xpad_ref[0:SUBLANES, :] = tail
    xc_b = xc.astype(BF16)
    half = w_lru // 2
    r_lin = jnp.concatenate([jnp.dot(xc_b[:, j * half:(j + 1) * half], wa_ref[j], preferred_element_type=F32)
                             for j in range(2)], axis=-1)
    i_lin = jnp.concatenate([jnp.dot(xc_b[:, j * half:(j + 1) * half], wi_ref[j], preferred_element_type=F32)
                             for j in range(2)], axis=-1)
    r = _sigmoid(r_lin + ba_ref[...])
    i_g = _sigmoid(i_lin + bi_ref[...])
    log_at = (-LRU_C) * r * _softplus(-lam_ref[...])
    a = jnp.exp(log_at)
    om = 1.0 - a * a
    mult = jnp.where(om > 0.0, om * lax.rsqrt(om), 0.0)
    if first_pos_zero:
        rows = lax.broadcasted_iota(I32, mult.shape, 0)
        mult = jnp.where((rows == 0) & (ti == 0), 1.0, mult)
    bterm = mult * (i_g * xc)
    hs = _linear_scan_rows(a, bterm, hc_ref[...])
    hc_ref[...] = hs[tt - 1:tt, :]
    y_b = hs * _gelu_tanh(xg_ref[...])
    mix_ref[:, dv:] = y_b.astype(BF16)

    lane_k = lax.broadcasted_iota(I32, (1, dqk), 1)
    st_rows = lax.broadcasted_iota(I32, (dv, dqk), 0)
    st_cols = lax.broadcasted_iota(I32, (dv, dqk), 1)
    st_mask = functools.reduce(
        jnp.logical_or,
        [(st_rows >= h * hv) & (st_rows < (h + 1) * hv) & (st_cols >= h * hk) & (st_cols < (h + 1) * hk)
         for h in range(GLA_HEADS)])
    crow = lax.broadcasted_iota(I32, (chunk, chunk), 0)
    ccol = lax.broadcasted_iota(I32, (chunk, chunk), 1)
    causal = ccol <= crow
    mid = chunk // 2 - 1
    for c in range(tt // chunk):
        rs = slice(c * chunk, (c + 1) * chunk)
        q = qk_ref[rs, 0:dqk] * (hk ** -0.5)
        k = qk_ref[rs, dqk:2 * dqk]
        v_b = v_ref[rs, :].astype(BF16)
        u = jnp.dot(gk_ref[rs, :].astype(BF16), wgk_ref[...], preferred_element_type=F32) + bgk_ref[...]
        b = _cumsum_rows(_log_sigmoid(u) * (1.0 / GLA_GATE_TAU))
        b_mid = b[mid:mid + 1, :]
        b_last = b[chunk - 1:chunk, :]
        q_e = q * jnp.exp(b - b_mid)
        k_e = (k * jnp.exp(b_mid - b)).astype(BF16)
        k_tail = (k * jnp.exp(b_last - b)).astype(BF16)
        q_dec = (q_e * jnp.exp(b_mid)).astype(BF16)
        q_e = q_e.astype(BF16)
        st = st_ref[...]
        o = lax.dot_general(q_dec, st.astype(BF16), (((1,), (1,)), ((), ())), preferred_element_type=F32)
        o_intra = []
        for h in range(GLA_HEADS):
            q_h = jnp.where((lane_k >= h * hk) & (lane_k < (h + 1) * hk), q_e, jnp.zeros_like(q_e))
            att = lax.dot_general(q_h, k_e, (((1,), (1,)), ((), ())), preferred_element_type=F32)
            att = jnp.where(causal, att, 0.0).astype(BF16)
            o_intra.append(jnp.dot(att, v_b[:, h * hv:(h + 1) * hv], preferred_element_type=F32))
        o = o + jnp.concatenate(o_intra, axis=-1)
        kv = lax.dot_general(v_b, k_tail, (((0,), (0,)), ((), ())), preferred_element_type=F32)
        st_ref[...] = st * jnp.exp(b_last) + jnp.where(st_mask, kv, 0.0)
        g = g_ref[rs, :]
        gate = g * _sigmoid(g)
        y_a = []
        for h in range(GLA_HEADS):
            o_h = o[:, h * hv:(h + 1) * hv]
            ms = jnp.mean(o_h * o_h, axis=-1, keepdims=True)
            y_a.append(o_h * lax.rsqrt(ms + EPS) * gnorm_ref[...] * gate[:, h * hv:(h + 1) * hv])
        mix_ref[rs, 0:dv] = jnp.concatenate(y_a, axis=-1).astype(BF16)

    @pl.when(ti == n_t - 1)
    def _():
        s_out_ref[...] = st_ref[...]
        h_out_ref[...] = hc_ref[...]
        buf_out_ref[...] = xpad_ref[0:SUBLANES, :]


def _ab_mixer(proj, bsz, t_len, first_pos_zero, s_gla, h_lru, conv_buf, w_gk2, b_gk, gla_norm, conv_w, conv_b,
              lru_wa, lru_ba, lru_wi, lru_bi, lru_lam):
    qk, v, g, gk, xr, xg = proj
    n = qk.shape[0]
    dqk, dv, w_lru = qk.shape[1] // 2, v.shape[1], xr.shape[1]
    hk, hv = dqk // GLA_HEADS, dv // GLA_HEADS
    tt = MIX_TILE if t_len % MIX_TILE == 0 else t_len
    chunk = GLA_CHUNK if tt % GLA_CHUNK == 0 else tt
    n_t = t_len // tt

    eye = jnp.eye(GLA_HEADS, dtype=F32)
    st0 = jnp.einsum('bhkv,hg->bhvgk', s_gla.astype(F32), eye).reshape(bsz, dv, dqk)
    buf0 = jnp.pad(conv_buf.astype(F32), ((0, 0), (SUBLANES - (CONV_WIDTH - 1), 0), (0, 0)))
    wgk = jnp.pad(w_gk2, ((0, gk.shape[1] - w_gk2.shape[0]), (0, 0))).astype(BF16)
    nb = lru_wa.shape[0]
    half_blocks = nb // 2

    def block_diag(w):
        bd = w.shape[1]
        e = jnp.eye(half_blocks, dtype=w.dtype)
        w2 = w.reshape(2, half_blocks, bd, bd)
        return jnp.einsum('jnio,nm->jnimo', w2, e).reshape(2, half_blocks * bd, half_blocks * bd).astype(BF16)

    row = lambda z: z.astype(F32)[None, :]
    tok = lambda w: pl.BlockSpec((tt, w), lambda b, i: (b * n_t + i, 0))
    per_b = lambda r, c: pl.BlockSpec((None, r, c), lambda b, i: (b, 0, 0))
    full2 = lambda r, c: pl.BlockSpec((r, c), lambda b, i: (0, 0))
    full3 = lambda a, r, c: pl.BlockSpec((a, r, c), lambda b, i: (0, 0, 0))
    mix, st, h_new, buf = pl.pallas_call(
        functools.partial(_ab_mixer_kernel, first_pos_zero, tt, chunk),
        grid=(bsz, n_t),
        in_specs=[tok(2 * dqk), tok(dv), tok(dv), tok(gk.shape[1]), tok(w_lru), tok(w_lru),
                  per_b(dv, dqk), per_b(1, w_lru), per_b(SUBLANES, w_lru),
                  full2(gk.shape[1], dqk), full2(1, dqk), full2(1, hv), full2(CONV_WIDTH, w_lru), full2(1, w_lru),
                  full3(2, w_lru // 2, w_lru // 2), full3(2, w_lru // 2, w_lru // 2),
                  full2(1, w_lru), full2(1, w_lru), full2(1, w_lru)],
        out_specs=[tok(dv + w_lru), per_b(dv, dqk), per_b(1, w_lru), per_b(SUBLANES, w_lru)],
        out_shape=[jax.ShapeDtypeStruct((n, dv + w_lru), BF16),
                   jax.ShapeDtypeStruct((bsz, dv, dqk), F32),
                   jax.ShapeDtypeStruct((bsz, 1, w_lru), F32),
                   jax.ShapeDtypeStruct((bsz, SUBLANES, w_lru), F32)],
        scratch_shapes=[pltpu.VMEM((dv, dqk), F32), pltpu.VMEM((1, w_lru), F32),
                        pltpu.VMEM((tt + SUBLANES, w_lru), F32)],
        compiler_params=_cparams("parallel", "arbitrary"),
        name="ab_mixer",
    )(qk, v, g, gk, xr, xg, st0, h_lru.astype(F32)[:, None, :], buf0,
      wgk, row(b_gk), row(gla_norm), conv_w.astype(F32), row(conv_b),
      block_diag(lru_wa), block_diag(lru_wi), row(lru_ba), row(lru_bi), row(lru_lam))
    st5 = st.reshape(bsz, GLA_HEADS, hv, GLA_HEADS, hk)
    s_new = jnp.stack([st5[:, h, :, h, :] for h in range(GLA_HEADS)], axis=1).swapaxes(-1, -2)
    return mix, s_new, h_new[:, 0], buf[:, SUBLANES - (CONV_WIDTH - 1):]


SB_MASKED = -1e30
SB_DEAD = -152.0
LOG2_E = 1.4426950408889634


def _sb_logits(q_h, k_blk):
    return lax.dot_general(q_h, k_blk, (((1,), (1,)), ((), ())), preferred_element_type=F32)


def _sb_gates(z, mask):
    neg_abs = pltpu.bitcast(pltpu.bitcast(z, I32) | jnp.int32(-2 ** 31), F32)
    ls = jnp.minimum(z, 0.0) - jnp.log2(1.0 + jnp.exp2(neg_abs))
    lk = ls - z
    if mask is not None:
        lk = jnp.where(mask, lk, 0.0)
    return ls, lk.astype(BF16), jnp.sum(lk, axis=-1, keepdims=True)


def _sb_prefix(lk, upper, suffix, mask):
    pre = jnp.dot(lk, upper, preferred_element_type=F32) + suffix
    if mask is not None:
        pre = jnp.where(mask, pre, SB_MASKED)
    return pre


def _sb_values(ls, pre, v_blk):
    return jnp.dot(jnp.exp2(ls + pre).astype(BF16), v_blk, preferred_element_type=F32)


def _split_heads(q, dh):
    lane = lax.broadcasted_iota(I32, (1, q.shape[1]), 1)
    zero = jnp.zeros_like(q)
    return [jnp.where((lane >= h * dh) & (lane < (h + 1) * dh), q, zero) for h in range(q.shape[1] // dh)], lane


def _merge_heads(acc, lane, dh):
    out = acc[0]
    for h in range(1, len(acc)):
        out = jnp.where(lane >= h * dh, acc[h], out)
    return out


def _upper(n):
    r = lax.broadcasted_iota(I32, (n, n), 0)
    c = lax.broadcasted_iota(I32, (n, n), 1)
    return jnp.where(r > c, 1.0, 0.0).astype(BF16)


def _group_queries(q, lanes, dh):
    groups, lane = [], None
    for g in range(q.shape[1] // lanes):
        heads_g, lane = _split_heads(q[:, g * lanes:(g + 1) * lanes], dh)
        groups.append(heads_g)
    return groups, lane


def _sb_block(q_groups, lanes, k_rows, v_rows, upper, mask, acc_ref, suf_ref):
    per_group = len(q_groups[0])
    for g, heads_g in enumerate(q_groups):
        k_blk = k_rows[:, g * lanes:(g + 1) * lanes]
        v_blk = v_rows[:, g * lanes:(g + 1) * lanes]
        for h, q_h in enumerate(heads_g):
            i = g * per_group + h
            ls, lk, tot = _sb_gates(_sb_logits(q_h, k_blk), mask)
            acc_ref[i] += _sb_values(ls, _sb_prefix(lk, upper, suf_ref[i], mask), v_blk)
            suf_ref[i] += tot


def _sb_live(suf_ref):
    return (jnp.max(suf_ref[...]) > SB_DEAD).astype(I32)


def _sb_output(acc_ref, q_groups, lane, dh):
    per_group = len(q_groups[0])
    return jnp.concatenate([_merge_heads([acc_ref[g * per_group + h] for h in range(per_group)], lane, dh)
                            for g in range(len(q_groups))], axis=-1)


def _sb_prompt_kernel(blk, dh, lanes, q_ref, k_ref, v_ref, o_ref, acc_ref, suf_ref):
    qi = pl.program_id(2)
    n_blk = qi + 1
    q_groups, lane = _group_queries(q_ref[...], lanes, dh)
    upper = _upper(blk)
    diag = lax.broadcasted_iota(I32, (blk, blk), 1) < lax.broadcasted_iota(I32, (blk, blk), 0)

    def block(s, mask):
        rows = pl.ds(pl.multiple_of((qi - s) * blk, blk), blk)
        _sb_block(q_groups, lanes, k_ref[rows, :], v_ref[rows, :], upper, mask, acc_ref, suf_ref)

    acc_ref[...] = jnp.zeros_like(acc_ref)
    suf_ref[...] = jnp.zeros_like(suf_ref)
    block(0, diag)

    def more(carry):
        s, go = carry
        return (s < n_blk) & (go > 0)

    def step(carry):
        s, _ = carry
        block(s, None)
        return s + 1, _sb_live(suf_ref)

    lax.while_loop(more, step, (jnp.int32(1), _sb_live(suf_ref)))
    o_ref[...] = _sb_output(acc_ref, q_groups, lane, dh).astype(o_ref.dtype)


def _sb_prompt(q, k, v, bsz, t_len, dh):
    n, width = q.shape
    blk = SB_BLOCK
    nq = t_len // blk
    step_w = SB_PROMPT_WIDTH
    lanes = SB_GROUP_LANES
    heads = step_w // dh
    return pl.pallas_call(
        functools.partial(_sb_prompt_kernel, blk, dh, lanes),
        grid=(bsz, width // step_w, nq),
        in_specs=[pl.BlockSpec((blk, step_w), lambda b, g, i: (b * nq + i, g)),
                  pl.BlockSpec((t_len, step_w), lambda b, g, i: (b, g)),
                  pl.BlockSpec((t_len, step_w), lambda b, g, i: (b, g))],
        out_specs=pl.BlockSpec((blk, step_w), lambda b, g, i: (b * nq + i, g)),
        out_shape=jax.ShapeDtypeStruct((n, width), BF16),
        scratch_shapes=[pltpu.VMEM((heads, blk, lanes), F32), pltpu.VMEM((heads, blk, 1), F32)],
        compiler_params=_cparams("parallel", "parallel", "arbitrary"),
        name="sb_prompt",
    )(q, k, v)


def _sb_sample_kernel(t_new, past_len, blk, dh, lanes, q_ref, k_ref, v_ref, pk_ref, pv_ref, o_ref, acc_ref, suf_ref):
    width = q_ref.shape[1]
    q_groups, lane = _group_queries(q_ref[...], lanes, dh)
    diag = lax.broadcasted_iota(I32, (t_new, t_new), 1) < lax.broadcasted_iota(I32, (t_new, t_new), 0)
    acc_ref[...] = jnp.zeros_like(acc_ref)
    suf_ref[...] = jnp.zeros_like(suf_ref)
    _sb_block(q_groups, lanes, k_ref[...], v_ref[...], _upper(t_new), diag, acc_ref, suf_ref)
    upper = _upper(blk)

    def more(carry):
        j, go = carry
        return (j >= 0) & (go > 0)

    def step(carry):
        j, _ = carry
        rows = pl.ds(pl.multiple_of(j * blk, blk), blk)
        _sb_block(q_groups, lanes, pk_ref[rows].reshape(blk, width).astype(BF16),
                  pv_ref[rows].reshape(blk, width).astype(BF16), upper, None, acc_ref, suf_ref)
        return j - 1, _sb_live(suf_ref)

    lax.while_loop(more, step, (jnp.int32(past_len // blk - 1), _sb_live(suf_ref)))
    o_ref[...] = _sb_output(acc_ref, q_groups, lane, dh).astype(o_ref.dtype)


def _sb_sample(q, k, v, past_k, past_v, bsz, t_len):
    n, width = q.shape
    _, past_len, heads, dh = past_k.shape
    blk = SB_BLOCK if past_len % SB_BLOCK == 0 else past_len
    lanes = SB_GROUP_LANES
    new = pl.BlockSpec((t_len, width), lambda b: (b, 0))
    past = pl.BlockSpec((None, past_len, heads, dh), lambda b: (b, 0, 0, 0))
    return pl.pallas_call(
        functools.partial(_sb_sample_kernel, t_len, past_len, blk, dh, lanes),
        grid=(bsz,),
        in_specs=[new, new, new, past, past],
        out_specs=new,
        out_shape=jax.ShapeDtypeStruct((n, width), BF16),
        scratch_shapes=[pltpu.VMEM((heads, t_len, lanes), F32), pltpu.VMEM((heads, t_len, 1), F32)],
        compiler_params=_cparams("parallel"),
        name="sb_sample",
    )(q, k, v, past_k, past_v)


def _first_argmax(vals):
    best_v, best_i = vals[0], jnp.zeros(vals[0].shape, I32)
    for i in range(1, len(vals)):
        better = vals[i] > best_v
        best_v = jnp.where(better, vals[i], best_v)
        best_i = jnp.where(better, i, best_i)
    return best_v, best_i


def _route_rows(logits_t, bias_col):
    s = _sigmoid(logits_t)
    sel = s + bias_col
    epg = EXPERTS_PER_GROUP
    scores = []
    for gi in range(N_GROUPS):
        rows = [sel[gi * epg + i:gi * epg + i + 1, :] for i in range(epg)]
        pair_sums = [rows[i] + rows[j] for i in range(epg) for j in range(i + 1, epg)]
        scores.append(functools.reduce(jnp.maximum, pair_sums))
    _, best = _first_argmax(scores)

    def in_best(mat, i):
        out = mat[i:i + 1, :]
        for gi in range(1, N_GROUPS):
            out = jnp.where(best == gi, mat[gi * epg + i:gi * epg + i + 1, :], out)
        return out

    sel_g = [in_best(sel, i) for i in range(epg)]
    s_g = [in_best(s, i) for i in range(epg)]
    _, i1 = _first_argmax(sel_g)
    _, i2 = _first_argmax([jnp.where(i1 == i, -jnp.inf, sel_g[i]) for i in range(epg)])
    lo = jnp.minimum(i1, i2)
    hi = jnp.maximum(i1, i2)
    pick = lambda idx: functools.reduce(lambda acc, i: jnp.where(idx == i, s_g[i], acc), range(1, epg), s_g[0])
    w_lo, w_hi = pick(lo), pick(hi)
    tot = w_lo + w_hi
    pair = jnp.where(lo == 0, 0, jnp.where(lo == 1, epg - 1, 2 * epg - 3)) + (hi - lo - 1)
    return best * N_PAIRS + pair, w_lo / tot, w_hi / tot


def _proj_route_kernel(a_ref, w_ref, x_ref, g1_ref, gain_ref, sh_ref, sc_ref, wr_hi_ref, wr_lo_ref, br_ref,
                       xn_ref, h2_ref, ri_ref):
    y = jnp.dot(a_ref[...], w_ref[...], preferred_element_type=F32)
    xn = x_ref[...] + g1_ref[...] * y
    xn_ref[...] = xn
    d = xn.shape[1]
    h2 = _rms_mod(xn, gain_ref[...], sh_ref[...], sc_ref[...])
    h2_ref[:, 0:d] = h2
    h_hi = h2.astype(BF16)
    h_lo = (h2 - h_hi.astype(F32)).astype(BF16)
    logits = (jnp.dot(h_hi, wr_hi_ref[...], preferred_element_type=F32)
              + jnp.dot(h_lo, wr_hi_ref[...], preferred_element_type=F32)
              + jnp.dot(h_hi, wr_lo_ref[...], preferred_element_type=F32))
    n_e = br_ref.shape[0]
    logits_t = logits.T[0:n_e, :]
    cls, w_lo, w_hi = _route_rows(logits_t, br_ref[...])
    tm = logits_t.shape[1]
    h2_ref[:, d:] = jnp.concatenate([w_lo, w_hi, jnp.zeros((LANES - 2, tm), F32)], axis=0).T
    ri_ref[...] = jnp.concatenate([cls, jnp.zeros((SUBLANES - 1, tm), I32)], axis=0)


def _proj_route(a, w_out, x, t_len, g1, gain, shift, scale, w_router, b_router):
    n, d = x.shape
    k = a.shape[1]
    n_e = w_router.shape[1]
    tm = _row_tile(n, t_len, ROW_TILE)
    g1_arr, g1_spec = _mod_operand(g1, n, t_len, tm)
    sh_arr, sh_spec = _mod_operand(shift, n, t_len, tm)
    sc_arr, sc_spec = _mod_operand(scale, n, t_len, tm)
    wr = jnp.pad(w_router.astype(F32), ((0, 0), (0, LANES - n_e)))
    wr_hi = wr.astype(BF16)
    wr_lo = (wr - wr_hi.astype(F32)).astype(BF16)
    tokd = pl.BlockSpec((tm, d), lambda i: (i, 0))
    const = lambda r, c: pl.BlockSpec((r, c), lambda i: (0, 0))
    route = pl.BlockSpec((SUBLANES, tm), lambda i: (0, i))
    return pl.pallas_call(
        _proj_route_kernel,
        grid=(n // tm,),
        in_specs=[pl.BlockSpec((tm, k), lambda i: (i, 0)), const(k, d), tokd, g1_spec(0), const(1, d),
                  sh_spec(0), sc_spec(0), const(d, LANES), const(d, LANES), const(n_e, 1)],
        out_specs=[tokd, pl.BlockSpec((tm, d + LANES), lambda i: (i, 0)), route],
        out_shape=[jax.ShapeDtypeStruct((n, d), F32), jax.ShapeDtypeStruct((n, d + LANES), F32),
                   jax.ShapeDtypeStruct((SUBLANES, n), I32)],
        compiler_params=_cparams("parallel"),
        name="proj_route",
    )(a, w_out, x, g1_arr, gain[None, :], sh_arr, sc_arr, wr_hi, wr_lo, b_router.astype(F32)[:, None])


def _gather_rows(src_hbm, idx_ref, base, n_rows, dst, sem):
    last = idx_ref.shape[0] - 1

    for r in range(n_rows):
        row = idx_ref[jnp.minimum(base + r, last)]
        pltpu.make_async_copy(src_hbm.at[pl.ds(row, 1)], dst.at[pl.ds(r, 1)], sem).start()


def _wait_rows(src_hbm, n_rows, dst, sem):
    pltpu.make_async_copy(src_hbm.at[pl.ds(0, n_rows)], dst, sem).wait()


def _moe_kernel(bm, d, order_ref, base_ref, valid_ref, e_lo_ref, e_hi_ref, used_ref,
                h_hbm, w1a_ref, w3a_ref, w2a_ref, w1b_ref, w3b_ref, w2b_ref,
                y_ref, buf0_ref, buf1_ref, sem_ref):
    i = pl.program_id(0)
    n_used = used_ref[0]

    @pl.when((i == 0) & (n_used > 0))
    def _():
        _gather_rows(h_hbm, order_ref, base_ref[0], bm, buf0_ref, sem_ref.at[0])

    def block(cur, cur_sem, nxt, nxt_sem):
        _wait_rows(h_hbm, bm, cur, cur_sem)
        xb = cur[:, 0:d].astype(BF16)
        live = lax.broadcasted_iota(I32, (bm, LANES), 0) < valid_ref[i]
        wts = jnp.where(live, cur[:, d:], 0.0)
        _gather_rows(h_hbm, order_ref, base_ref[i + 1], bm, nxt, nxt_sem)

        def expert(w1_ref, w3_ref, w2_ref):
            u = jnp.dot(xb, w1_ref[...], preferred_element_type=F32)
            t = jnp.dot(xb, w3_ref[...], preferred_element_type=F32)
            mid = (u * _sigmoid(u) * t).astype(BF16)
            return jnp.dot(mid, w2_ref[...], preferred_element_type=F32)

        y_ref[...] = (expert(w1a_ref, w3a_ref, w2a_ref) * wts[:, 0:1]
                      + expert(w1b_ref, w3b_ref, w2b_ref) * wts[:, 1:2])

        @pl.when(i == n_used - 1)
        def _():
            _wait_rows(h_hbm, bm, nxt, nxt_sem)

    @pl.when((i < n_used) & (i % 2 == 0))
    def _():
        block(buf0_ref, sem_ref.at[0], buf1_ref, sem_ref.at[1])

    @pl.when((i < n_used) & (i % 2 == 1))
    def _():
        block(buf1_ref, sem_ref.at[1], buf0_ref, sem_ref.at[0])

    @pl.when(i >= n_used)
    def _():
        y_ref[...] = jnp.zeros_like(y_ref)


def _moe_sorted(h2w, order, blk_base, blk_valid, blk_lo, blk_hi, n_used, w1, w3, w2):
    n, dw = h2w.shape
    d = dw - LANES
    n_blocks = blk_valid.shape[0]
    bm = MOE_BLOCK
    de = w1.shape[2]

    def wspec(which, r, c):
        def index(i, order, base, valid, lo, hi, used):
            blk = jnp.minimum(i, jnp.maximum(used[0] - 1, 0))
            return ((lo, hi)[which][blk], 0, 0)
        return pl.BlockSpec((None, r, c), index)

    return pl.pallas_call(
        functools.partial(_moe_kernel, bm, d),
        grid_spec=pltpu.PrefetchScalarGridSpec(
            num_scalar_prefetch=6,
            grid=(n_blocks,),
            in_specs=[pl.BlockSpec(memory_space=pl.ANY),
                      wspec(0, d, de), wspec(0, d, de), wspec(0, de, d),
                      wspec(1, d, de), wspec(1, d, de), wspec(1, de, d)],
            out_specs=pl.BlockSpec((bm, d), lambda i, *_: (i, 0)),
            scratch_shapes=[pltpu.VMEM((bm, dw), F32), pltpu.VMEM((bm, dw), F32), pltpu.SemaphoreType.DMA((2,))]),
        out_shape=jax.ShapeDtypeStruct((n_blocks * bm, d), F32),
        compiler_params=_cparams("arbitrary"),
        name="moe_sorted",
    )(order, blk_base, blk_valid, blk_lo, blk_hi, n_used, h2w, w1, w3, w2, w1, w3, w2)


def _unsort_kernel(tm, final_norm, pos_ref, y_hbm, x_ref, g2_ref, gain_ref, o_ref, buf0_ref, buf1_ref, sem_ref):
    i = pl.program_id(0)
    n_i = pl.num_programs(0)

    @pl.when(i == 0)
    def _():
        _gather_rows(y_hbm, pos_ref, 0, tm, buf0_ref, sem_ref.at[0])

    def step(cur, cur_sem, nxt, nxt_sem):
        @pl.when(i + 1 < n_i)
        def _():
            _gather_rows(y_hbm, pos_ref, (i + 1) * tm, tm, nxt, nxt_sem)

        _wait_rows(y_hbm, tm, cur, cur_sem)
        x = x_ref[...] + g2_ref[...] * cur[...]
        if final_norm:
            ms = jnp.mean(x * x, axis=-1, keepdims=True)
            x = x * lax.rsqrt(ms + EPS) * gain_ref[...]
        o_ref[...] = x

    @pl.when(i % 2 == 0)
    def _():
        step(buf0_ref, sem_ref.at[0], buf1_ref, sem_ref.at[1])

    @pl.when(i % 2 == 1)
    def _():
        step(buf1_ref, sem_ref.at[1], buf0_ref, sem_ref.at[0])


def _unsort_resid(y_sorted, pos, x, t_len, g2, final_gain):
    n, d = x.shape
    tm = _row_tile(n, t_len, GATHER_TILE)
    g2_arr, g2_spec = _mod_operand(g2, n, t_len, tm)
    final_norm = final_gain is not None
    gain = (final_gain if final_norm else jnp.ones((d,), F32)).astype(F32)[None, :]
    return pl.pallas_call(
        functools.partial(_unsort_kernel, tm, final_norm),
        grid_spec=pltpu.PrefetchScalarGridSpec(
            num_scalar_prefetch=1,
            grid=(n // tm,),
            in_specs=[pl.BlockSpec(memory_space=pl.ANY),
                      pl.BlockSpec((tm, d), lambda i, pos: (i, 0)),
                      g2_spec(0),
                      pl.BlockSpec((1, d), lambda i, pos: (0, 0))],
            out_specs=pl.BlockSpec((tm, d), lambda i, pos: (i, 0)),
            scratch_shapes=[pltpu.VMEM((tm, d), F32), pltpu.VMEM((tm, d), F32), pltpu.SemaphoreType.DMA((2,))]),
        out_shape=jax.ShapeDtypeStruct((n, d), F32),
        compiler_params=_cparams("arbitrary"),
        name="unsort_resid",
    )(pos, y_sorted, x, g2_arr, gain)


def _sort_plan(cls):
    n = cls.shape[0]
    bm = MOE_BLOCK
    n_blocks = -(-n // bm) + N_CLASSES
    classes = jnp.arange(N_CLASSES, dtype=I32)
    cls_sorted, order = lax.sort((cls, jnp.arange(n, dtype=I32)), num_keys=1, is_stable=True)
    counts = jnp.sum((cls[:, None] == classes[None, :]).astype(I32), axis=0)
    padded = (counts + bm - 1) // bm * bm
    pad_end = jnp.cumsum(padded)
    pad_start = pad_end - padded
    start = jnp.cumsum(counts) - counts
    shift = jnp.sum(jnp.where(cls_sorted[:, None] == classes[None, :], (pad_start - start)[None, :], 0), axis=1)
    dest = jnp.arange(n, dtype=I32) + shift
    _, pos = lax.sort((order, dest), num_keys=1)
    blk_first = jnp.arange(n_blocks + 1, dtype=I32) * bm
    blk_cls = jnp.minimum(jnp.sum((pad_end[None, :] <= blk_first[:, None]).astype(I32), axis=1), N_CLASSES - 1)
    onehot = blk_cls[:, None] == classes[None, :]
    pick = lambda table: jnp.sum(jnp.where(onehot, table[None, :], 0), axis=1)
    into = blk_first - pick(pad_start)
    blk_base = jnp.clip(pick(start) + into, 0, n)
    blk_valid = jnp.clip(pick(counts) - into, 0, bm)[:n_blocks]
    pair_lo = jnp.array([i for i in range(EXPERTS_PER_GROUP) for j in range(i + 1, EXPERTS_PER_GROUP)], I32)
    pair_hi = jnp.array([j for i in range(EXPERTS_PER_GROUP) for j in range(i + 1, EXPERTS_PER_GROUP)], I32)
    group = classes // N_PAIRS
    blk_lo = pick(group * EXPERTS_PER_GROUP + pair_lo[classes % N_PAIRS])
    blk_hi = pick(group * EXPERTS_PER_GROUP + pair_hi[classes % N_PAIRS])
    n_used = (pad_end[-1] // bm).astype(I32)[None]
    return order, pos, blk_base.astype(I32), blk_valid.astype(I32), blk_lo, blk_hi, n_used


def _moe_layer(h2w, route_i, x, t_len, g2, w1, w3, w2, final_gain):
    order, pos, blk_base, blk_valid, blk_lo, blk_hi, n_used = _sort_plan(route_i[0])
    y_sorted = _moe_sorted(h2w, order, blk_base, blk_valid, blk_lo, blk_hi, n_used, w1, w3, w2)
    return _unsort_resid(y_sorted, pos, x, t_len, g2, final_gain)


def _trunk(x3, c, p, past):
    bsz, t_len, d = x3.shape
    n = bsz * t_len
    x = x3.reshape(n, d)
    depth = p['w_ada'].shape[0]
    mod = _ada(c.astype(F32), p['w_ada'], p['b_ada']).reshape(depth, bsz, N_MOD, d)
    gla_s, lru_s, conv_s, ks, vs = [], [], [], [], []
    for l in range(depth):
        sh1, sc1, g1, sh2, sc2, g2 = [mod[l, :, i] for i in range(N_MOD)]
        j = l // 2
        if l % 2 == 0:
            w_in = p['w_ab_in'][j]
            dqk2 = 2 * p['w_gk2'].shape[2]
            dv = GLA_HEADS * p['gla_norm'].shape[1]
            rank = p['w_gk2'].shape[1]
            w_lru = p['lru_lam'].shape[1]
            cuts = [0, dqk2, dqk2 + dv, dqk2 + 2 * dv, dqk2 + 2 * dv + rank,
                    dqk2 + 2 * dv + rank + w_lru, dqk2 + 2 * dv + rank + 2 * w_lru]
            cols = [w_in[:, cuts[i]:cuts[i + 1]] for i in range(6)]
            cols[3] = jnp.pad(cols[3], ((0, 0), (0, LANES - rank)))
            proj = _norm_proj(x, t_len, p['norm_mix'][l], sh1, sc1,
                              [(w.astype(BF16), 1.0, [(F32, None)]) for w in cols])
            if past is None:
                s0 = jnp.zeros((bsz, GLA_HEADS, dqk2 // 2 // GLA_HEADS, dv // GLA_HEADS), F32)
                h0 = jnp.zeros((bsz, w_lru), F32)
                buf = jnp.zeros((bsz, CONV_WIDTH - 1, w_lru), F32)
            else:
                s0, h0, buf = past['state_gla'][j], past['state_lru'][j], past['state_conv'][j]
            mix, sg, sl, sc = _ab_mixer(proj, bsz, t_len, past is None, s0, h0, buf, p['w_gk2'][j], p['b_gk'][j],
                                        p['gla_norm'][j], p['conv_w'][j], p['conv_b'][j], p['lru_wa'][j],
                                        p['lru_ba'][j], p['lru_wi'][j], p['lru_bi'][j], p['lru_lam'][j])
            gla_s.append(sg)
            lru_s.append(sl)
            conv_s.append(sc)
            w_out = p['w_ab_out'][j]
        else:
            w_qkv = p['w_sb_qkv'][j]
            width = w_qkv.shape[1] // 3
            heads = past['cache_k'].shape[3] if past is not None else p['sb_heads']
            dh = width // heads
            pieces = [(w_qkv[:, 0:width].astype(BF16), LOG2_E * dh ** -0.5, [(BF16, None)]),
                      (w_qkv[:, width:2 * width].astype(BF16), 1.0, [(F32, heads), (BF16, None)]),
                      (w_qkv[:, 2 * width:].astype(BF16), 1.0, [(F32, heads), (BF16, None)])]
            q_b, k_f, k_b, v_f, v_b = _norm_proj(x, t_len, p['norm_mix'][l], sh1, sc1, pieces)
            if past is None:
                mix = _sb_prompt(q_b, k_b, v_b, bsz, t_len, dh)
            else:
                mix = _sb_sample(q_b, k_b, v_b, past['cache_k'][j].astype(F32), past['cache_v'][j].astype(F32),
                                 bsz, t_len)
            ks.append(k_f.reshape(bsz, t_len, heads, dh))
            vs.append(v_f.reshape(bsz, t_len, heads, dh))
            w_out = p['w_sb_out'][j]
        xn, h2w, route_i = _proj_route(mix, w_out.astype(BF16), x, t_len, g1, p['norm_ffn'][l], sh2, sc2,
                                       p['w_router'], p['b_router'])
        x = _moe_layer(h2w, route_i, xn, t_len, g2,
                       p['w_e1'][l].astype(BF16), p['w_e3'][l].astype(BF16), p['w_e2'][l].astype(BF16),
                       p['norm_out'] if l == depth - 1 else None)
    stack = lambda xs: xs[0][None] if len(xs) == 1 else jnp.stack(xs)
    return x.reshape(bsz, t_len, d), (stack(gla_s), stack(lru_s), stack(conv_s), stack(ks), stack(vs))


def kernel(x_prompt, x_sample, state_gla, state_lru, state_conv, cache_k, cache_v, c_prompt, c_sample,
           w_ada, b_ada, norm_mix, norm_ffn, norm_out, w_ab_in, w_gk2, b_gk, gla_norm, conv_w, conv_b,
           lru_wa, lru_ba, lru_wi, lru_bi, lru_lam, w_ab_out, w_sb_qkv, w_sb_out, w_router, b_router,
           w_e1, w_e3, w_e2):
    p = dict(w_ada=w_ada, b_ada=b_ada, norm_mix=norm_mix, norm_ffn=norm_ffn, norm_out=norm_out,
             w_ab_in=w_ab_in, w_gk2=w_gk2, b_gk=b_gk, gla_norm=gla_norm, conv_w=conv_w, conv_b=conv_b,
             lru_wa=lru_wa, lru_ba=lru_ba, lru_wi=lru_wi, lru_bi=lru_bi, lru_lam=lru_lam, w_ab_out=w_ab_out,
             w_sb_qkv=w_sb_qkv, w_sb_out=w_sb_out, w_router=w_router, b_router=b_router,
             w_e1=w_e1, w_e3=w_e3, w_e2=w_e2, sb_heads=cache_k.shape[3])
    past = dict(state_gla=state_gla, state_lru=state_lru, state_conv=state_conv, cache_k=cache_k, cache_v=cache_v)
    y_prompt, (p_gla, p_lru, p_conv, p_k, p_v) = _trunk(x_prompt, c_prompt, p, None)
    y_sample, (s_gla, s_lru, s_conv, s_k, s_v) = _trunk(x_sample, c_sample, p, past)
    return (y_prompt, y_sample, p_gla, p_lru, p_conv, p_k, p_v, s_gla, s_lru, s_conv, s_k, s_v)
```

```python
import functools

import jax
import jax.numpy as jnp
from jax import lax
from jax.experimental import pallas as pl
from jax.experimental.pallas import tpu as pltpu

F32 = jnp.float32
BF16 = jnp.bfloat16
I32 = jnp.int32

EPS = 1e-6
N_MOD = 6
GLA_HEADS = 4
GLA_GATE_TAU = 16.0
LRU_C = 8.0
CONV_WIDTH = 4
N_GROUPS = 4
EXPERTS_PER_GROUP = 4
N_PAIRS = 6
N_CLASSES = N_GROUPS * N_PAIRS

LANES = 128
SUBLANES = 8
VMEM_LIMIT = 56 * 1024 * 1024

ROW_TILE = 512
MIX_TILE = 256
GLA_CHUNK = 64
SB_BLOCK = 256
SB_GROUP_LANES = 256
SB_PROMPT_WIDTH = 512
MOE_BLOCK = 256
GATHER_TILE = 256


def _cparams(*sem):
    return pltpu.CompilerParams(dimension_semantics=sem, vmem_limit_bytes=VMEM_LIMIT)


def _log_sigmoid(z):
    return jnp.minimum(z, 0.0) - jnp.log(1.0 + jnp.exp(-jnp.abs(z)))


def _softplus(z):
    return jnp.maximum(z, 0.0) + jnp.log1p(jnp.exp(-jnp.abs(z)))


def _sigmoid(z):
    return 1.0 / (1.0 + jnp.exp(-z))


def _rms_mod(x, gain, shift, scale):
    ms = jnp.mean(x * x, axis=-1, keepdims=True)
    y = x * lax.rsqrt(ms + EPS) * gain
    return y * (1.0 + scale) + shift


def _row_tile(n_rows, t_len, target):
    if t_len % target == 0:
        return target
    return n_rows


def _mod_operand(mod, n_rows, t_len, tile):
    d = mod.shape[-1]
    if t_len % tile == 0:
        per_seq = t_len // tile
        return mod[:, None, :], (lambda nidx: pl.BlockSpec((None, 1, d), lambda *i: (i[nidx] // per_seq, 0, 0)))
    rows = jnp.repeat(mod, t_len, axis=0)
    return rows, (lambda nidx: pl.BlockSpec((tile, d), lambda *i: (i[nidx], 0)))


def _ada_kernel(c_ref, w_ref, b_ref, o_ref):
    c = c_ref[...]
    cond = c * _sigmoid(c)
    o_ref[...] = jnp.dot(cond.astype(BF16), w_ref[...].astype(BF16), preferred_element_type=F32) + b_ref[...]


def _ada(c, w_ada, b_ada):
    depth, d, e = w_ada.shape
    b = c.shape[0]
    tn = d
    return pl.pallas_call(
        _ada_kernel,
        grid=(depth, e // tn),
        in_specs=[pl.BlockSpec((b, d), lambda l, j: (0, 0)),
                  pl.BlockSpec((None, d, tn), lambda l, j: (l, 0, j)),
                  pl.BlockSpec((None, 1, tn), lambda l, j: (l, 0, j))],
        out_specs=pl.BlockSpec((None, b, tn), lambda l, j: (l, 0, j)),
        out_shape=jax.ShapeDtypeStruct((depth, b, e), F32),
        compiler_params=_cparams("parallel", "parallel"),
        name="ada_mod",
    )(c, w_ada, b_ada[:, None, :])


def _norm_proj_kernel(out_plan, x_ref, gain_ref, sh_ref, sc_ref, *refs):
    n_w = len(out_plan)
    w_refs, o_refs = refs[:n_w], refs[n_w:]
    h = _rms_mod(x_ref[...], gain_ref[...], sh_ref[...], sc_ref[...]).astype(BF16)
    k = 0
    for w_ref, (scale, dtypes) in zip(w_refs, out_plan):
        y = jnp.dot(h, w_ref[...], preferred_element_type=F32)
        if scale != 1.0:
            y = y * scale
        for dt, heads in dtypes:
            out = y.astype(dt)
            o_refs[k][...] = out if heads is None else out.T.reshape(o_refs[k].shape)
            k += 1


def _norm_proj(x, t_len, gain, shift, scale, pieces):
    n, d = x.shape
    tm = _row_tile(n, t_len, ROW_TILE)
    sh_arr, sh_spec = _mod_operand(shift, n, t_len, tm)
    sc_arr, sc_spec = _mod_operand(scale, n, t_len, tm)
    in_specs = [pl.BlockSpec((tm, d), lambda i: (i, 0)),
                pl.BlockSpec((1, d), lambda i: (0, 0)),
                sh_spec(0), sc_spec(0)]
    out_specs, out_shapes, plan, weights = [], [], [], []
    for w, s, dtypes in pieces:
        e = w.shape[1]
        in_specs.append(pl.BlockSpec((d, e), lambda i: (0, 0)))
        weights.append(w)
        plan.append((s, tuple(dtypes)))
        for dt, heads in dtypes:
            if heads is None:
                out_specs.append(pl.BlockSpec((tm, e), lambda i: (i, 0)))
                out_shapes.append(jax.ShapeDtypeStruct((n, e), dt))
            else:
                per_seq = t_len // tm
                out_specs.append(pl.BlockSpec((None, heads, e // heads, tm),
                                              lambda i: (i // per_seq, 0, 0, i % per_seq)))
                out_shapes.append(jax.ShapeDtypeStruct((n // t_len, heads, e // heads, t_len), dt))
    return pl.pallas_call(
        functools.partial(_norm_proj_kernel, tuple(plan)),
        grid=(n // tm,),
        in_specs=in_specs, out_specs=out_specs, out_shape=out_shapes,
        compiler_params=_cparams("parallel"),
        name="norm_proj",
    )(x, gain[None, :], sh_arr, sc_arr, *weights)


def _shift_rows(x, s, fill):
    rows = lax.broadcasted_iota(I32, x.shape, 0)
    return jnp.where(rows >= s, pltpu.roll(x, s, axis=0), fill)


def _cumsum_rows(x):
    n = x.shape[0]
    s = 1
    while s < n:
        x = x + _shift_rows(x, s, 0.0)
        s *= 2
    return x


def _linear_scan_rows(a, b, h_init):
    n, w = a.shape
    a = a.reshape(n // SUBLANES, SUBLANES, w)
    b = b.reshape(n // SUBLANES, SUBLANES, w)
    in_group = lax.broadcasted_iota(I32, a.shape, 1)
    s = 1
    while s < SUBLANES:
        keep = in_group >= s
        b = a * jnp.where(keep, pltpu.roll(b, s, axis=1), 0.0) + b
        a = a * jnp.where(keep, pltpu.roll(a, s, axis=1), 1.0)
        s *= 2
    groups, carry = [], h_init
    for g in range(n // SUBLANES):
        h = a[g] * carry + b[g]
        groups.append(h)
        carry = h[SUBLANES - 1:SUBLANES, :]
    return jnp.concatenate(groups, axis=0)


def _gelu_tanh(x):
    return 0.5 * x * (1.0 + jnp.tanh(0.7978845608028654 * (x + 0.044715 * (x * x * x))))


def _ab_mixer_kernel(first_pos_zero, tt, chunk,
                     qk_ref, v_ref, g_ref, gk_ref, xr_ref, xg_ref, s0_ref, h0_ref, buf0_ref,
                     wgk_ref, bgk_ref, gnorm_ref, cw_ref, cb_ref, wa_ref, wi_ref, ba_ref, bi_ref, lam_ref,
                     mix_ref, s_out_ref, h_out_ref, buf_out_ref,
                     st_ref, hc_ref, xpad_ref):
    ti = pl.program_id(1)
    n_t = pl.num_programs(1)
    dqk = qk_ref.shape[1] // 2
    dv = v_ref.shape[1]
    hk = dqk // GLA_HEADS
    hv = dv // GLA_HEADS
    w_lru = xr_ref.shape[1]

    @pl.when(ti == 0)
    def _():
        st_ref[...] = s0_ref[...]
        hc_ref[...] = h0_ref[...]
        xpad_ref[0:SUBLANES, :] = buf0_ref[...]

    xpad_ref[SUBLANES:SUBLANES + tt, :] = xr_ref[...]
    xc = cb_ref[...]
    for i in range(CONV_WIDTH):
        off = SUBLANES - (CONV_WIDTH - 1) + i
        xc = xc + cw_ref[i:i + 1, :] * xpad_ref[off:off + tt, :]
    tail = xpad_ref[tt:tt + SUBLANES, :]
    xpad_ref[0:SUBLANES, :] = tail
    xc_b = xc.astype(BF16)
    half = w_lru // 2
    r_lin = jnp.concatenate([jnp.dot(xc_b[:, j * half:(j + 1) * half], wa_ref[j], preferred_element_type=F32)
                             for j in range(2)], axis=-1)
    i_lin = jnp.concatenate([jnp.dot(xc_b[:, j * half:(j + 1) * half], wi_ref[j], preferred_element_type=F32)
                             for j in range(2)], axis=-1)
    r = _sigmoid(r_lin + ba_ref[...])
    i_g = _sigmoid(i_lin + bi_ref[...])
    log_at = (-LRU_C) * r * _softplus(-lam_ref[...])
    a = jnp.exp(log_at)
    om = 1.0 - a * a
    mult = jnp.where(om > 0.0, om * lax.rsqrt(om), 0.0)
    if first_pos_zero:
        rows = lax.broadcasted_iota(I32, mult.shape, 0)
        mult = jnp.where((rows == 0) & (ti == 0), 1.0, mult)
    bterm = mult * (i_g * xc)
    hs = _linear_scan_rows(a, bterm, hc_ref[...])
    hc_ref[...] = hs[tt - 1:tt, :]
    y_b = hs * _gelu_tanh(xg_ref[...])
    mix_ref[:, dv:] = y_b.astype(BF16)

    lane_k = lax.broadcasted_iota(I32, (1, dqk), 1)
    st_rows = lax.broadcasted_iota(I32, (dv, dqk), 0)
    st_cols = lax.broadcasted_iota(I32, (dv, dqk), 1)
    st_mask = functools.reduce(
        jnp.logical_or,
        [(st_rows >= h * hv) & (st_rows < (h + 1) * hv) & (st_cols >= h * hk) & (st_cols < (h + 1) * hk)
         for h in range(GLA_HEADS)])
    crow = lax.broadcasted_iota(I32, (chunk, chunk), 0)
    ccol = lax.broadcasted_iota(I32, (chunk, chunk), 1)
    causal = ccol <= crow
    mid = chunk // 2 - 1
    for c in range(tt // chunk):
        rs = slice(c * chunk, (c + 1) * chunk)
        q = qk_ref[rs, 0:dqk] * (hk ** -0.5)
        k = qk_ref[rs, dqk:2 * dqk]
        v_b = v_ref[rs, :].astype(BF16)
        u = jnp.dot(gk_ref[rs, :].astype(BF16), wgk_ref[...], preferred_element_type=F32) + bgk_ref[...]
        b = _cumsum_rows(_log_sigmoid(u) * (1.0 / GLA_GATE_TAU))
        b_mid = b[mid:mid + 1, :]
        b_last = b[chunk - 1:chunk, :]
        q_e = q * jnp.exp(b - b_mid)
        k_e = (k * jnp.exp(b_mid - b)).astype(BF16)
        k_tail = (k * jnp.exp(b_last - b)).astype(BF16)
        q_dec = (q_e * jnp.exp(b_mid)).astype(BF16)
        q_e = q_e.astype(BF16)
        st = st_ref[...]
        o = lax.dot_general(q_dec, st.astype(BF16), (((1,), (1,)), ((), ())), preferred_element_type=F32)
        o_intra = []
        for h in range(GLA_HEADS):
            q_h = jnp.where((lane_k >= h * hk) & (lane_k < (h + 1) * hk), q_e, jnp.zeros_like(q_e))
            att = lax.dot_general(q_h, k_e, (((1,), (1,)), ((), ())), preferred_element_type=F32)
            att = jnp.where(causal, att, 0.0).astype(BF16)
            o_intra.append(jnp.dot(att, v_b[:, h * hv:(h + 1) * hv], preferred_element_type=F32))
        o = o + jnp.concatenate(o_intra, axis=-1)
        kv = lax.dot_general(v_b, k_tail, (((0,), (0,)), ((), ())), preferred_element_type=F32)
        st_ref[...] = st * jnp.exp(b_last) + jnp.where(st_mask, kv, 0.0)
        g = g_ref[rs, :]
        gate = g * _sigmoid(g)
        y_a = []
        for h in range(GLA_HEADS):
            o_h = o[:, h * hv:(h + 1) * hv]
            ms = jnp.mean(o_h * o_h, axis=-1, keepdims=True)
            y_a.append(o_h * lax.rsqrt(ms + EPS) * gnorm_ref[...] * gate[:, h * hv:(h + 1) * hv])
        mix_ref[rs, 0:dv] = jnp.concatenate(y_a, axis=-1).astype(BF16)

    @pl.when(ti == n_t - 1)
    def _():
        s_out_ref[...] = st_ref[...]
        h_out_ref[...] = hc_ref[...]
        buf_out_ref[...] = xpad_ref[0:SUBLANES, :]


def _ab_mixer(proj, bsz, t_len, first_pos_zero, s_gla, h_lru, conv_buf, w_gk2, b_gk, gla_norm, conv_w, conv_b,
              lru_wa, lru_ba, lru_wi, lru_bi, lru_lam):
    qk, v, g, gk, xr, xg = proj
    n = qk.shape[0]
    dqk, dv, w_lru = qk.shape[1] // 2, v.shape[1], xr.shape[1]
    hk, hv = dqk // GLA_HEADS, dv // GLA_HEADS
    tt = MIX_TILE if t_len % MIX_TILE == 0 else t_len
    chunk = GLA_CHUNK if tt % GLA_CHUNK == 0 else tt
    n_t = t_len // tt

    eye = jnp.eye(GLA_HEADS, dtype=F32)
    st0 = jnp.einsum('bhkv,hg->bhvgk', s_gla.astype(F32), eye).reshape(bsz, dv, dqk)
    buf0 = jnp.pad(conv_buf.astype(F32), ((0, 0), (SUBLANES - (CONV_WIDTH - 1), 0), (0, 0)))
    wgk = jnp.pad(w_gk2, ((0, gk.shape[1] - w_gk2.shape[0]), (0, 0))).astype(BF16)
    nb = lru_wa.shape[0]
    half_blocks = nb // 2

    def block_diag(w):
        bd = w.shape[1]
        e = jnp.eye(half_blocks, dtype=w.dtype)
        w2 = w.reshape(2, half_blocks, bd, bd)
        return jnp.einsum('jnio,nm->jnimo', w2, e).reshape(2, half_blocks * bd, half_blocks * bd).astype(BF16)

    row = lambda z: z.astype(F32)[None, :]
    tok = lambda w: pl.BlockSpec((tt, w), lambda b, i: (b * n_t + i, 0))
    per_b = lambda r, c: pl.BlockSpec((None, r, c), lambda b, i: (b, 0, 0))
    full2 = lambda r, c: pl.BlockSpec((r, c), lambda b, i: (0, 0))
    full3 = lambda a, r, c: pl.BlockSpec((a, r, c), lambda b, i: (0, 0, 0))
    mix, st, h_new, buf = pl.pallas_call(
        functools.partial(_ab_mixer_kernel, first_pos_zero, tt, chunk),
        grid=(bsz, n_t),
        in_specs=[tok(2 * dqk), tok(dv), tok(dv), tok(gk.shape[1]), tok(w_lru), tok(w_lru),
                  per_b(dv, dqk), per_b(1, w_lru), per_b(SUBLANES, w_lru),
                  full2(gk.shape[1], dqk), full2(1, dqk), full2(1, hv), full2(CONV_WIDTH, w_lru), full2(1, w_lru),
                  full3(2, w_lru // 2, w_lru // 2), full3(2, w_lru // 2, w_lru // 2),
                  full2(1, w_lru), full2(1, w_lru), full2(1, w_lru)],
        out_specs=[tok(dv + w_lru), per_b(dv, dqk), per_b(1, w_lru), per_b(SUBLANES, w_lru)],
        out_shape=[jax.ShapeDtypeStruct((n, dv + w_lru), BF16),
                   jax.ShapeDtypeStruct((bsz, dv, dqk), F32),
                   jax.ShapeDtypeStruct((bsz, 1, w_lru), F32),
                   jax.ShapeDtypeStruct((bsz, SUBLANES, w_lru), F32)],
        scratch_shapes=[pltpu.VMEM((dv, dqk), F32), pltpu.VMEM((1, w_lru), F32),
                        pltpu.VMEM((tt + SUBLANES, w_lru), F32)],
        compiler_params=_cparams("parallel", "arbitrary"),
        name="ab_mixer",
    )(qk, v, g, gk, xr, xg, st0, h_lru.astype(F32)[:, None, :], buf0,
      wgk, row(b_gk), row(gla_norm), conv_w.astype(F32), row(conv_b),
      block_diag(lru_wa), block_diag(lru_wi), row(lru_ba), row(lru_bi), row(lru_lam))
    st5 = st.reshape(bsz, GLA_HEADS, hv, GLA_HEADS, hk)
    s_new = jnp.stack([st5[:, h, :, h, :] for h in range(GLA_HEADS)], axis=1).swapaxes(-1, -2)
    return mix, s_new, h_new[:, 0], buf[:, SUBLANES - (CONV_WIDTH - 1):]


SB_MASKED = -1e30
SB_DEAD = -152.0
LOG2_E = 1.4426950408889634


def _sb_logits(q_h, k_blk, keys_on_lanes):
    if keys_on_lanes:
        return jnp.dot(q_h, k_blk, preferred_element_type=F32)
    return lax.dot_general(q_h, k_blk, (((1,), (1,)), ((), ())), preferred_element_type=F32)


def _sb_gates(z, mask):
    neg_abs = pltpu.bitcast(pltpu.bitcast(z, I32) | jnp.int32(-2 ** 31), F32)
    ls = jnp.minimum(z, 0.0) - jnp.log2(1.0 + jnp.exp2(neg_abs))
    lk = ls - z
    if mask is not None:
        lk = jnp.where(mask, lk, 0.0)
    return ls, lk.astype(BF16), jnp.sum(lk, axis=-1, keepdims=True)


def _sb_prefix(lk, upper, suffix, mask):
    pre = jnp.dot(lk, upper, preferred_element_type=F32) + suffix
    if mask is not None:
        pre = jnp.where(mask, pre, SB_MASKED)
    return pre


def _sb_values(ls, pre, v_blk, keys_on_lanes):
    w = jnp.exp2(ls + pre).astype(BF16)
    if keys_on_lanes:
        return lax.dot_general(w, v_blk, (((1,), (1,)), ((), ())), preferred_element_type=F32)
    return jnp.dot(w, v_blk, preferred_element_type=F32)


def _split_heads(q, dh):
    lane = lax.broadcasted_iota(I32, (1, q.shape[1]), 1)
    zero = jnp.zeros_like(q)
    return [jnp.where((lane >= h * dh) & (lane < (h + 1) * dh), q, zero) for h in range(q.shape[1] // dh)], lane


def _merge_heads(acc, lane, dh):
    out = acc[0]
    for h in range(1, len(acc)):
        out = jnp.where(lane >= h * dh, acc[h], out)
    return out


def _upper(n):
    r = lax.broadcasted_iota(I32, (n, n), 0)
    c = lax.broadcasted_iota(I32, (n, n), 1)
    return jnp.where(r > c, 1.0, 0.0).astype(BF16)


def _group_queries(q, lanes, dh):
    groups, lane = [], None
    for g in range(q.shape[1] // lanes):
        heads_g, lane = _split_heads(q[:, g * lanes:(g + 1) * lanes], dh)
        groups.append(heads_g)
    return groups, lane


def _sb_block(q_groups, lanes, k_rows, v_rows, upper, mask, acc_ref, suf_ref, keys_on_lanes=False):
    per_group = len(q_groups[0])
    for g, heads_g in enumerate(q_groups):
        cols = slice(g * lanes, (g + 1) * lanes)
        k_blk = k_rows[cols, :] if keys_on_lanes else k_rows[:, cols]
        v_blk = v_rows[cols, :] if keys_on_lanes else v_rows[:, cols]
        for h, q_h in enumerate(heads_g):
            i = g * per_group + h
            ls, lk, tot = _sb_gates(_sb_logits(q_h, k_blk, keys_on_lanes), mask)
            acc_ref[i] += _sb_values(ls, _sb_prefix(lk, upper, suf_ref[i], mask), v_blk, keys_on_lanes)
            suf_ref[i] += tot


def _sb_live(suf_ref):
    return (jnp.max(suf_ref[...]) > SB_DEAD).astype(I32)


def _sb_output(acc_ref, q_groups, lane, dh):
    per_group = len(q_groups[0])
    return jnp.concatenate([_merge_heads([acc_ref[g * per_group + h] for h in range(per_group)], lane, dh)
                            for g in range(len(q_groups))], axis=-1)


def _sb_prompt_kernel(blk, dh, lanes, q_ref, k_ref, v_ref, o_ref, acc_ref, suf_ref):
    qi = pl.program_id(2)
    n_blk = qi + 1
    q_groups, lane = _group_queries(q_ref[...], lanes, dh)
    upper = _upper(blk)
    diag = lax.broadcasted_iota(I32, (blk, blk), 1) < lax.broadcasted_iota(I32, (blk, blk), 0)

    def block(s, mask):
        rows = pl.ds(pl.multiple_of((qi - s) * blk, blk), blk)
        _sb_block(q_groups, lanes, k_ref[rows, :], v_ref[rows, :], upper, mask, acc_ref, suf_ref)

    acc_ref[...] = jnp.zeros_like(acc_ref)
    suf_ref[...] = jnp.zeros_like(suf_ref)
    block(0, diag)

    def more(carry):
        s, go = carry
        return (s < n_blk) & (go > 0)

    def step(carry):
        s, _ = carry
        block(s, None)
        return s + 1, _sb_live(suf_ref)

    lax.while_loop(more, step, (jnp.int32(1), _sb_live(suf_ref)))
    o_ref[...] = _sb_output(acc_ref, q_groups, lane, dh).astype(o_ref.dtype)


def _sb_prompt(q, k, v, bsz, t_len, dh):
    n, width = q.shape
    blk = SB_BLOCK
    nq = t_len // blk
    step_w = SB_PROMPT_WIDTH
    lanes = SB_GROUP_LANES
    heads = step_w // dh
    return pl.pallas_call(
        functools.partial(_sb_prompt_kernel, blk, dh, lanes),
        grid=(bsz, width // step_w, nq),
        in_specs=[pl.BlockSpec((blk, step_w), lambda b, g, i: (b * nq + i, g)),
                  pl.BlockSpec((t_len, step_w), lambda b, g, i: (b, g)),
                  pl.BlockSpec((t_len, step_w), lambda b, g, i: (b, g))],
        out_specs=pl.BlockSpec((blk, step_w), lambda b, g, i: (b * nq + i, g)),
        out_shape=jax.ShapeDtypeStruct((n, width), BF16),
        scratch_shapes=[pltpu.VMEM((heads, blk, lanes), F32), pltpu.VMEM((heads, blk, 1), F32)],
        compiler_params=_cparams("parallel", "parallel", "arbitrary"),
        name="sb_prompt",
    )(q, k, v)


def _sb_sample_kernel(t_new, past_len, blk, dh, lanes, q_ref, k_ref, v_ref, pk_ref, pv_ref, o_ref, acc_ref, suf_ref):
    width = q_ref.shape[1]
    q_groups, lane = _group_queries(q_ref[...], lanes, dh)
    diag = lax.broadcasted_iota(I32, (t_new, t_new), 1) < lax.broadcasted_iota(I32, (t_new, t_new), 0)
    acc_ref[...] = jnp.zeros_like(acc_ref)
    suf_ref[...] = jnp.zeros_like(suf_ref)
    _sb_block(q_groups, lanes, k_ref[...], v_ref[...], _upper(t_new), diag, acc_ref, suf_ref)
    upper = _upper(blk)

    def more(carry):
        j, go = carry
        return (j >= 0) & (go > 0)

    def step(carry):
        j, _ = carry
        keys = pl.ds(pl.multiple_of(j * blk, blk), blk)
        _sb_block(q_groups, lanes, pk_ref[:, :, keys].reshape(width, blk).astype(BF16),
                  pv_ref[:, :, keys].reshape(width, blk).astype(BF16), upper, None, acc_ref, suf_ref,
                  keys_on_lanes=True)
        return j - 1, _sb_live(suf_ref)

    lax.while_loop(more, step, (jnp.int32(past_len // blk - 1), _sb_live(suf_ref)))
    o_ref[...] = _sb_output(acc_ref, q_groups, lane, dh).astype(o_ref.dtype)


def _sb_sample(q, k, v, past_k, past_v, bsz, t_len):
    n, width = q.shape
    _, heads, dh, past_len = past_k.shape
    blk = SB_BLOCK if past_len % SB_BLOCK == 0 else past_len
    lanes = SB_GROUP_LANES
    new = pl.BlockSpec((t_len, width), lambda b: (b, 0))
    past = pl.BlockSpec((None, heads, dh, past_len), lambda b: (b, 0, 0, 0))
    return pl.pallas_call(
        functools.partial(_sb_sample_kernel, t_len, past_len, blk, dh, lanes),
        grid=(bsz,),
        in_specs=[new, new, new, past, past],
        out_specs=new,
        out_shape=jax.ShapeDtypeStruct((n, width), BF16),
        scratch_shapes=[pltpu.VMEM((heads, t_len, lanes), F32), pltpu.VMEM((heads, t_len, 1), F32)],
        compiler_params=_cparams("parallel"),
        name="sb_sample",
    )(q, k, v, past_k, past_v)


def _first_argmax(vals):
    best_v, best_i = vals[0], jnp.zeros(vals[0].shape, I32)
    for i in range(1, len(vals)):
        better = vals[i] > best_v
        best_v = jnp.where(better, vals[i], best_v)
        best_i = jnp.where(better, i, best_i)
    return best_v, best_i


def _route_rows(logits_t, bias_col):
    s = _sigmoid(logits_t)
    sel = s + bias_col
    epg = EXPERTS_PER_GROUP
    scores = []
    for gi in range(N_GROUPS):
        rows = [sel[gi * epg + i:gi * epg + i + 1, :] for i in range(epg)]
        pair_sums = [rows[i] + rows[j] for i in range(epg) for j in range(i + 1, epg)]
        scores.append(functools.reduce(jnp.maximum, pair_sums))
    _, best = _first_argmax(scores)

    def in_best(mat, i):
        out = mat[i:i + 1, :]
        for gi in range(1, N_GROUPS):
            out = jnp.where(best == gi, mat[gi * epg + i:gi * epg + i + 1, :], out)
        return out

    sel_g = [in_best(sel, i) for i in range(epg)]
    s_g = [in_best(s, i) for i in range(epg)]
    _, i1 = _first_argmax(sel_g)
    _, i2 = _first_argmax([jnp.where(i1 == i, -jnp.inf, sel_g[i]) for i in range(epg)])
    lo = jnp.minimum(i1, i2)
    hi = jnp.maximum(i1, i2)
    pick = lambda idx: functools.reduce(lambda acc, i: jnp.where(idx == i, s_g[i], acc), range(1, epg), s_g[0])
    w_lo, w_hi = pick(lo), pick(hi)
    tot = w_lo + w_hi
    pair = jnp.where(lo == 0, 0, jnp.where(lo == 1, epg - 1, 2 * epg - 3)) + (hi - lo - 1)
    return best * N_PAIRS + pair, w_lo / tot, w_hi / tot


def _proj_route_kernel(a_ref, w_ref, x_ref, g1_ref, gain_ref, sh_ref, sc_ref, wr_hi_ref, wr_lo_ref, br_ref,
                       xn_ref, h2_ref, ri_ref):
    y = jnp.dot(a_ref[...], w_ref[...], preferred_element_type=F32)
    xn = x_ref[...] + g1_ref[...] * y
    xn_ref[...] = xn
    d = xn.shape[1]
    h2 = _rms_mod(xn, gain_ref[...], sh_ref[...], sc_ref[...])
    h2_ref[:, 0:d] = h2
    h_hi = h2.astype(BF16)
    h_lo = (h2 - h_hi.astype(F32)).astype(BF16)
    logits = (jnp.dot(h_hi, wr_hi_ref[...], preferred_element_type=F32)
              + jnp.dot(h_lo, wr_hi_ref[...], preferred_element_type=F32)
              + jnp.dot(h_hi, wr_lo_ref[...], preferred_element_type=F32))
    n_e = br_ref.shape[0]
    logits_t = logits.T[0:n_e, :]
    cls, w_lo, w_hi = _route_rows(logits_t, br_ref[...])
    tm = logits_t.shape[1]
    h2_ref[:, d:] = jnp.concatenate([w_lo, w_hi, jnp.zeros((LANES - 2, tm), F32)], axis=0).T
    ri_ref[...] = jnp.concatenate([cls, jnp.zeros((SUBLANES - 1, tm), I32)], axis=0)


def _proj_route(a, w_out, x, t_len, g1, gain, shift, scale, w_router, b_router):
    n, d = x.shape
    k = a.shape[1]
    n_e = w_router.shape[1]
    tm = _row_tile(n, t_len, ROW_TILE)
    g1_arr, g1_spec = _mod_operand(g1, n, t_len, tm)
    sh_arr, sh_spec = _mod_operand(shift, n, t_len, tm)
    sc_arr, sc_spec = _mod_operand(scale, n, t_len, tm)
    wr = jnp.pad(w_router.astype(F32), ((0, 0), (0, LANES - n_e)))
    wr_hi = wr.astype(BF16)
    wr_lo = (wr - wr_hi.astype(F32)).astype(BF16)
    tokd = pl.BlockSpec((tm, d), lambda i: (i, 0))
    const = lambda r, c: pl.BlockSpec((r, c), lambda i: (0, 0))
    route = pl.BlockSpec((SUBLANES, tm), lambda i: (0, i))
    return pl.pallas_call(
        _proj_route_kernel,
        grid=(n // tm,),
        in_specs=[pl.BlockSpec((tm, k), lambda i: (i, 0)), const(k, d), tokd, g1_spec(0), const(1, d),
                  sh_spec(0), sc_spec(0), const(d, LANES), const(d, LANES), const(n_e, 1)],
        out_specs=[tokd, pl.BlockSpec((tm, d + LANES), lambda i: (i, 0)), route],
        out_shape=[jax.ShapeDtypeStruct((n, d), F32), jax.ShapeDtypeStruct((n, d + LANES), F32),
                   jax.ShapeDtypeStruct((SUBLANES, n), I32)],
        compiler_params=_cparams("parallel"),
        name="proj_route",
    )(a, w_out, x, g1_arr, gain[None, :], sh_arr, sc_arr, wr_hi, wr_lo, b_router.astype(F32)[:, None])


def _gather_rows(src_hbm, idx_ref, base, n_rows, dst, sem):
    last = idx_ref.shape[0] - 1

    for r in range(n_rows):
        row = idx_ref[jnp.minimum(base + r, last)]
        pltpu.make_async_copy(src_hbm.at[pl.ds(row, 1)], dst.at[pl.ds(r, 1)], sem).start()


def _wait_rows(src_hbm, n_rows, dst, sem):
    pltpu.make_async_copy(src_hbm.at[pl.ds(0, n_rows)], dst, sem).wait()


def _moe_kernel(bm, d, order_ref, base_ref, valid_ref, e_lo_ref, e_hi_ref, used_ref,
                h_hbm, w1a_ref, w3a_ref, w2a_ref, w1b_ref, w3b_ref, w2b_ref,
                y_ref, buf0_ref, buf1_ref, sem_ref):
    i = pl.program_id(0)
    n_used = used_ref[0]

    @pl.when((i == 0) & (n_used > 0))
    def _():
        _gather_rows(h_hbm, order_ref, base_ref[0], bm, buf0_ref, sem_ref.at[0])

    def block(cur, cur_sem, nxt, nxt_sem):
        _wait_rows(h_hbm, bm, cur, cur_sem)
        xb = cur[:, 0:d].astype(BF16)
        live = lax.broadcasted_iota(I32, (bm, LANES), 0) < valid_ref[i]
        wts = jnp.where(live, cur[:, d:], 0.0)
        _gather_rows(h_hbm, order_ref, base_ref[i + 1], bm, nxt, nxt_sem)

        def expert(w1_ref, w3_ref, w2_ref):
            u = jnp.dot(xb, w1_ref[...], preferred_element_type=F32)
            t = jnp.dot(xb, w3_ref[...], preferred_element_type=F32)
            mid = (u * _sigmoid(u) * t).astype(BF16)
            return jnp.dot(mid, w2_ref[...], preferred_element_type=F32)

        y_ref[...] = (expert(w1a_ref, w3a_ref, w2a_ref) * wts[:, 0:1]
                      + expert(w1b_ref, w3b_ref, w2b_ref) * wts[:, 1:2])

        @pl.when(i == n_used - 1)
        def _():
            _wait_rows(h_hbm, bm, nxt, nxt_sem)

    @pl.when((i < n_used) & (i % 2 == 0))
    def _():
        block(buf0_ref, sem_ref.at[0], buf1_ref, sem_ref.at[1])

    @pl.when((i < n_used) & (i % 2 == 1))
    def _():
        block(buf1_ref, sem_ref.at[1], buf0_ref, sem_ref.at[0])

    @pl.when(i >= n_used)
    def _():
        y_ref[...] = jnp.zeros_like(y_ref)


def _moe_sorted(h2w, order, blk_base, blk_valid, blk_lo, blk_hi, n_used, w1, w3, w2):
    n, dw = h2w.shape
    d = dw - LANES
    n_blocks = blk_valid.shape[0]
    bm = MOE_BLOCK
    de = w1.shape[2]

    def wspec(which, r, c):
        def index(i, order, base, valid, lo, hi, used):
            blk = jnp.minimum(i, jnp.maximum(used[0] - 1, 0))
            return ((lo, hi)[which][blk], 0, 0)
        return pl.BlockSpec((None, r, c), index)

    return pl.pallas_call(
        functools.partial(_moe_kernel, bm, d),
        grid_spec=pltpu.PrefetchScalarGridSpec(
            num_scalar_prefetch=6,
            grid=(n_blocks,),
            in_specs=[pl.BlockSpec(memory_space=pl.ANY),
                      wspec(0, d, de), wspec(0, d, de), wspec(0, de, d),
                      wspec(1, d, de), wspec(1, d, de), wspec(1, de, d)],
            out_specs=pl.BlockSpec((bm, d), lambda i, *_: (i, 0)),
            scratch_shapes=[pltpu.VMEM((bm, dw), F32), pltpu.VMEM((bm, dw), F32), pltpu.SemaphoreType.DMA((2,))]),
        out_shape=jax.ShapeDtypeStruct((n_blocks * bm, d), F32),
        compiler_params=_cparams("arbitrary"),
        name="moe_sorted",
    )(order, blk_base, blk_valid, blk_lo, blk_hi, n_used, h2w, w1, w3, w2, w1, w3, w2)


def _unsort_kernel(tm, final_norm, pos_ref, y_hbm, x_ref, g2_ref, gain_ref, o_ref, buf0_ref, buf1_ref, sem_ref):
    i = pl.program_id(0)
    n_i = pl.num_programs(0)

    @pl.when(i == 0)
    def _():
        _gather_rows(y_hbm, pos_ref, 0, tm, buf0_ref, sem_ref.at[0])

    def step(cur, cur_sem, nxt, nxt_sem):
        @pl.when(i + 1 < n_i)
        def _():
            _gather_rows(y_hbm, pos_ref, (i + 1) * tm, tm, nxt, nxt_sem)

        _wait_rows(y_hbm, tm, cur, cur_sem)
        x = x_ref[...] + g2_ref[...] * cur[...]
        if final_norm:
            ms = jnp.mean(x * x, axis=-1, keepdims=True)
            x = x * lax.rsqrt(ms + EPS) * gain_ref[...]
        o_ref[...] = x

    @pl.when(i % 2 == 0)
    def _():
        step(buf0_ref, sem_ref.at[0], buf1_ref, sem_ref.at[1])

    @pl.when(i % 2 == 1)
    def _():
        step(buf1_ref, sem_ref.at[1], buf0_ref, sem_ref.at[0])


def _unsort_resid(y_sorted, pos, x, t_len, g2, final_gain):
    n, d = x.shape
    tm = _row_tile(n, t_len, GATHER_TILE)
    g2_arr, g2_spec = _mod_operand(g2, n, t_len, tm)
    final_norm = final_gain is not None
    gain = (final_gain if final_norm else jnp.ones((d,), F32)).astype(F32)[None, :]
    return pl.pallas_call(
        functools.partial(_unsort_kernel, tm, final_norm),
        grid_spec=pltpu.PrefetchScalarGridSpec(
            num_scalar_prefetch=1,
            grid=(n // tm,),
            in_specs=[pl.BlockSpec(memory_space=pl.ANY),
                      pl.BlockSpec((tm, d), lambda i, pos: (i, 0)),
                      g2_spec(0),
                      pl.BlockSpec((1, d), lambda i, pos: (0, 0))],
            out_specs=pl.BlockSpec((tm, d), lambda i, pos: (i, 0)),
            scratch_shapes=[pltpu.VMEM((tm, d), F32), pltpu.VMEM((tm, d), F32), pltpu.SemaphoreType.DMA((2,))]),
        out_shape=jax.ShapeDtypeStruct((n, d), F32),
        compiler_params=_cparams("arbitrary"),
        name="unsort_resid",
    )(pos, y_sorted, x, g2_arr, gain)


def _sort_plan(cls):
    n = cls.shape[0]
    bm = MOE_BLOCK
    n_blocks = -(-n // bm) + N_CLASSES
    classes = jnp.arange(N_CLASSES, dtype=I32)
    cls_sorted, order = lax.sort((cls, jnp.arange(n, dtype=I32)), num_keys=1, is_stable=True)
    counts = jnp.sum((cls[:, None] == classes[None, :]).astype(I32), axis=0)
    padded = (counts + bm - 1) // bm * bm
    pad_end = jnp.cumsum(padded)
    pad_start = pad_end - padded
    start = jnp.cumsum(counts) - counts
    shift = jnp.sum(jnp.where(cls_sorted[:, None] == classes[None, :], (pad_start - start)[None, :], 0), axis=1)
    dest = jnp.arange(n, dtype=I32) + shift
    _, pos = lax.sort((order, dest), num_keys=1)
    blk_first = jnp.arange(n_blocks + 1, dtype=I32) * bm
    blk_cls = jnp.minimum(jnp.sum((pad_end[None, :] <= blk_first[:, None]).astype(I32), axis=1), N_CLASSES - 1)
    onehot = blk_cls[:, None] == classes[None, :]
    pick = lambda table: jnp.sum(jnp.where(onehot, table[None, :], 0), axis=1)
    into = blk_first - pick(pad_start)
    blk_base = jnp.clip(pick(start) + into, 0, n)
    blk_valid = jnp.clip(pick(counts) - into, 0, bm)[:n_blocks]
    pair_lo = jnp.array([i for i in range(EXPERTS_PER_GROUP) for j in range(i + 1, EXPERTS_PER_GROUP)], I32)
    pair_hi = jnp.array([j for i in range(EXPERTS_PER_GROUP) for j in range(i + 1, EXPERTS_PER_GROUP)], I32)
    group = classes // N_PAIRS
    blk_lo = pick(group * EXPERTS_PER_GROUP + pair_lo[classes % N_PAIRS])
    blk_hi = pick(group * EXPERTS_PER_GROUP + pair_hi[classes % N_PAIRS])
    n_used = (pad_end[-1] // bm).astype(I32)[None]
    return order, pos, blk_base.astype(I32), blk_valid.astype(I32), blk_lo, blk_hi, n_used


def _moe_layer(h2w, route_i, x, t_len, g2, w1, w3, w2, final_gain):
    order, pos, blk_base, blk_valid, blk_lo, blk_hi, n_used = _sort_plan(route_i[0])
    y_sorted = _moe_sorted(h2w, order, blk_base, blk_valid, blk_lo, blk_hi, n_used, w1, w3, w2)
    return _unsort_resid(y_sorted, pos, x, t_len, g2, final_gain)


def _trunk(x3, c, p, past):
    bsz, t_len, d = x3.shape
    n = bsz * t_len
    x = x3.reshape(n, d)
    depth = p['w_ada'].shape[0]
    mod = _ada(c.astype(F32), p['w_ada'], p['b_ada']).reshape(depth, bsz, N_MOD, d)
    gla_s, lru_s, conv_s, ks, vs = [], [], [], [], []
    for l in range(depth):
        sh1, sc1, g1, sh2, sc2, g2 = [mod[l, :, i] for i in range(N_MOD)]
        j = l // 2
        if l % 2 == 0:
            w_in = p['w_ab_in'][j]
            dqk2 = 2 * p['w_gk2'].shape[2]
            dv = GLA_HEADS * p['gla_norm'].shape[1]
            rank = p['w_gk2'].shape[1]
            w_lru = p['lru_lam'].shape[1]
            cuts = [0, dqk2, dqk2 + dv, dqk2 + 2 * dv, dqk2 + 2 * dv + rank,
                    dqk2 + 2 * dv + rank + w_lru, dqk2 + 2 * dv + rank + 2 * w_lru]
            cols = [w_in[:, cuts[i]:cuts[i + 1]] for i in range(6)]
            cols[3] = jnp.pad(cols[3], ((0, 0), (0, LANES - rank)))
            proj = _norm_proj(x, t_len, p['norm_mix'][l], sh1, sc1,
                              [(w.astype(BF16), 1.0, [(F32, None)]) for w in cols])
            if past is None:
                s0 = jnp.zeros((bsz, GLA_HEADS, dqk2 // 2 // GLA_HEADS, dv // GLA_HEADS), F32)
                h0 = jnp.zeros((bsz, w_lru), F32)
                buf = jnp.zeros((bsz, CONV_WIDTH - 1, w_lru), F32)
            else:
                s0, h0, buf = past['state_gla'][j], past['state_lru'][j], past['state_conv'][j]
            mix, sg, sl, sc = _ab_mixer(proj, bsz, t_len, past is None, s0, h0, buf, p['w_gk2'][j], p['b_gk'][j],
                                        p['gla_norm'][j], p['conv_w'][j], p['conv_b'][j], p['lru_wa'][j],
                                        p['lru_ba'][j], p['lru_wi'][j], p['lru_bi'][j], p['lru_lam'][j])
            gla_s.append(sg)
            lru_s.append(sl)
            conv_s.append(sc)
            w_out = p['w_ab_out'][j]
        else:
            w_qkv = p['w_sb_qkv'][j]
            width = w_qkv.shape[1] // 3
            heads = past['cache_k'].shape[3] if past is not None else p['sb_heads']
            dh = width // heads
            split = heads if t_len % ROW_TILE == 0 else None
            pieces = [(w_qkv[:, 0:width].astype(BF16), LOG2_E * dh ** -0.5, [(BF16, None)]),
                      (w_qkv[:, width:2 * width].astype(BF16), 1.0, [(F32, split), (BF16, None)]),
                      (w_qkv[:, 2 * width:].astype(BF16), 1.0, [(F32, split), (BF16, None)])]
            q_b, k_f, k_b, v_f, v_b = _norm_proj(x, t_len, p['norm_mix'][l], sh1, sc1, pieces)
            if past is None:
                mix = _sb_prompt(q_b, k_b, v_b, bsz, t_len, dh)
            else:
                time_minor = lambda c: c.astype(F32).transpose(0, 2, 3, 1)
                mix = _sb_sample(q_b, k_b, v_b, time_minor(past['cache_k'][j]), time_minor(past['cache_v'][j]),
                                 bsz, t_len)
            rows = lambda z: z.transpose(0, 3, 1, 2) if split else z.reshape(bsz, t_len, heads, dh)
            ks.append(rows(k_f))
            vs.append(rows(v_f))
            w_out = p['w_sb_out'][j]
        xn, h2w, route_i = _proj_route(mix, w_out.astype(BF16), x, t_len, g1, p['norm_ffn'][l], sh2, sc2,
                                       p['w_router'], p['b_router'])
        x = _moe_layer(h2w, route_i, xn, t_len, g2,
                       p['w_e1'][l].astype(BF16), p['w_e3'][l].astype(BF16), p['w_e2'][l].astype(BF16),
                       p['norm_out'] if l == depth - 1 else None)
    stack = lambda xs: xs[0][None] if len(xs) == 1 else jnp.stack(xs)
    return x.reshape(bsz, t_len, d), (stack(gla_s), stack(lru_s), stack(conv_s), stack(ks), stack(vs))


def kernel(x_prompt, x_sample, state_gla, state_lru, state_conv, cache_k, cache_v, c_prompt, c_sample,
           w_ada, b_ada, norm_mix, norm_ffn, norm_out, w_ab_in, w_gk2, b_gk, gla_norm, conv_w, conv_b,
           lru_wa, lru_ba, lru_wi, lru_bi, lru_lam, w_ab_out, w_sb_qkv, w_sb_out, w_router, b_router,
           w_e1, w_e3, w_e2):
    p = dict(w_ada=w_ada, b_ada=b_ada, norm_mix=norm_mix, norm_ffn=norm_ffn, norm_out=norm_out,
             w_ab_in=w_ab_in, w_gk2=w_gk2, b_gk=b_gk, gla_norm=gla_norm, conv_w=conv_w, conv_b=conv_b,
             lru_wa=lru_wa, lru_ba=lru_ba, lru_wi=lru_wi, lru_bi=lru_bi, lru_lam=lru_lam, w_ab_out=w_ab_out,
             w_sb_qkv=w_sb_qkv, w_sb_out=w_sb_out, w_router=w_router, b_router=b_router,
             w_e1=w_e1, w_e3=w_e3, w_e2=w_e2, sb_heads=cache_k.shape[3])
    past = dict(state_gla=state_gla, state_lru=state_lru, state_conv=state_conv, cache_k=cache_k, cache_v=cache_v)
    y_prompt, (p_gla, p_lru, p_conv, p_k, p_v) = _trunk(x_prompt, c_prompt, p, None)
    y_sample, (s_gla, s_lru, s_conv, s_k, s_v) = _trunk(x_sample, c_sample, p, past)
    return (y_prompt, y_sample, p_gla, p_lru, p_conv, p_k, p_v, s_gla, s_lru, s_conv, s_k, s_v)
```

```python
import functools

import jax
import jax.numpy as jnp
from jax import lax
from jax.experimental import pallas as pl
from jax.experimental.pallas import tpu as pltpu

F32 = jnp.float32
BF16 = jnp.bfloat16
I32 = jnp.int32

EPS = 1e-6
N_MOD = 6
GLA_HEADS = 4
GLA_GATE_TAU = 16.0
LRU_C = 8.0
CONV_WIDTH = 4
N_GROUPS = 4
EXPERTS_PER_GROUP = 4
N_PAIRS = 6
N_CLASSES = N_GROUPS * N_PAIRS

LANES = 128
SUBLANES = 8
VMEM_LIMIT = 56 * 1024 * 1024

ROW_TILE = 512
MIX_TILE = 256
GLA_CHUNK = 64
SB_BLOCK = 256
SB_GROUP_LANES = 256
SB_PROMPT_WIDTH = 512
MOE_BLOCK = 256
MOE_BLOCK_SMALL = 64


def _moe_block(n_tokens):
    return MOE_BLOCK if n_tokens >= N_CLASSES * MOE_BLOCK else MOE_BLOCK_SMALL
GATHER_TILE = 256


def _cparams(*sem):
    return pltpu.CompilerParams(dimension_semantics=sem, vmem_limit_bytes=VMEM_LIMIT)


def _log_sigmoid(z):
    return jnp.minimum(z, 0.0) - jnp.log(1.0 + jnp.exp(-jnp.abs(z)))


def _softplus(z):
    return jnp.maximum(z, 0.0) + jnp.log1p(jnp.exp(-jnp.abs(z)))


def _sigmoid(z):
    return 1.0 / (1.0 + jnp.exp(-z))


def _rms_mod(x, gain, shift, scale):
    ms = jnp.mean(x * x, axis=-1, keepdims=True)
    y = x * lax.rsqrt(ms + EPS) * gain
    return y * (1.0 + scale) + shift


def _row_tile(n_rows, t_len, target):
    if t_len % target == 0:
        return target
    return n_rows


def _mod_operand(mod, n_rows, t_len, tile):
    d = mod.shape[-1]
    if t_len % tile == 0:
        per_seq = t_len // tile
        return mod[:, None, :], (lambda nidx: pl.BlockSpec((None, 1, d), lambda *i: (i[nidx] // per_seq, 0, 0)))
    rows = jnp.repeat(mod, t_len, axis=0)
    return rows, (lambda nidx: pl.BlockSpec((tile, d), lambda *i: (i[nidx], 0)))


def _ada_kernel(c_ref, w_ref, b_ref, o_ref):
    c = c_ref[...]
    cond = c * _sigmoid(c)
    o_ref[...] = jnp.dot(cond.astype(BF16), w_ref[...].astype(BF16), preferred_element_type=F32) + b_ref[...]


def _ada(c, w_ada, b_ada):
    depth, d, e = w_ada.shape
    b = c.shape[0]
    tn = d
    return pl.pallas_call(
        _ada_kernel,
        grid=(depth, e // tn),
        in_specs=[pl.BlockSpec((b, d), lambda l, j: (0, 0)),
                  pl.BlockSpec((None, d, tn), lambda l, j: (l, 0, j)),
                  pl.BlockSpec((None, 1, tn), lambda l, j: (l, 0, j))],
        out_specs=pl.BlockSpec((None, b, tn), lambda l, j: (l, 0, j)),
        out_shape=jax.ShapeDtypeStruct((depth, b, e), F32),
        compiler_params=_cparams("parallel", "parallel"),
        name="ada_mod",
    )(c, w_ada, b_ada[:, None, :])


def _norm_proj_kernel(out_plan, x_ref, gain_ref, sh_ref, sc_ref, *refs):
    n_w = len(out_plan)
    w_refs, o_refs = refs[:n_w], refs[n_w:]
    h = _rms_mod(x_ref[...], gain_ref[...], sh_ref[...], sc_ref[...]).astype(BF16)
    k = 0
    for w_ref, (scale, dtypes) in zip(w_refs, out_plan):
        y = jnp.dot(h, w_ref[...], preferred_element_type=F32)
        if scale != 1.0:
            y = y * scale
        for dt, heads in dtypes:
            out = y.astype(dt)
            o_refs[k][...] = out if heads is None else out.T.reshape(o_refs[k].shape)
            k += 1


def _norm_proj(x, t_len, gain, shift, scale, pieces):
    n, d = x.shape
    tm = _row_tile(n, t_len, ROW_TILE)
    sh_arr, sh_spec = _mod_operand(shift, n, t_len, tm)
    sc_arr, sc_spec = _mod_operand(scale, n, t_len, tm)
    in_specs = [pl.BlockSpec((tm, d), lambda i: (i, 0)),
                pl.BlockSpec((1, d), lambda i: (0, 0)),
                sh_spec(0), sc_spec(0)]
    out_specs, out_shapes, plan, weights = [], [], [], []
    for w, s, dtypes in pieces:
        e = w.shape[1]
        in_specs.append(pl.BlockSpec((d, e), lambda i: (0, 0)))
        weights.append(w)
        plan.append((s, tuple(dtypes)))
        for dt, heads in dtypes:
            if heads is None:
                out_specs.append(pl.BlockSpec((tm, e), lambda i: (i, 0)))
                out_shapes.append(jax.ShapeDtypeStruct((n, e), dt))
            else:
                per_seq = t_len // tm
                out_specs.append(pl.BlockSpec((None, heads, e // heads, tm),
                                              lambda i: (i // per_seq, 0, 0, i % per_seq)))
                out_shapes.append(jax.ShapeDtypeStruct((n // t_len, heads, e // heads, t_len), dt))
    return pl.pallas_call(
        functools.partial(_norm_proj_kernel, tuple(plan)),
        grid=(n // tm,),
        in_specs=in_specs, out_specs=out_specs, out_shape=out_shapes,
        compiler_params=_cparams("parallel"),
        name="norm_proj",
    )(x, gain[None, :], sh_arr, sc_arr, *weights)


def _shift_rows(x, s, fill):
    rows = lax.broadcasted_iota(I32, x.shape, 0)
    return jnp.where(rows >= s, pltpu.roll(x, s, axis=0), fill)


def _cumsum_rows(x):
    n = x.shape[0]
    s = 1
    while s < n:
        x = x + _shift_rows(x, s, 0.0)
        s *= 2
    return x


def _linear_scan_rows(a, b, h_init):
    n, w = a.shape
    a = a.reshape(n // SUBLANES, SUBLANES, w)
    b = b.reshape(n // SUBLANES, SUBLANES, w)
    in_group = lax.broadcasted_iota(I32, a.shape, 1)
    s = 1
    while s < SUBLANES:
        keep = in_group >= s
        b = a * jnp.where(keep, pltpu.roll(b, s, axis=1), 0.0) + b
        a = a * jnp.where(keep, pltpu.roll(a, s, axis=1), 1.0)
        s *= 2
    groups, carry = [], h_init
    for g in range(n // SUBLANES):
        h = a[g] * carry + b[g]
        groups.append(h)
        carry = h[SUBLANES - 1:SUBLANES, :]
    return jnp.concatenate(groups, axis=0)


def _gelu_tanh(x):
    return 0.5 * x * (1.0 + jnp.tanh(0.7978845608028654 * (x + 0.044715 * (x * x * x))))


def _ab_mixer_kernel(first_pos_zero, tt, chunk,
                     qk_ref, v_ref, g_ref, gk_ref, xr_ref, xg_ref, s0_ref, h0_ref, buf0_ref,
                     wgk_ref, bgk_ref, gnorm_ref, cw_ref, cb_ref, wa_ref, wi_ref, ba_ref, bi_ref, lam_ref,
                     mix_ref, s_out_ref, h_out_ref, buf_out_ref,
                     st_ref, hc_ref, xpad_ref):
    ti = pl.program_id(1)
    n_t = pl.num_programs(1)
    dqk = qk_ref.shape[1] // 2
    dv = v_ref.shape[1]
    hk = dqk // GLA_HEADS
    hv = dv // GLA_HEADS
    w_lru = xr_ref.shape[1]

    @pl.when(ti == 0)
    def _():
        st_ref[...] = s0_ref[...]
        hc_ref[...] = h0_ref[...]
        xpad_ref[0:SUBLANES, :] = buf0_ref[...]

    xpad_ref[SUBLANES:SUBLANES + tt, :] = xr_ref[...]
    xc = cb_ref[...]
    for i in range(CONV_WIDTH):
        off = SUBLANES - (CONV_WIDTH - 1) + i
        xc = xc + cw_ref[i:i + 1, :] * xpad_ref[off:off + tt, :]
    tail = xpad_ref[tt:tt + SUBLANES, :]
    xpad_ref[0:SUBLANES, :] = tail
    xc_b = xc.astype(BF16)
    half = w_lru // 2
    r_lin = jnp.concatenate([jnp.dot(xc_b[:, j * half:(j + 1) * half], wa_ref[j], preferred_element_type=F32)
                             for j in range(2)], axis=-1)
    i_lin = jnp.concatenate([jnp.dot(xc_b[:, j * half:(j + 1) * half], wi_ref[j], preferred_element_type=F32)
                             for j in range(2)], axis=-1)
    r = _sigmoid(r_lin + ba_ref[...])
    i_g = _sigmoid(i_lin + bi_ref[...])
    log_at = (-LRU_C) * r * _softplus(-lam_ref[...])
    a = jnp.exp(log_at)
    om = 1.0 - a * a
    mult = jnp.where(om > 0.0, om * lax.rsqrt(om), 0.0)
    if first_pos_zero:
        rows = lax.broadcasted_iota(I32, mult.shape, 0)
        mult = jnp.where((rows == 0) & (ti == 0), 1.0, mult)
    bterm = mult * (i_g * xc)
    hs = _linear_scan_rows(a, bterm, hc_ref[...])
    hc_ref[...] = hs[tt - 1:tt, :]
    y_b = hs * _gelu_tanh(xg_ref[...])
    mix_ref[:, dv:] = y_b.astype(BF16)

    lane_k = lax.broadcasted_iota(I32, (1, dqk), 1)
    st_rows = lax.broadcasted_iota(I32, (dv, dqk), 0)
    st_cols = lax.broadcasted_iota(I32, (dv, dqk), 1)
    st_mask = functools.reduce(
        jnp.logical_or,
        [(st_rows >= h * hv) & (st_rows < (h + 1) * hv) & (st_cols >= h * hk) & (st_cols < (h + 1) * hk)
         for h in range(GLA_HEADS)])
    crow = lax.broadcasted_iota(I32, (chunk, chunk), 0)
    ccol = lax.broadcasted_iota(I32, (chunk, chunk), 1)
    causal = ccol <= crow
    mid = chunk // 2 - 1
    for c in range(tt // chunk):
        rs = slice(c * chunk, (c + 1) * chunk)
        q = qk_ref[rs, 0:dqk] * (hk ** -0.5)
        k = qk_ref[rs, dqk:2 * dqk]
        v_b = v_ref[rs, :].astype(BF16)
        u = jnp.dot(gk_ref[rs, :].astype(BF16), wgk_ref[...], preferred_element_type=F32) + bgk_ref[...]
        b = _cumsum_rows(_log_sigmoid(u) * (1.0 / GLA_GATE_TAU))
        b_mid = b[mid:mid + 1, :]
        b_last = b[chunk - 1:chunk, :]
        q_e = q * jnp.exp(b - b_mid)
        k_e = (k * jnp.exp(b_mid - b)).astype(BF16)
        k_tail = (k * jnp.exp(b_last - b)).astype(BF16)
        q_dec = (q_e * jnp.exp(b_mid)).astype(BF16)
        q_e = q_e.astype(BF16)
        st = st_ref[...]
        o = lax.dot_general(q_dec, st.astype(BF16), (((1,), (1,)), ((), ())), preferred_element_type=F32)
        o_intra = []
        for h in range(GLA_HEADS):
            q_h = jnp.where((lane_k >= h * hk) & (lane_k < (h + 1) * hk), q_e, jnp.zeros_like(q_e))
            att = lax.dot_general(q_h, k_e, (((1,), (1,)), ((), ())), preferred_element_type=F32)
            att = jnp.where(causal, att, 0.0).astype(BF16)
            o_intra.append(jnp.dot(att, v_b[:, h * hv:(h + 1) * hv], preferred_element_type=F32))
        o = o + jnp.concatenate(o_intra, axis=-1)
        kv = lax.dot_general(v_b, k_tail, (((0,), (0,)), ((), ())), preferred_element_type=F32)
        st_ref[...] = st * jnp.exp(b_last) + jnp.where(st_mask, kv, 0.0)
        g = g_ref[rs, :]
        gate = g * _sigmoid(g)
        y_a = []
        for h in range(GLA_HEADS):
            o_h = o[:, h * hv:(h + 1) * hv]
            ms = jnp.mean(o_h * o_h, axis=-1, keepdims=True)
            y_a.append(o_h * lax.rsqrt(ms + EPS) * gnorm_ref[...] * gate[:, h * hv:(h + 1) * hv])
        mix_ref[rs, 0:dv] = jnp.concatenate(y_a, axis=-1).astype(BF16)

    @pl.when(ti == n_t - 1)
    def _():
        s_out_ref[...] = st_ref[...]
        h_out_ref[...] = hc_ref[...]
        buf_out_ref[...] = xpad_ref[0:SUBLANES, :]


def _ab_mixer(proj, bsz, t_len, first_pos_zero, s_gla, h_lru, conv_buf, w_gk2, b_gk, gla_norm, conv_w, conv_b,
              lru_wa, lru_ba, lru_wi, lru_bi, lru_lam):
    qk, v, g, gk, xr, xg = proj
    n = qk.shape[0]
    dqk, dv, w_lru = qk.shape[1] // 2, v.shape[1], xr.shape[1]
    hk, hv = dqk // GLA_HEADS, dv // GLA_HEADS
    tt = MIX_TILE if t_len % MIX_TILE == 0 else t_len
    chunk = GLA_CHUNK if tt % GLA_CHUNK == 0 else tt
    n_t = t_len // tt

    eye = jnp.eye(GLA_HEADS, dtype=F32)
    st0 = jnp.einsum('bhkv,hg->bhvgk', s_gla.astype(F32), eye).reshape(bsz, dv, dqk)
    buf0 = jnp.pad(conv_buf.astype(F32), ((0, 0), (SUBLANES - (CONV_WIDTH - 1), 0), (0, 0)))
    wgk = jnp.pad(w_gk2, ((0, gk.shape[1] - w_gk2.shape[0]), (0, 0))).astype(BF16)
    nb = lru_wa.shape[0]
    half_blocks = nb // 2

    def block_diag(w):
        bd = w.shape[1]
        e = jnp.eye(half_blocks, dtype=w.dtype)
        w2 = w.reshape(2, half_blocks, bd, bd)
        return jnp.einsum('jnio,nm->jnimo', w2, e).reshape(2, half_blocks * bd, half_blocks * bd).astype(BF16)

    row = lambda z: z.astype(F32)[None, :]
    tok = lambda w: pl.BlockSpec((tt, w), lambda b, i: (b * n_t + i, 0))
    per_b = lambda r, c: pl.BlockSpec((None, r, c), lambda b, i: (b, 0, 0))
    full2 = lambda r, c: pl.BlockSpec((r, c), lambda b, i: (0, 0))
    full3 = lambda a, r, c: pl.BlockSpec((a, r, c), lambda b, i: (0, 0, 0))
    mix, st, h_new, buf = pl.pallas_call(
        functools.partial(_ab_mixer_kernel, first_pos_zero, tt, chunk),
        grid=(bsz, n_t),
        in_specs=[tok(2 * dqk), tok(dv), tok(dv), tok(gk.shape[1]), tok(w_lru), tok(w_lru),
                  per_b(dv, dqk), per_b(1, w_lru), per_b(SUBLANES, w_lru),
                  full2(gk.shape[1], dqk), full2(1, dqk), full2(1, hv), full2(CONV_WIDTH, w_lru), full2(1, w_lru),
                  full3(2, w_lru // 2, w_lru // 2), full3(2, w_lru // 2, w_lru // 2),
                  full2(1, w_lru), full2(1, w_lru), full2(1, w_lru)],
        out_specs=[tok(dv + w_lru), per_b(dv, dqk), per_b(1, w_lru), per_b(SUBLANES, w_lru)],
        out_shape=[jax.ShapeDtypeStruct((n, dv + w_lru), BF16),
                   jax.ShapeDtypeStruct((bsz, dv, dqk), F32),
                   jax.ShapeDtypeStruct((bsz, 1, w_lru), F32),
                   jax.ShapeDtypeStruct((bsz, SUBLANES, w_lru), F32)],
        scratch_shapes=[pltpu.VMEM((dv, dqk), F32), pltpu.VMEM((1, w_lru), F32),
                        pltpu.VMEM((tt + SUBLANES, w_lru), F32)],
        compiler_params=_cparams("parallel", "arbitrary"),
        name="ab_mixer",
    )(qk, v, g, gk, xr, xg, st0, h_lru.astype(F32)[:, None, :], buf0,
      wgk, row(b_gk), row(gla_norm), conv_w.astype(F32), row(conv_b),
      block_diag(lru_wa), block_diag(lru_wi), row(lru_ba), row(lru_bi), row(lru_lam))
    st5 = st.reshape(bsz, GLA_HEADS, hv, GLA_HEADS, hk)
    s_new = jnp.stack([st5[:, h, :, h, :] for h in range(GLA_HEADS)], axis=1).swapaxes(-1, -2)
    return mix, s_new, h_new[:, 0], buf[:, SUBLANES - (CONV_WIDTH - 1):]


SB_MASKED = -1e30
SB_DEAD = -152.0
LOG2_E = 1.4426950408889634


def _sb_logits(q_h, k_blk, keys_on_lanes):
    if keys_on_lanes:
        return jnp.dot(q_h, k_blk, preferred_element_type=F32)
    return lax.dot_general(q_h, k_blk, (((1,), (1,)), ((), ())), preferred_element_type=F32)


def _sb_gates(z, mask):
    neg_abs = pltpu.bitcast(pltpu.bitcast(z, I32) | jnp.int32(-2 ** 31), F32)
    ls = jnp.minimum(z, 0.0) - jnp.log2(1.0 + jnp.exp2(neg_abs))
    lk = ls - z
    if mask is not None:
        lk = jnp.where(mask, lk, 0.0)
    return ls, lk.astype(BF16), jnp.sum(lk, axis=-1, keepdims=True)


def _sb_prefix(lk, upper, suffix, mask):
    pre = jnp.dot(lk, upper, preferred_element_type=F32) + suffix
    if mask is not None:
        pre = jnp.where(mask, pre, SB_MASKED)
    return pre


def _sb_values(ls, pre, v_blk, keys_on_lanes):
    w = jnp.exp2(ls + pre).astype(BF16)
    if keys_on_lanes:
        return lax.dot_general(w, v_blk, (((1,), (1,)), ((), ())), preferred_element_type=F32)
    return jnp.dot(w, v_blk, preferred_element_type=F32)


def _split_heads(q, dh):
    lane = lax.broadcasted_iota(I32, (1, q.shape[1]), 1)
    zero = jnp.zeros_like(q)
    return [jnp.where((lane >= h * dh) & (lane < (h + 1) * dh), q, zero) for h in range(q.shape[1] // dh)], lane


def _merge_heads(acc, lane, dh):
    out = acc[0]
    for h in range(1, len(acc)):
        out = jnp.where(lane >= h * dh, acc[h], out)
    return out


def _upper(n):
    r = lax.broadcasted_iota(I32, (n, n), 0)
    c = lax.broadcasted_iota(I32, (n, n), 1)
    return jnp.where(r > c, 1.0, 0.0).astype(BF16)


def _group_queries(q, lanes, dh):
    groups, lane = [], None
    for g in range(q.shape[1] // lanes):
        heads_g, lane = _split_heads(q[:, g * lanes:(g + 1) * lanes], dh)
        groups.append(heads_g)
    return groups, lane


def _sb_block(q_groups, lanes, k_rows, v_rows, upper, mask, acc_ref, suf_ref, keys_on_lanes=False):
    per_group = len(q_groups[0])
    for g, heads_g in enumerate(q_groups):
        cols = slice(g * lanes, (g + 1) * lanes)
        k_blk = k_rows[cols, :] if keys_on_lanes else k_rows[:, cols]
        v_blk = v_rows[cols, :] if keys_on_lanes else v_rows[:, cols]
        for h, q_h in enumerate(heads_g):
            i = g * per_group + h
            ls, lk, tot = _sb_gates(_sb_logits(q_h, k_blk, keys_on_lanes), mask)
            acc_ref[i] += _sb_values(ls, _sb_prefix(lk, upper, suf_ref[i], mask), v_blk, keys_on_lanes)
            suf_ref[i] += tot


def _sb_live(suf_ref):
    return (jnp.max(suf_ref[...]) > SB_DEAD).astype(I32)


def _sb_output(acc_ref, q_groups, lane, dh):
    per_group = len(q_groups[0])
    return jnp.concatenate([_merge_heads([acc_ref[g * per_group + h] for h in range(per_group)], lane, dh)
                            for g in range(len(q_groups))], axis=-1)


def _sb_prompt_kernel(blk, dh, lanes, q_ref, k_ref, v_ref, o_ref, acc_ref, suf_ref):
    qi = pl.program_id(2)
    n_blk = qi + 1
    q_groups, lane = _group_queries(q_ref[...], lanes, dh)
    upper = _upper(blk)
    diag = lax.broadcasted_iota(I32, (blk, blk), 1) < lax.broadcasted_iota(I32, (blk, blk), 0)

    def block(s, mask):
        rows = pl.ds(pl.multiple_of((qi - s) * blk, blk), blk)
        _sb_block(q_groups, lanes, k_ref[rows, :], v_ref[rows, :], upper, mask, acc_ref, suf_ref)

    acc_ref[...] = jnp.zeros_like(acc_ref)
    suf_ref[...] = jnp.zeros_like(suf_ref)
    block(0, diag)

    def more(carry):
        s, go = carry
        return (s < n_blk) & (go > 0)

    def step(carry):
        s, _ = carry
        block(s, None)
        return s + 1, _sb_live(suf_ref)

    lax.while_loop(more, step, (jnp.int32(1), _sb_live(suf_ref)))
    o_ref[...] = _sb_output(acc_ref, q_groups, lane, dh).astype(o_ref.dtype)


def _sb_prompt(q, k, v, bsz, t_len, dh):
    n, width = q.shape
    blk = SB_BLOCK
    nq = t_len // blk
    step_w = SB_PROMPT_WIDTH
    lanes = SB_GROUP_LANES
    heads = step_w // dh
    return pl.pallas_call(
        functools.partial(_sb_prompt_kernel, blk, dh, lanes),
        grid=(bsz, width // step_w, nq),
        in_specs=[pl.BlockSpec((blk, step_w), lambda b, g, i: (b * nq + i, g)),
                  pl.BlockSpec((t_len, step_w), lambda b, g, i: (b, g)),
                  pl.BlockSpec((t_len, step_w), lambda b, g, i: (b, g))],
        out_specs=pl.BlockSpec((blk, step_w), lambda b, g, i: (b * nq + i, g)),
        out_shape=jax.ShapeDtypeStruct((n, width), BF16),
        scratch_shapes=[pltpu.VMEM((heads, blk, lanes), F32), pltpu.VMEM((heads, blk, 1), F32)],
        compiler_params=_cparams("parallel", "parallel", "arbitrary"),
        name="sb_prompt",
    )(q, k, v)


def _sb_sample_kernel(t_new, past_len, blk, dh, lanes, q_ref, k_ref, v_ref, pk_ref, pv_ref, o_ref, acc_ref, suf_ref):
    width = q_ref.shape[1]
    q_groups, lane = _group_queries(q_ref[...], lanes, dh)
    diag = lax.broadcasted_iota(I32, (t_new, t_new), 1) < lax.broadcasted_iota(I32, (t_new, t_new), 0)
    acc_ref[...] = jnp.zeros_like(acc_ref)
    suf_ref[...] = jnp.zeros_like(suf_ref)
    _sb_block(q_groups, lanes, k_ref[...], v_ref[...], _upper(t_new), diag, acc_ref, suf_ref)
    upper = _upper(blk)

    def more(carry):
        j, go = carry
        return (j >= 0) & (go > 0)

    def step(carry):
        j, _ = carry
        keys = pl.ds(pl.multiple_of(j * blk, blk), blk)
        _sb_block(q_groups, lanes, pk_ref[:, :, keys].reshape(width, blk).astype(BF16),
                  pv_ref[:, :, keys].reshape(width, blk).astype(BF16), upper, None, acc_ref, suf_ref,
                  keys_on_lanes=True)
        return j - 1, _sb_live(suf_ref)

    lax.while_loop(more, step, (jnp.int32(past_len // blk - 1), _sb_live(suf_ref)))
    o_ref[...] = _sb_output(acc_ref, q_groups, lane, dh).astype(o_ref.dtype)


def _sb_sample(q, k, v, past_k, past_v, bsz, t_len):
    n, width = q.shape
    _, heads, dh, past_len = past_k.shape
    blk = SB_BLOCK if past_len % SB_BLOCK == 0 else past_len
    lanes = SB_GROUP_LANES
    new = pl.BlockSpec((t_len, width), lambda b: (b, 0))
    past = pl.BlockSpec((None, heads, dh, past_len), lambda b: (b, 0, 0, 0))
    return pl.pallas_call(
        functools.partial(_sb_sample_kernel, t_len, past_len, blk, dh, lanes),
        grid=(bsz,),
        in_specs=[new, new, new, past, past],
        out_specs=new,
        out_shape=jax.ShapeDtypeStruct((n, width), BF16),
        scratch_shapes=[pltpu.VMEM((heads, t_len, lanes), F32), pltpu.VMEM((heads, t_len, 1), F32)],
        compiler_params=_cparams("parallel"),
        name="sb_sample",
    )(q, k, v, past_k, past_v)


def _first_argmax(vals):
    best_v, best_i = vals[0], jnp.zeros(vals[0].shape, I32)
    for i in range(1, len(vals)):
        better = vals[i] > best_v
        best_v = jnp.where(better, vals[i], best_v)
        best_i = jnp.where(better, i, best_i)
    return best_v, best_i


def _route_rows(logits_t, bias_col):
    s = _sigmoid(logits_t)
    sel = s + bias_col
    epg = EXPERTS_PER_GROUP
    scores = []
    for gi in range(N_GROUPS):
        rows = [sel[gi * epg + i:gi * epg + i + 1, :] for i in range(epg)]
        pair_sums = [rows[i] + rows[j] for i in range(epg) for j in range(i + 1, epg)]
        scores.append(functools.reduce(jnp.maximum, pair_sums))
    _, best = _first_argmax(scores)

    def in_best(mat, i):
        out = mat[i:i + 1, :]
        for gi in range(1, N_GROUPS):
            out = jnp.where(best == gi, mat[gi * epg + i:gi * epg + i + 1, :], out)
        return out

    sel_g = [in_best(sel, i) for i in range(epg)]
    s_g = [in_best(s, i) for i in range(epg)]
    _, i1 = _first_argmax(sel_g)
    _, i2 = _first_argmax([jnp.where(i1 == i, -jnp.inf, sel_g[i]) for i in range(epg)])
    lo = jnp.minimum(i1, i2)
    hi = jnp.maximum(i1, i2)
    pick = lambda idx: functools.reduce(lambda acc, i: jnp.where(idx == i, s_g[i], acc), range(1, epg), s_g[0])
    w_lo, w_hi = pick(lo), pick(hi)
    tot = w_lo + w_hi
    pair = jnp.where(lo == 0, 0, jnp.where(lo == 1, epg - 1, 2 * epg - 3)) + (hi - lo - 1)
    return best * N_PAIRS + pair, w_lo / tot, w_hi / tot


def _proj_route_kernel(a_ref, w_ref, x_ref, g1_ref, gain_ref, sh_ref, sc_ref, wr_hi_ref, wr_lo_ref, br_ref,
                       xn_ref, h2_ref, ri_ref):
    y = jnp.dot(a_ref[...], w_ref[...], preferred_element_type=F32)
    xn = x_ref[...] + g1_ref[...] * y
    xn_ref[...] = xn
    d = xn.shape[1]
    h2 = _rms_mod(xn, gain_ref[...], sh_ref[...], sc_ref[...])
    h2_ref[:, 0:d] = h2
    h_hi = h2.astype(BF16)
    h_lo = (h2 - h_hi.astype(F32)).astype(BF16)
    logits = (jnp.dot(h_hi, wr_hi_ref[...], preferred_element_type=F32)
              + jnp.dot(h_lo, wr_hi_ref[...], preferred_element_type=F32)
              + jnp.dot(h_hi, wr_lo_ref[...], preferred_element_type=F32))
    n_e = br_ref.shape[0]
    logits_t = logits.T[0:n_e, :]
    cls, w_lo, w_hi = _route_rows(logits_t, br_ref[...])
    tm = logits_t.shape[1]
    h2_ref[:, d:] = jnp.concatenate([w_lo, w_hi, jnp.zeros((LANES - 2, tm), F32)], axis=0).T
    ri_ref[...] = jnp.concatenate([cls, jnp.zeros((SUBLANES - 1, tm), I32)], axis=0)


def _proj_route(a, w_out, x, t_len, g1, gain, shift, scale, w_router, b_router):
    n, d = x.shape
    k = a.shape[1]
    n_e = w_router.shape[1]
    tm = _row_tile(n, t_len, ROW_TILE)
    g1_arr, g1_spec = _mod_operand(g1, n, t_len, tm)
    sh_arr, sh_spec = _mod_operand(shift, n, t_len, tm)
    sc_arr, sc_spec = _mod_operand(scale, n, t_len, tm)
    wr = jnp.pad(w_router.astype(F32), ((0, 0), (0, LANES - n_e)))
    wr_hi = wr.astype(BF16)
    wr_lo = (wr - wr_hi.astype(F32)).astype(BF16)
    tokd = pl.BlockSpec((tm, d), lambda i: (i, 0))
    const = lambda r, c: pl.BlockSpec((r, c), lambda i: (0, 0))
    route = pl.BlockSpec((SUBLANES, tm), lambda i: (0, i))
    return pl.pallas_call(
        _proj_route_kernel,
        grid=(n // tm,),
        in_specs=[pl.BlockSpec((tm, k), lambda i: (i, 0)), const(k, d), tokd, g1_spec(0), const(1, d),
                  sh_spec(0), sc_spec(0), const(d, LANES), const(d, LANES), const(n_e, 1)],
        out_specs=[tokd, pl.BlockSpec((tm, d + LANES), lambda i: (i, 0)), route],
        out_shape=[jax.ShapeDtypeStruct((n, d), F32), jax.ShapeDtypeStruct((n, d + LANES), F32),
                   jax.ShapeDtypeStruct((SUBLANES, n), I32)],
        compiler_params=_cparams("parallel"),
        name="proj_route",
    )(a, w_out, x, g1_arr, gain[None, :], sh_arr, sc_arr, wr_hi, wr_lo, b_router.astype(F32)[:, None])


def _gather_rows(src_hbm, idx_ref, base, n_rows, dst, sem):
    last = idx_ref.shape[0] - 1

    for r in range(n_rows):
        row = idx_ref[jnp.minimum(base + r, last)]
        pltpu.make_async_copy(src_hbm.at[pl.ds(row, 1)], dst.at[pl.ds(r, 1)], sem).start()


def _wait_rows(src_hbm, n_rows, dst, sem):
    pltpu.make_async_copy(src_hbm.at[pl.ds(0, n_rows)], dst, sem).wait()


def _moe_kernel(bm, d, order_ref, base_ref, valid_ref, e_lo_ref, e_hi_ref, used_ref,
                h_hbm, w1a_ref, w3a_ref, w2a_ref, w1b_ref, w3b_ref, w2b_ref,
                y_ref, buf0_ref, buf1_ref, sem_ref):
    i = pl.program_id(0)
    n_used = used_ref[0]

    @pl.when((i == 0) & (n_used > 0))
    def _():
        _gather_rows(h_hbm, order_ref, base_ref[0], bm, buf0_ref, sem_ref.at[0])

    def block(cur, cur_sem, nxt, nxt_sem):
        _wait_rows(h_hbm, bm, cur, cur_sem)
        xb = cur[:, 0:d].astype(BF16)
        live = lax.broadcasted_iota(I32, (bm, LANES), 0) < valid_ref[i]
        wts = jnp.where(live, cur[:, d:], 0.0)
        _gather_rows(h_hbm, order_ref, base_ref[i + 1], bm, nxt, nxt_sem)

        def expert(w1_ref, w3_ref, w2_ref):
            u = jnp.dot(xb, w1_ref[...], preferred_element_type=F32)
            t = jnp.dot(xb, w3_ref[...], preferred_element_type=F32)
            mid = (u * _sigmoid(u) * t).astype(BF16)
            return jnp.dot(mid, w2_ref[...], preferred_element_type=F32)

        y_ref[...] = (expert(w1a_ref, w3a_ref, w2a_ref) * wts[:, 0:1]
                      + expert(w1b_ref, w3b_ref, w2b_ref) * wts[:, 1:2])

        @pl.when(i == n_used - 1)
        def _():
            _wait_rows(h_hbm, bm, nxt, nxt_sem)

    @pl.when((i < n_used) & (i % 2 == 0))
    def _():
        block(buf0_ref, sem_ref.at[0], buf1_ref, sem_ref.at[1])

    @pl.when((i < n_used) & (i % 2 == 1))
    def _():
        block(buf1_ref, sem_ref.at[1], buf0_ref, sem_ref.at[0])

    @pl.when(i >= n_used)
    def _():
        y_ref[...] = jnp.zeros_like(y_ref)


def _moe_sorted(h2w, order, blk_base, blk_valid, blk_lo, blk_hi, n_used, w1, w3, w2):
    n, dw = h2w.shape
    d = dw - LANES
    n_blocks = blk_valid.shape[0]
    bm = _moe_block(n)
    de = w1.shape[2]

    def wspec(which, r, c):
        def index(i, order, base, valid, lo, hi, used):
            blk = jnp.minimum(i, jnp.maximum(used[0] - 1, 0))
            return ((lo, hi)[which][blk], 0, 0)
        return pl.BlockSpec((None, r, c), index)

    return pl.pallas_call(
        functools.partial(_moe_kernel, bm, d),
        grid_spec=pltpu.PrefetchScalarGridSpec(
            num_scalar_prefetch=6,
            grid=(n_blocks,),
            in_specs=[pl.BlockSpec(memory_space=pl.ANY),
                      wspec(0, d, de), wspec(0, d, de), wspec(0, de, d),
                      wspec(1, d, de), wspec(1, d, de), wspec(1, de, d)],
            out_specs=pl.BlockSpec((bm, d), lambda i, *_: (i, 0)),
            scratch_shapes=[pltpu.VMEM((bm, dw), F32), pltpu.VMEM((bm, dw), F32), pltpu.SemaphoreType.DMA((2,))]),
        out_shape=jax.ShapeDtypeStruct((n_blocks * bm, d), F32),
        compiler_params=_cparams("arbitrary"),
        name="moe_sorted",
    )(order, blk_base, blk_valid, blk_lo, blk_hi, n_used, h2w, w1, w3, w2, w1, w3, w2)


def _unsort_kernel(tm, final_norm, pos_ref, y_hbm, x_ref, g2_ref, gain_ref, o_ref, buf0_ref, buf1_ref, sem_ref):
    i = pl.program_id(0)
    n_i = pl.num_programs(0)

    @pl.when(i == 0)
    def _():
        _gather_rows(y_hbm, pos_ref, 0, tm, buf0_ref, sem_ref.at[0])

    def step(cur, cur_sem, nxt, nxt_sem):
        @pl.when(i + 1 < n_i)
        def _():
            _gather_rows(y_hbm, pos_ref, (i + 1) * tm, tm, nxt, nxt_sem)

        _wait_rows(y_hbm, tm, cur, cur_sem)
        x = x_ref[...] + g2_ref[...] * cur[...]
        if final_norm:
            ms = jnp.mean(x * x, axis=-1, keepdims=True)
            x = x * lax.rsqrt(ms + EPS) * gain_ref[...]
        o_ref[...] = x

    @pl.when(i % 2 == 0)
    def _():
        step(buf0_ref, sem_ref.at[0], buf1_ref, sem_ref.at[1])

    @pl.when(i % 2 == 1)
    def _():
        step(buf1_ref, sem_ref.at[1], buf0_ref, sem_ref.at[0])


def _unsort_resid(y_sorted, pos, x, t_len, g2, final_gain):
    n, d = x.shape
    tm = _row_tile(n, t_len, GATHER_TILE)
    g2_arr, g2_spec = _mod_operand(g2, n, t_len, tm)
    final_norm = final_gain is not None
    gain = (final_gain if final_norm else jnp.ones((d,), F32)).astype(F32)[None, :]
    return pl.pallas_call(
        functools.partial(_unsort_kernel, tm, final_norm),
        grid_spec=pltpu.PrefetchScalarGridSpec(
            num_scalar_prefetch=1,
            grid=(n // tm,),
            in_specs=[pl.BlockSpec(memory_space=pl.ANY),
                      pl.BlockSpec((tm, d), lambda i, pos: (i, 0)),
                      g2_spec(0),
                      pl.BlockSpec((1, d), lambda i, pos: (0, 0))],
            out_specs=pl.BlockSpec((tm, d), lambda i, pos: (i, 0)),
            scratch_shapes=[pltpu.VMEM((tm, d), F32), pltpu.VMEM((tm, d), F32), pltpu.SemaphoreType.DMA((2,))]),
        out_shape=jax.ShapeDtypeStruct((n, d), F32),
        compiler_params=_cparams("arbitrary"),
        name="unsort_resid",
    )(pos, y_sorted, x, g2_arr, gain)


def _sort_plan(cls):
    n = cls.shape[0]
    bm = _moe_block(n)
    n_blocks = -(-n // bm) + N_CLASSES
    classes = jnp.arange(N_CLASSES, dtype=I32)
    cls_sorted, order = lax.sort((cls, jnp.arange(n, dtype=I32)), num_keys=1, is_stable=True)
    counts = jnp.sum((cls[:, None] == classes[None, :]).astype(I32), axis=0)
    padded = (counts + bm - 1) // bm * bm
    pad_end = jnp.cumsum(padded)
    pad_start = pad_end - padded
    start = jnp.cumsum(counts) - counts
    shift = jnp.sum(jnp.where(cls_sorted[:, None] == classes[None, :], (pad_start - start)[None, :], 0), axis=1)
    dest = jnp.arange(n, dtype=I32) + shift
    _, pos = lax.sort((order, dest), num_keys=1)
    blk_first = jnp.arange(n_blocks + 1, dtype=I32) * bm
    blk_cls = jnp.minimum(jnp.sum((pad_end[None, :] <= blk_first[:, None]).astype(I32), axis=1), N_CLASSES - 1)
    onehot = blk_cls[:, None] == classes[None, :]
    pick = lambda table: jnp.sum(jnp.where(onehot, table[None, :], 0), axis=1)
    into = blk_first - pick(pad_start)
    blk_base = jnp.clip(pick(start) + into, 0, n)
    blk_valid = jnp.clip(pick(counts) - into, 0, bm)[:n_blocks]
    pair_lo = jnp.array([i for i in range(EXPERTS_PER_GROUP) for j in range(i + 1, EXPERTS_PER_GROUP)], I32)
    pair_hi = jnp.array([j for i in range(EXPERTS_PER_GROUP) for j in range(i + 1, EXPERTS_PER_GROUP)], I32)
    group = classes // N_PAIRS
    blk_lo = pick(group * EXPERTS_PER_GROUP + pair_lo[classes % N_PAIRS])
    blk_hi = pick(group * EXPERTS_PER_GROUP + pair_hi[classes % N_PAIRS])
    n_used = (pad_end[-1] // bm).astype(I32)[None]
    return order, pos, blk_base.astype(I32), blk_valid.astype(I32), blk_lo, blk_hi, n_used


def _moe_layer(h2w, route_i, x, t_len, g2, w1, w3, w2, final_gain):
    order, pos, blk_base, blk_valid, blk_lo, blk_hi, n_used = _sort_plan(route_i[0])
    y_sorted = _moe_sorted(h2w, order, blk_base, blk_valid, blk_lo, blk_hi, n_used, w1, w3, w2)
    return _unsort_resid(y_sorted, pos, x, t_len, g2, final_gain)


def _trunk(x3, c, p, past):
    bsz, t_len, d = x3.shape
    n = bsz * t_len
    x = x3.reshape(n, d)
    depth = p['w_ada'].shape[0]
    mod = _ada(c.astype(F32), p['w_ada'], p['b_ada']).reshape(depth, bsz, N_MOD, d)
    gla_s, lru_s, conv_s, ks, vs = [], [], [], [], []
    for l in range(depth):
        sh1, sc1, g1, sh2, sc2, g2 = [mod[l, :, i] for i in range(N_MOD)]
        j = l // 2
        if l % 2 == 0:
            w_in = p['w_ab_in'][j]
            dqk2 = 2 * p['w_gk2'].shape[2]
            dv = GLA_HEADS * p['gla_norm'].shape[1]
            rank = p['w_gk2'].shape[1]
            w_lru = p['lru_lam'].shape[1]
            cuts = [0, dqk2, dqk2 + dv, dqk2 + 2 * dv, dqk2 + 2 * dv + rank,
                    dqk2 + 2 * dv + rank + w_lru, dqk2 + 2 * dv + rank + 2 * w_lru]
            cols = [w_in[:, cuts[i]:cuts[i + 1]] for i in range(6)]
            cols[3] = jnp.pad(cols[3], ((0, 0), (0, LANES - rank)))
            proj = _norm_proj(x, t_len, p['norm_mix'][l], sh1, sc1,
                              [(w.astype(BF16), 1.0, [(F32, None)]) for w in cols])
            if past is None:
                s0 = jnp.zeros((bsz, GLA_HEADS, dqk2 // 2 // GLA_HEADS, dv // GLA_HEADS), F32)
                h0 = jnp.zeros((bsz, w_lru), F32)
                buf = jnp.zeros((bsz, CONV_WIDTH - 1, w_lru), F32)
            else:
                s0, h0, buf = past['state_gla'][j], past['state_lru'][j], past['state_conv'][j]
            mix, sg, sl, sc = _ab_mixer(proj, bsz, t_len, past is None, s0, h0, buf, p['w_gk2'][j], p['b_gk'][j],
                                        p['gla_norm'][j], p['conv_w'][j], p['conv_b'][j], p['lru_wa'][j],
                                        p['lru_ba'][j], p['lru_wi'][j], p['lru_bi'][j], p['lru_lam'][j])
            gla_s.append(sg)
            lru_s.append(sl)
            conv_s.append(sc)
            w_out = p['w_ab_out'][j]
        else:
            w_qkv = p['w_sb_qkv'][j]
            width = w_qkv.shape[1] // 3
            heads = past['cache_k'].shape[3] if past is not None else p['sb_heads']
            dh = width // heads
            split = heads if t_len % ROW_TILE == 0 else None
            pieces = [(w_qkv[:, 0:width].astype(BF16), LOG2_E * dh ** -0.5, [(BF16, None)]),
                      (w_qkv[:, width:2 * width].astype(BF16), 1.0, [(F32, split), (BF16, None)]),
                      (w_qkv[:, 2 * width:].astype(BF16), 1.0, [(F32, split), (BF16, None)])]
            q_b, k_f, k_b, v_f, v_b = _norm_proj(x, t_len, p['norm_mix'][l], sh1, sc1, pieces)
            if past is None:
                mix = _sb_prompt(q_b, k_b, v_b, bsz, t_len, dh)
            else:
                time_minor = lambda c: c.astype(F32).transpose(0, 2, 3, 1)
                mix = _sb_sample(q_b, k_b, v_b, time_minor(past['cache_k'][j]), time_minor(past['cache_v'][j]),
                                 bsz, t_len)
            rows = lambda z: z.transpose(0, 3, 1, 2) if split else z.reshape(bsz, t_len, heads, dh)
            ks.append(rows(k_f))
            vs.append(rows(v_f))
            w_out = p['w_sb_out'][j]
        xn, h2w, route_i = _proj_route(mix, w_out.astype(BF16), x, t_len, g1, p['norm_ffn'][l], sh2, sc2,
                                       p['w_router'], p['b_router'])
        x = _moe_layer(h2w, route_i, xn, t_len, g2,
                       p['w_e1'][l].astype(BF16), p['w_e3'][l].astype(BF16), p['w_e2'][l].astype(BF16),
                       p['norm_out'] if l == depth - 1 else None)
    stack = lambda xs: xs[0][None] if len(xs) == 1 else jnp.stack(xs)
    return x.reshape(bsz, t_len, d), (stack(gla_s), stack(lru_s), stack(conv_s), stack(ks), stack(vs))


def kernel(x_prompt, x_sample, state_gla, state_lru, state_conv, cache_k, cache_v, c_prompt, c_sample,
           w_ada, b_ada, norm_mix, norm_ffn, norm_out, w_ab_in, w_gk2, b_gk, gla_norm, conv_w, conv_b,
           lru_wa, lru_ba, lru_wi, lru_bi, lru_lam, w_ab_out, w_sb_qkv, w_sb_out, w_router, b_router,
           w_e1, w_e3, w_e2):
    p = dict(w_ada=w_ada, b_ada=b_ada, norm_mix=norm_mix, norm_ffn=norm_ffn, norm_out=norm_out,
             w_ab_in=w_ab_in, w_gk2=w_gk2, b_gk=b_gk, gla_norm=gla_norm, conv_w=conv_w, conv_b=conv_b,
             lru_wa=lru_wa, lru_ba=lru_ba, lru_wi=lru_wi, lru_bi=lru_bi, lru_lam=lru_lam, w_ab_out=w_ab_out,
             w_sb_qkv=w_sb_qkv, w_sb_out=w_sb_out, w_router=w_router, b_router=b_router,
             w_e1=w_e1, w_e3=w_e3, w_e2=w_e2, sb_heads=cache_k.shape[3])
    past = dict(state_gla=state_gla, state_lru=state_lru, state_conv=state_conv, cache_k=cache_k, cache_v=cache_v)
    y_prompt, (p_gla, p_lru, p_conv, p_k, p_v) = _trunk(x_prompt, c_prompt, p, None)
    y_sample, (s_gla, s_lru, s_conv, s_k, s_v) = _trunk(x_sample, c_sample, p, past)
    return (y_prompt, y_sample, p_gla, p_lru, p_conv, p_k, p_v, s_gla, s_lru, s_conv, s_k, s_v)
```

```python
import functools

import jax
import jax.numpy as jnp
from jax import lax
from jax.experimental import pallas as pl
from jax.experimental.pallas import tpu as pltpu

F32 = jnp.float32
BF16 = jnp.bfloat16
I32 = jnp.int32

EPS = 1e-6
N_MOD = 6
GLA_HEADS = 4
GLA_GATE_TAU = 16.0
LRU_C = 8.0
CONV_WIDTH = 4
N_GROUPS = 4
EXPERTS_PER_GROUP = 4
N_PAIRS = 6
N_CLASSES = N_GROUPS * N_PAIRS

LANES = 128
SUBLANES = 8
VMEM_LIMIT = 56 * 1024 * 1024

ROW_TILE = 512
MIX_TILE = 256
GLA_CHUNK = 64
SB_BLOCK = 256
SB_GROUP_LANES = 256
SB_PROMPT_WIDTH = 512
MOE_BLOCK = 256
MOE_BLOCK_SMALL = 64


def _moe_block(n_tokens):
    return MOE_BLOCK if n_tokens >= N_CLASSES * MOE_BLOCK else MOE_BLOCK_SMALL
GATHER_TILE = 256


def _cparams(*sem):
    return pltpu.CompilerParams(dimension_semantics=sem, vmem_limit_bytes=VMEM_LIMIT)


def _log_sigmoid(z):
    return jnp.minimum(z, 0.0) - jnp.log(1.0 + jnp.exp(-jnp.abs(z)))


def _softplus(z):
    return jnp.maximum(z, 0.0) + jnp.log1p(jnp.exp(-jnp.abs(z)))


def _sigmoid(z):
    return 1.0 / (1.0 + jnp.exp(-z))


def _rms_mod(x, gain, shift, scale):
    ms = jnp.mean(x * x, axis=-1, keepdims=True)
    y = x * lax.rsqrt(ms + EPS) * gain
    return y * (1.0 + scale) + shift


def _row_tile(n_rows, t_len, target):
    if t_len % target == 0:
        return target
    return n_rows


def _mod_operand(mod, n_rows, t_len, tile):
    d = mod.shape[-1]
    if t_len % tile == 0:
        per_seq = t_len // tile
        return mod[:, None, :], (lambda nidx: pl.BlockSpec((None, 1, d), lambda *i: (i[nidx] // per_seq, 0, 0)))
    rows = jnp.repeat(mod, t_len, axis=0)
    return rows, (lambda nidx: pl.BlockSpec((tile, d), lambda *i: (i[nidx], 0)))


def _ada_kernel(c_ref, w_ref, b_ref, o_ref):
    c = c_ref[...]
    cond = c * _sigmoid(c)
    o_ref[...] = jnp.dot(cond.astype(BF16), w_ref[...].astype(BF16), preferred_element_type=F32) + b_ref[...]


def _ada(c, w_ada, b_ada):
    depth, d, e = w_ada.shape
    b = c.shape[0]
    tn = d
    return pl.pallas_call(
        _ada_kernel,
        grid=(depth, e // tn),
        in_specs=[pl.BlockSpec((b, d), lambda l, j: (0, 0)),
                  pl.BlockSpec((None, d, tn), lambda l, j: (l, 0, j)),
                  pl.BlockSpec((None, 1, tn), lambda l, j: (l, 0, j))],
        out_specs=pl.BlockSpec((None, b, tn), lambda l, j: (l, 0, j)),
        out_shape=jax.ShapeDtypeStruct((depth, b, e), F32),
        compiler_params=_cparams("parallel", "parallel"),
        name="ada_mod",
    )(c, w_ada, b_ada[:, None, :])


def _norm_proj_kernel(out_plan, x_ref, gain_ref, sh_ref, sc_ref, *refs):
    n_w = len(out_plan)
    w_refs, o_refs = refs[:n_w], refs[n_w:]
    h = _rms_mod(x_ref[...], gain_ref[...], sh_ref[...], sc_ref[...]).astype(BF16)
    k = 0
    for w_ref, (scale, dtypes) in zip(w_refs, out_plan):
        y = jnp.dot(h, w_ref[...], preferred_element_type=F32)
        if scale != 1.0:
            y = y * scale
        for dt, heads in dtypes:
            out = y.astype(dt)
            o_refs[k][...] = out if heads is None else out.T.reshape(o_refs[k].shape)
            k += 1


def _norm_proj(x, t_len, gain, shift, scale, pieces):
    n, d = x.shape
    tm = _row_tile(n, t_len, ROW_TILE)
    sh_arr, sh_spec = _mod_operand(shift, n, t_len, tm)
    sc_arr, sc_spec = _mod_operand(scale, n, t_len, tm)
    in_specs = [pl.BlockSpec((tm, d), lambda i: (i, 0)),
                pl.BlockSpec((1, d), lambda i: (0, 0)),
                sh_spec(0), sc_spec(0)]
    out_specs, out_shapes, plan, weights = [], [], [], []
    for w, s, dtypes in pieces:
        e = w.shape[1]
        in_specs.append(pl.BlockSpec((d, e), lambda i: (0, 0)))
        weights.append(w)
        plan.append((s, tuple(dtypes)))
        for dt, heads in dtypes:
            if heads is None:
                out_specs.append(pl.BlockSpec((tm, e), lambda i: (i, 0)))
                out_shapes.append(jax.ShapeDtypeStruct((n, e), dt))
            else:
                per_seq = t_len // tm
                out_specs.append(pl.BlockSpec((None, heads, e // heads, tm),
                                              lambda i: (i // per_seq, 0, 0, i % per_seq)))
                out_shapes.append(jax.ShapeDtypeStruct((n // t_len, heads, e // heads, t_len), dt))
    return pl.pallas_call(
        functools.partial(_norm_proj_kernel, tuple(plan)),
        grid=(n // tm,),
        in_specs=in_specs, out_specs=out_specs, out_shape=out_shapes,
        compiler_params=_cparams("parallel"),
        name="norm_proj",
    )(x, gain[None, :], sh_arr, sc_arr, *weights)


def _shift_rows(x, s, fill):
    rows = lax.broadcasted_iota(I32, x.shape, 0)
    return jnp.where(rows >= s, pltpu.roll(x, s, axis=0), fill)


def _cumsum_rows(x):
    n = x.shape[0]
    s = 1
    while s < n:
        x = x + _shift_rows(x, s, 0.0)
        s *= 2
    return x


def _linear_scan_rows(a, b, h_init):
    n, w = a.shape
    a = a.reshape(n // SUBLANES, SUBLANES, w)
    b = b.reshape(n // SUBLANES, SUBLANES, w)
    in_group = lax.broadcasted_iota(I32, a.shape, 1)
    s = 1
    while s < SUBLANES:
        keep = in_group >= s
        b = a * jnp.where(keep, pltpu.roll(b, s, axis=1), 0.0) + b
        a = a * jnp.where(keep, pltpu.roll(a, s, axis=1), 1.0)
        s *= 2
    groups, carry = [], h_init
    for g in range(n // SUBLANES):
        h = a[g] * carry + b[g]
        groups.append(h)
        carry = h[SUBLANES - 1:SUBLANES, :]
    return jnp.concatenate(groups, axis=0)


def _gelu_tanh(x):
    return 0.5 * x * (1.0 + jnp.tanh(0.7978845608028654 * (x + 0.044715 * (x * x * x))))


def _ab_mixer_kernel(first_pos_zero, tt, chunk,
                     qk_ref, v_ref, g_ref, gk_ref, xr_ref, xg_ref, s0_ref, h0_ref, buf0_ref,
                     wgk_ref, bgk_ref, gnorm_ref, cw_ref, cb_ref, wa_ref, wi_ref, ba_ref, bi_ref, lam_ref,
                     mix_ref, s_out_ref, h_out_ref, buf_out_ref,
                     st_ref, hc_ref, xpad_ref):
    ti = pl.program_id(1)
    n_t = pl.num_programs(1)
    dqk = qk_ref.shape[1] // 2
    dv = v_ref.shape[1]
    hk = dqk // GLA_HEADS
    hv = dv // GLA_HEADS
    w_lru = xr_ref.shape[1]

    @pl.when(ti == 0)
    def _():
        st_ref[...] = s0_ref[...]
        hc_ref[...] = h0_ref[...]
        xpad_ref[0:SUBLANES, :] = buf0_ref[...]

    xpad_ref[SUBLANES:SUBLANES + tt, :] = xr_ref[...]
    xc = cb_ref[...]
    for i in range(CONV_WIDTH):
        off = SUBLANES - (CONV_WIDTH - 1) + i
        xc = xc + cw_ref[i:i + 1, :] * xpad_ref[off:off + tt, :]
    tail = xpad_ref[tt:tt + SUBLANES, :]
    xpad_ref[0:SUBLANES, :] = tail
    xc_b = xc.astype(BF16)
    half = w_lru // 2
    r_lin = jnp.concatenate([jnp.dot(xc_b[:, j * half:(j + 1) * half], wa_ref[j], preferred_element_type=F32)
                             for j in range(2)], axis=-1)
    i_lin = jnp.concatenate([jnp.dot(xc_b[:, j * half:(j + 1) * half], wi_ref[j], preferred_element_type=F32)
                             for j in range(2)], axis=-1)
    r = _sigmoid(r_lin + ba_ref[...])
    i_g = _sigmoid(i_lin + bi_ref[...])
    log_at = (-LRU_C) * r * _softplus(-lam_ref[...])
    a = jnp.exp(log_at)
    om = 1.0 - a * a
    mult = jnp.where(om > 0.0, om * lax.rsqrt(om), 0.0)
    if first_pos_zero:
        rows = lax.broadcasted_iota(I32, mult.shape, 0)
        mult = jnp.where((rows == 0) & (ti == 0), 1.0, mult)
    bterm = mult * (i_g * xc)
    hs = _linear_scan_rows(a, bterm, hc_ref[...])
    hc_ref[...] = hs[tt - 1:tt, :]
    y_b = hs * _gelu_tanh(xg_ref[...])
    mix_ref[:, dv:] = y_b.astype(BF16)

    lane_k = lax.broadcasted_iota(I32, (1, dqk), 1)
    st_rows = lax.broadcasted_iota(I32, (dv, dqk), 0)
    st_cols = lax.broadcasted_iota(I32, (dv, dqk), 1)
    st_mask = functools.reduce(
        jnp.logical_or,
        [(st_rows >= h * hv) & (st_rows < (h + 1) * hv) & (st_cols >= h * hk) & (st_cols < (h + 1) * hk)
         for h in range(GLA_HEADS)])
    crow = lax.broadcasted_iota(I32, (chunk, chunk), 0)
    ccol = lax.broadcasted_iota(I32, (chunk, chunk), 1)
    causal = ccol <= crow
    mid = chunk // 2 - 1
    for c in range(tt // chunk):
        rs = slice(c * chunk, (c + 1) * chunk)
        q = qk_ref[rs, 0:dqk] * (hk ** -0.5)
        k = qk_ref[rs, dqk:2 * dqk]
        v_b = v_ref[rs, :].astype(BF16)
        u = jnp.dot(gk_ref[rs, :].astype(BF16), wgk_ref[...], preferred_element_type=F32) + bgk_ref[...]
        b = _cumsum_rows(_log_sigmoid(u) * (1.0 / GLA_GATE_TAU))
        b_mid = b[mid:mid + 1, :]
        b_last = b[chunk - 1:chunk, :]
        q_e = q * jnp.exp(b - b_mid)
        k_e = (k * jnp.exp(b_mid - b)).astype(BF16)
        k_tail = (k * jnp.exp(b_last - b)).astype(BF16)
        q_dec = (q_e * jnp.exp(b_mid)).astype(BF16)
        q_e = q_e.astype(BF16)
        st = st_ref[...]
        o = lax.dot_general(q_dec, st.astype(BF16), (((1,), (1,)), ((), ())), preferred_element_type=F32)
        o_intra = []
        for h in range(GLA_HEADS):
            q_h = jnp.where((lane_k >= h * hk) & (lane_k < (h + 1) * hk), q_e, jnp.zeros_like(q_e))
            att = lax.dot_general(q_h, k_e, (((1,), (1,)), ((), ())), preferred_element_type=F32)
            att = jnp.where(causal, att, 0.0).astype(BF16)
            o_intra.append(jnp.dot(att, v_b[:, h * hv:(h + 1) * hv], preferred_element_type=F32))
        o = o + jnp.concatenate(o_intra, axis=-1)
        kv = lax.dot_general(v_b, k_tail, (((0,), (0,)), ((), ())), preferred_element_type=F32)
        st_ref[...] = st * jnp.exp(b_last) + jnp.where(st_mask, kv, 0.0)
        g = g_ref[rs, :]
        gate = g * _sigmoid(g)
        y_a = []
        for h in range(GLA_HEADS):
            o_h = o[:, h * hv:(h + 1) * hv]
            ms = jnp.mean(o_h * o_h, axis=-1, keepdims=True)
            y_a.append(o_h * lax.rsqrt(ms + EPS) * gnorm_ref[...] * gate[:, h * hv:(h + 1) * hv])
        mix_ref[rs, 0:dv] = jnp.concatenate(y_a, axis=-1).astype(BF16)

    @pl.when(ti == n_t - 1)
    def _():
        s_out_ref[...] = st_ref[...]
        h_out_ref[...] = hc_ref[...]
        buf_out_ref[...] = xpad_ref[0:SUBLANES, :]


def _ab_mixer(proj, bsz, t_len, first_pos_zero, s_gla, h_lru, conv_buf, w_gk2, b_gk, gla_norm, conv_w, conv_b,
              lru_wa, lru_ba, lru_wi, lru_bi, lru_lam):
    qk, v, g, gk, xr, xg = proj
    n = qk.shape[0]
    dqk, dv, w_lru = qk.shape[1] // 2, v.shape[1], xr.shape[1]
    hk, hv = dqk // GLA_HEADS, dv // GLA_HEADS
    tt = MIX_TILE if t_len % MIX_TILE == 0 else t_len
    chunk = GLA_CHUNK if tt % GLA_CHUNK == 0 else tt
    n_t = t_len // tt

    eye = jnp.eye(GLA_HEADS, dtype=F32)
    st0 = jnp.einsum('bhkv,hg->bhvgk', s_gla.astype(F32), eye).reshape(bsz, dv, dqk)
    buf0 = jnp.pad(conv_buf.astype(F32), ((0, 0), (SUBLANES - (CONV_WIDTH - 1), 0), (0, 0)))
    wgk = jnp.pad(w_gk2, ((0, gk.shape[1] - w_gk2.shape[0]), (0, 0))).astype(BF16)
    nb = lru_wa.shape[0]
    half_blocks = nb // 2

    def block_diag(w):
        bd = w.shape[1]
        e = jnp.eye(half_blocks, dtype=w.dtype)
        w2 = w.reshape(2, half_blocks, bd, bd)
        return jnp.einsum('jnio,nm->jnimo', w2, e).reshape(2, half_blocks * bd, half_blocks * bd).astype(BF16)

    row = lambda z: z.astype(F32)[None, :]
    tok = lambda w: pl.BlockSpec((tt, w), lambda b, i: (b * n_t + i, 0))
    per_b = lambda r, c: pl.BlockSpec((None, r, c), lambda b, i: (b, 0, 0))
    full2 = lambda r, c: pl.BlockSpec((r, c), lambda b, i: (0, 0))
    full3 = lambda a, r, c: pl.BlockSpec((a, r, c), lambda b, i: (0, 0, 0))
    mix, st, h_new, buf = pl.pallas_call(
        functools.partial(_ab_mixer_kernel, first_pos_zero, tt, chunk),
        grid=(bsz, n_t),
        in_specs=[tok(2 * dqk), tok(dv), tok(dv), tok(gk.shape[1]), tok(w_lru), tok(w_lru),
                  per_b(dv, dqk), per_b(1, w_lru), per_b(SUBLANES, w_lru),
                  full2(gk.shape[1], dqk), full2(1, dqk), full2(1, hv), full2(CONV_WIDTH, w_lru), full2(1, w_lru),
                  full3(2, w_lru // 2, w_lru // 2), full3(2, w_lru // 2, w_lru // 2),
                  full2(1, w_lru), full2(1, w_lru), full2(1, w_lru)],
        out_specs=[tok(dv + w_lru), per_b(dv, dqk), per_b(1, w_lru), per_b(SUBLANES, w_lru)],
        out_shape=[jax.ShapeDtypeStruct((n, dv + w_lru), BF16),
                   jax.ShapeDtypeStruct((bsz, dv, dqk), F32),
                   jax.ShapeDtypeStruct((bsz, 1, w_lru), F32),
                   jax.ShapeDtypeStruct((bsz, SUBLANES, w_lru), F32)],
        scratch_shapes=[pltpu.VMEM((dv, dqk), F32), pltpu.VMEM((1, w_lru), F32),
                        pltpu.VMEM((tt + SUBLANES, w_lru), F32)],
        compiler_params=_cparams("parallel", "arbitrary"),
        name="ab_mixer",
    )(qk, v, g, gk, xr, xg, st0, h_lru.astype(F32)[:, None, :], buf0,
      wgk, row(b_gk), row(gla_norm), conv_w.astype(F32), row(conv_b),
      block_diag(lru_wa), block_diag(lru_wi), row(lru_ba), row(lru_bi), row(lru_lam))
    st5 = st.reshape(bsz, GLA_HEADS, hv, GLA_HEADS, hk)
    s_new = jnp.stack([st5[:, h, :, h, :] for h in range(GLA_HEADS)], axis=1).swapaxes(-1, -2)
    return mix, s_new, h_new[:, 0], buf[:, SUBLANES - (CONV_WIDTH - 1):]


SB_MASKED = -1e30
SB_DEAD = -152.0
LOG2_E = 1.4426950408889634


def _sb_logits(q_h, k_blk, keys_on_lanes):
    if keys_on_lanes:
        return jnp.dot(q_h, k_blk, preferred_element_type=F32)
    return lax.dot_general(q_h, k_blk, (((1,), (1,)), ((), ())), preferred_element_type=F32)


def _sb_gates(z, mask):
    neg_abs = pltpu.bitcast(pltpu.bitcast(z, I32) | jnp.int32(-2 ** 31), F32)
    ls = jnp.minimum(z, 0.0) - jnp.log2(1.0 + jnp.exp2(neg_abs))
    lk = ls - z
    if mask is not None:
        lk = jnp.where(mask, lk, 0.0)
    return ls, lk.astype(BF16), jnp.sum(lk, axis=-1, keepdims=True)


def _sb_prefix(lk, upper, suffix, mask):
    pre = jnp.dot(lk, upper, preferred_element_type=F32) + suffix
    if mask is not None:
        pre = jnp.where(mask, pre, SB_MASKED)
    return pre


def _sb_values(ls, pre, v_blk, keys_on_lanes):
    w = jnp.exp2(ls + pre).astype(BF16)
    if keys_on_lanes:
        return lax.dot_general(w, v_blk, (((1,), (1,)), ((), ())), preferred_element_type=F32)
    return jnp.dot(w, v_blk, preferred_element_type=F32)


def _split_heads(q, dh):
    lane = lax.broadcasted_iota(I32, (1, q.shape[1]), 1)
    zero = jnp.zeros_like(q)
    return [jnp.where((lane >= h * dh) & (lane < (h + 1) * dh), q, zero) for h in range(q.shape[1] // dh)], lane


def _merge_heads(acc, lane, dh):
    out = acc[0]
    for h in range(1, len(acc)):
        out = jnp.where(lane >= h * dh, acc[h], out)
    return out


def _upper(n):
    r = lax.broadcasted_iota(I32, (n, n), 0)
    c = lax.broadcasted_iota(I32, (n, n), 1)
    return jnp.where(r > c, 1.0, 0.0).astype(BF16)


def _group_queries(q, lanes, dh):
    groups, lane = [], None
    for g in range(q.shape[1] // lanes):
        heads_g, lane = _split_heads(q[:, g * lanes:(g + 1) * lanes], dh)
        groups.append(heads_g)
    return groups, lane


def _sb_block(q_groups, lanes, k_rows, v_rows, upper, mask, acc_ref, suf_ref, keys_on_lanes=False):
    per_group = len(q_groups[0])
    for g, heads_g in enumerate(q_groups):
        cols = slice(g * lanes, (g + 1) * lanes)
        k_blk = k_rows[cols, :] if keys_on_lanes else k_rows[:, cols]
        v_blk = v_rows[cols, :] if keys_on_lanes else v_rows[:, cols]
        for h, q_h in enumerate(heads_g):
            i = g * per_group + h
            ls, lk, tot = _sb_gates(_sb_logits(q_h, k_blk, keys_on_lanes), mask)
            acc_ref[i] += _sb_values(ls, _sb_prefix(lk, upper, suf_ref[i], mask), v_blk, keys_on_lanes)
            suf_ref[i] += tot


def _sb_live(suf_ref):
    return (jnp.max(suf_ref[...]) > SB_DEAD).astype(I32)


def _sb_output(acc_ref, q_groups, lane, dh):
    per_group = len(q_groups[0])
    return jnp.concatenate([_merge_heads([acc_ref[g * per_group + h] for h in range(per_group)], lane, dh)
                            for g in range(len(q_groups))], axis=-1)


def _sb_prompt_kernel(blk, dh, lanes, q_ref, k_ref, v_ref, o_ref, acc_ref, suf_ref):
    qi = pl.program_id(2)
    n_blk = qi + 1
    q_groups, lane = _group_queries(q_ref[...], lanes, dh)
    upper = _upper(blk)
    diag = lax.broadcasted_iota(I32, (blk, blk), 1) < lax.broadcasted_iota(I32, (blk, blk), 0)

    def block(s, mask):
        rows = pl.ds(pl.multiple_of((qi - s) * blk, blk), blk)
        _sb_block(q_groups, lanes, k_ref[rows, :], v_ref[rows, :], upper, mask, acc_ref, suf_ref)

    acc_ref[...] = jnp.zeros_like(acc_ref)
    suf_ref[...] = jnp.zeros_like(suf_ref)
    block(0, diag)

    def more(carry):
        s, go = carry
        return (s < n_blk) & (go > 0)

    def step(carry):
        s, _ = carry
        block(s, None)
        return s + 1, _sb_live(suf_ref)

    lax.while_loop(more, step, (jnp.int32(1), _sb_live(suf_ref)))
    o_ref[...] = _sb_output(acc_ref, q_groups, lane, dh).astype(o_ref.dtype)


def _sb_prompt(q, k, v, bsz, t_len, dh):
    n, width = q.shape
    blk = SB_BLOCK
    nq = t_len // blk
    step_w = SB_PROMPT_WIDTH
    lanes = SB_GROUP_LANES
    heads = step_w // dh
    return pl.pallas_call(
        functools.partial(_sb_prompt_kernel, blk, dh, lanes),
        grid=(bsz, width // step_w, nq),
        in_specs=[pl.BlockSpec((blk, step_w), lambda b, g, i: (b * nq + i, g)),
                  pl.BlockSpec((t_len, step_w), lambda b, g, i: (b, g)),
                  pl.BlockSpec((t_len, step_w), lambda b, g, i: (b, g))],
        out_specs=pl.BlockSpec((blk, step_w), lambda b, g, i: (b * nq + i, g)),
        out_shape=jax.ShapeDtypeStruct((n, width), BF16),
        scratch_shapes=[pltpu.VMEM((heads, blk, lanes), F32), pltpu.VMEM((heads, blk, 1), F32)],
        compiler_params=_cparams("parallel", "parallel", "arbitrary"),
        name="sb_prompt",
    )(q, k, v)


def _sb_sample_kernel(t_new, past_len, blk, dh, lanes, q_ref, k_ref, v_ref, pk_ref, pv_ref, o_ref, acc_ref, suf_ref):
    width = q_ref.shape[1]
    q_groups, lane = _group_queries(q_ref[...], lanes, dh)
    diag = lax.broadcasted_iota(I32, (t_new, t_new), 1) < lax.broadcasted_iota(I32, (t_new, t_new), 0)
    acc_ref[...] = jnp.zeros_like(acc_ref)
    suf_ref[...] = jnp.zeros_like(suf_ref)
    _sb_block(q_groups, lanes, k_ref[...], v_ref[...], _upper(t_new), diag, acc_ref, suf_ref)
    upper = _upper(blk)

    def more(carry):
        j, go = carry
        return (j >= 0) & (go > 0)

    def step(carry):
        j, _ = carry
        keys = pl.ds(pl.multiple_of(j * blk, blk), blk)
        _sb_block(q_groups, lanes, pk_ref[:, :, keys].reshape(width, blk).astype(BF16),
                  pv_ref[:, :, keys].reshape(width, blk).astype(BF16), upper, None, acc_ref, suf_ref,
                  keys_on_lanes=True)
        return j - 1, _sb_live(suf_ref)

    lax.while_loop(more, step, (jnp.int32(past_len // blk - 1), _sb_live(suf_ref)))
    o_ref[...] = _sb_output(acc_ref, q_groups, lane, dh).astype(o_ref.dtype)


def _sb_sample(q, k, v, past_k, past_v, bsz, t_len):
    n, width = q.shape
    _, heads, dh, past_len = past_k.shape
    blk = SB_BLOCK if past_len % SB_BLOCK == 0 else past_len
    lanes = SB_GROUP_LANES
    new = pl.BlockSpec((t_len, width), lambda b: (b, 0))
    past = pl.BlockSpec((None, heads, dh, past_len), lambda b: (b, 0, 0, 0))
    return pl.pallas_call(
        functools.partial(_sb_sample_kernel, t_len, past_len, blk, dh, lanes),
        grid=(bsz,),
        in_specs=[new, new, new, past, past],
        out_specs=new,
        out_shape=jax.ShapeDtypeStruct((n, width), BF16),
        scratch_shapes=[pltpu.VMEM((heads, t_len, lanes), F32), pltpu.VMEM((heads, t_len, 1), F32)],
        compiler_params=_cparams("parallel"),
        name="sb_sample",
    )(q, k, v, past_k, past_v)


def _first_argmax(vals):
    best_v, best_i = vals[0], jnp.zeros(vals[0].shape, I32)
    for i in range(1, len(vals)):
        better = vals[i] > best_v
        best_v = jnp.where(better, vals[i], best_v)
        best_i = jnp.where(better, i, best_i)
    return best_v, best_i


def _route_rows(logits_t, bias_col):
    s = _sigmoid(logits_t)
    sel = s + bias_col
    epg = EXPERTS_PER_GROUP
    scores = []
    for gi in range(N_GROUPS):
        rows = [sel[gi * epg + i:gi * epg + i + 1, :] for i in range(epg)]
        pair_sums = [rows[i] + rows[j] for i in range(epg) for j in range(i + 1, epg)]
        scores.append(functools.reduce(jnp.maximum, pair_sums))
    _, best = _first_argmax(scores)

    def in_best(mat, i):
        out = mat[i:i + 1, :]
        for gi in range(1, N_GROUPS):
            out = jnp.where(best == gi, mat[gi * epg + i:gi * epg + i + 1, :], out)
        return out

    sel_g = [in_best(sel, i) for i in range(epg)]
    s_g = [in_best(s, i) for i in range(epg)]
    _, i1 = _first_argmax(sel_g)
    _, i2 = _first_argmax([jnp.where(i1 == i, -jnp.inf, sel_g[i]) for i in range(epg)])
    lo = jnp.minimum(i1, i2)
    hi = jnp.maximum(i1, i2)
    pick = lambda idx: functools.reduce(lambda acc, i: jnp.where(idx == i, s_g[i], acc), range(1, epg), s_g[0])
    w_lo, w_hi = pick(lo), pick(hi)
    tot = w_lo + w_hi
    pair = jnp.where(lo == 0, 0, jnp.where(lo == 1, epg - 1, 2 * epg - 3)) + (hi - lo - 1)
    return best * N_PAIRS + pair, w_lo / tot, w_hi / tot


def _proj_route_kernel(a_ref, w_ref, x_ref, g1_ref, gain_ref, sh_ref, sc_ref, wr_hi_ref, wr_lo_ref, br_ref,
                       xn_ref, h2_ref, ri_ref):
    y = jnp.dot(a_ref[...], w_ref[...], preferred_element_type=F32)
    xn = x_ref[...] + g1_ref[...] * y
    xn_ref[...] = xn
    d = xn.shape[1]
    h2 = _rms_mod(xn, gain_ref[...], sh_ref[...], sc_ref[...])
    h2_ref[:, 0:d] = h2
    h_hi = h2.astype(BF16)
    h_lo = (h2 - h_hi.astype(F32)).astype(BF16)
    logits = (jnp.dot(h_hi, wr_hi_ref[...], preferred_element_type=F32)
              + jnp.dot(h_lo, wr_hi_ref[...], preferred_element_type=F32)
              + jnp.dot(h_hi, wr_lo_ref[...], preferred_element_type=F32))
    n_e = br_ref.shape[0]
    logits_t = logits.T[0:n_e, :]
    cls, w_lo, w_hi = _route_rows(logits_t, br_ref[...])
    tm = logits_t.shape[1]
    h2_ref[:, d:] = jnp.concatenate([w_lo, w_hi, jnp.zeros((LANES - 2, tm), F32)], axis=0).T
    ri_ref[...] = jnp.concatenate([cls, jnp.zeros((SUBLANES - 1, tm), I32)], axis=0)


def _proj_route(a, w_out, x, t_len, g1, gain, shift, scale, w_router, b_router):
    n, d = x.shape
    k = a.shape[1]
    n_e = w_router.shape[1]
    tm = _row_tile(n, t_len, ROW_TILE)
    g1_arr, g1_spec = _mod_operand(g1, n, t_len, tm)
    sh_arr, sh_spec = _mod_operand(shift, n, t_len, tm)
    sc_arr, sc_spec = _mod_operand(scale, n, t_len, tm)
    wr = jnp.pad(w_router.astype(F32), ((0, 0), (0, LANES - n_e)))
    wr_hi = wr.astype(BF16)
    wr_lo = (wr - wr_hi.astype(F32)).astype(BF16)
    tokd = pl.BlockSpec((tm, d), lambda i: (i, 0))
    const = lambda r, c: pl.BlockSpec((r, c), lambda i: (0, 0))
    route = pl.BlockSpec((SUBLANES, tm), lambda i: (0, i))
    return pl.pallas_call(
        _proj_route_kernel,
        grid=(n // tm,),
        in_specs=[pl.BlockSpec((tm, k), lambda i: (i, 0)), const(k, d), tokd, g1_spec(0), const(1, d),
                  sh_spec(0), sc_spec(0), const(d, LANES), const(d, LANES), const(n_e, 1)],
        out_specs=[tokd, pl.BlockSpec((tm, d + LANES), lambda i: (i, 0)), route],
        out_shape=[jax.ShapeDtypeStruct((n, d), F32), jax.ShapeDtypeStruct((n, d + LANES), F32),
                   jax.ShapeDtypeStruct((SUBLANES, n), I32)],
        compiler_params=_cparams("parallel"),
        name="proj_route",
    )(a, w_out, x, g1_arr, gain[None, :], sh_arr, sc_arr, wr_hi, wr_lo, b_router.astype(F32)[:, None])


def _gather_rows(src_hbm, idx_ref, base, n_rows, dst, sem):
    last = idx_ref.shape[0] - 1

    for r in range(n_rows):
        row = idx_ref[jnp.minimum(base + r, last)]
        pltpu.make_async_copy(src_hbm.at[pl.ds(row, 1)], dst.at[pl.ds(r, 1)], sem).start(priority=r % 2)


def _wait_rows(src_hbm, n_rows, dst, sem):
    pltpu.make_async_copy(src_hbm.at[pl.ds(0, n_rows)], dst, sem).wait()


def _moe_kernel(bm, d, order_ref, base_ref, valid_ref, e_lo_ref, e_hi_ref, used_ref,
                h_hbm, w1a_ref, w3a_ref, w2a_ref, w1b_ref, w3b_ref, w2b_ref,
                y_ref, buf0_ref, buf1_ref, sem_ref):
    i = pl.program_id(0)
    n_used = used_ref[0]

    @pl.when((i == 0) & (n_used > 0))
    def _():
        _gather_rows(h_hbm, order_ref, base_ref[0], bm, buf0_ref, sem_ref.at[0])

    def block(cur, cur_sem, nxt, nxt_sem):
        _wait_rows(h_hbm, bm, cur, cur_sem)
        xb = cur[:, 0:d].astype(BF16)
        live = lax.broadcasted_iota(I32, (bm, LANES), 0) < valid_ref[i]
        wts = jnp.where(live, cur[:, d:], 0.0)
        _gather_rows(h_hbm, order_ref, base_ref[i + 1], bm, nxt, nxt_sem)

        def expert(w1_ref, w3_ref, w2_ref):
            u = jnp.dot(xb, w1_ref[...], preferred_element_type=F32)
            t = jnp.dot(xb, w3_ref[...], preferred_element_type=F32)
            mid = (u * _sigmoid(u) * t).astype(BF16)
            return jnp.dot(mid, w2_ref[...], preferred_element_type=F32)

        y_ref[...] = (expert(w1a_ref, w3a_ref, w2a_ref) * wts[:, 0:1]
                      + expert(w1b_ref, w3b_ref, w2b_ref) * wts[:, 1:2])

        @pl.when(i == n_used - 1)
        def _():
            _wait_rows(h_hbm, bm, nxt, nxt_sem)

    @pl.when((i < n_used) & (i % 2 == 0))
    def _():
        block(buf0_ref, sem_ref.at[0], buf1_ref, sem_ref.at[1])

    @pl.when((i < n_used) & (i % 2 == 1))
    def _():
        block(buf1_ref, sem_ref.at[1], buf0_ref, sem_ref.at[0])

    @pl.when(i >= n_used)
    def _():
        y_ref[...] = jnp.zeros_like(y_ref)


def _moe_sorted(h2w, order, blk_base, blk_valid, blk_lo, blk_hi, n_used, w1, w3, w2):
    n, dw = h2w.shape
    d = dw - LANES
    n_blocks = blk_valid.shape[0]
    bm = _moe_block(n)
    de = w1.shape[2]

    def wspec(which, r, c):
        def index(i, order, base, valid, lo, hi, used):
            blk = jnp.minimum(i, jnp.maximum(used[0] - 1, 0))
            return ((lo, hi)[which][blk], 0, 0)
        return pl.BlockSpec((None, r, c), index)

    return pl.pallas_call(
        functools.partial(_moe_kernel, bm, d),
        grid_spec=pltpu.PrefetchScalarGridSpec(
            num_scalar_prefetch=6,
            grid=(n_blocks,),
            in_specs=[pl.BlockSpec(memory_space=pl.ANY),
                      wspec(0, d, de), wspec(0, d, de), wspec(0, de, d),
                      wspec(1, d, de), wspec(1, d, de), wspec(1, de, d)],
            out_specs=pl.BlockSpec((bm, d), lambda i, *_: (i, 0)),
            scratch_shapes=[pltpu.VMEM((bm, dw), F32), pltpu.VMEM((bm, dw), F32), pltpu.SemaphoreType.DMA((2,))]),
        out_shape=jax.ShapeDtypeStruct((n_blocks * bm, d), F32),
        compiler_params=_cparams("arbitrary"),
        name="moe_sorted",
    )(order, blk_base, blk_valid, blk_lo, blk_hi, n_used, h2w, w1, w3, w2, w1, w3, w2)


def _unsort_kernel(tm, final_norm, pos_ref, y_hbm, x_ref, g2_ref, gain_ref, o_ref, buf0_ref, buf1_ref, sem_ref):
    i = pl.program_id(0)
    n_i = pl.num_programs(0)

    @pl.when(i == 0)
    def _():
        _gather_rows(y_hbm, pos_ref, 0, tm, buf0_ref, sem_ref.at[0])

    def step(cur, cur_sem, nxt, nxt_sem):
        @pl.when(i + 1 < n_i)
        def _():
            _gather_rows(y_hbm, pos_ref, (i + 1) * tm, tm, nxt, nxt_sem)

        _wait_rows(y_hbm, tm, cur, cur_sem)
        x = x_ref[...] + g2_ref[...] * cur[...]
        if final_norm:
            ms = jnp.mean(x * x, axis=-1, keepdims=True)
            x = x * lax.rsqrt(ms + EPS) * gain_ref[...]
        o_ref[...] = x

    @pl.when(i % 2 == 0)
    def _():
        step(buf0_ref, sem_ref.at[0], buf1_ref, sem_ref.at[1])

    @pl.when(i % 2 == 1)
    def _():
        step(buf1_ref, sem_ref.at[1], buf0_ref, sem_ref.at[0])


def _unsort_resid(y_sorted, pos, x, t_len, g2, final_gain):
    n, d = x.shape
    tm = _row_tile(n, t_len, GATHER_TILE)
    g2_arr, g2_spec = _mod_operand(g2, n, t_len, tm)
    final_norm = final_gain is not None
    gain = (final_gain if final_norm else jnp.ones((d,), F32)).astype(F32)[None, :]
    return pl.pallas_call(
        functools.partial(_unsort_kernel, tm, final_norm),
        grid_spec=pltpu.PrefetchScalarGridSpec(
            num_scalar_prefetch=1,
            grid=(n // tm,),
            in_specs=[pl.BlockSpec(memory_space=pl.ANY),
                      pl.BlockSpec((tm, d), lambda i, pos: (i, 0)),
                      g2_spec(0),
                      pl.BlockSpec((1, d), lambda i, pos: (0, 0))],
            out_specs=pl.BlockSpec((tm, d), lambda i, pos: (i, 0)),
            scratch_shapes=[pltpu.VMEM((tm, d), F32), pltpu.VMEM((tm, d), F32), pltpu.SemaphoreType.DMA((2,))]),
        out_shape=jax.ShapeDtypeStruct((n, d), F32),
        compiler_params=_cparams("arbitrary"),
        name="unsort_resid",
    )(pos, y_sorted, x, g2_arr, gain)


def _sort_plan(cls):
    n = cls.shape[0]
    bm = _moe_block(n)
    n_blocks = -(-n // bm) + N_CLASSES
    classes = jnp.arange(N_CLASSES, dtype=I32)
    cls_sorted, order = lax.sort((cls, jnp.arange(n, dtype=I32)), num_keys=1, is_stable=True)
    counts = jnp.sum((cls[:, None] == classes[None, :]).astype(I32), axis=0)
    padded = (counts + bm - 1) // bm * bm
    pad_end = jnp.cumsum(padded)
    pad_start = pad_end - padded
    start = jnp.cumsum(counts) - counts
    shift = jnp.sum(jnp.where(cls_sorted[:, None] == classes[None, :], (pad_start - start)[None, :], 0), axis=1)
    dest = jnp.arange(n, dtype=I32) + shift
    _, pos = lax.sort((order, dest), num_keys=1)
    blk_first = jnp.arange(n_blocks + 1, dtype=I32) * bm
    blk_cls = jnp.minimum(jnp.sum((pad_end[None, :] <= blk_first[:, None]).astype(I32), axis=1), N_CLASSES - 1)
    onehot = blk_cls[:, None] == classes[None, :]
    pick = lambda table: jnp.sum(jnp.where(onehot, table[None, :], 0), axis=1)
    into = blk_first - pick(pad_start)
    blk_base = jnp.clip(pick(start) + into, 0, n)
    blk_valid = jnp.clip(pick(counts) - into, 0, bm)[:n_blocks]
    pair_lo = jnp.array([i for i in range(EXPERTS_PER_GROUP) for j in range(i + 1, EXPERTS_PER_GROUP)], I32)
    pair_hi = jnp.array([j for i in range(EXPERTS_PER_GROUP) for j in range(i + 1, EXPERTS_PER_GROUP)], I32)
    group = classes // N_PAIRS
    blk_lo = pick(group * EXPERTS_PER_GROUP + pair_lo[classes % N_PAIRS])
    blk_hi = pick(group * EXPERTS_PER_GROUP + pair_hi[classes % N_PAIRS])
    n_used = (pad_end[-1] // bm).astype(I32)[None]
    return order, pos, blk_base.astype(I32), blk_valid.astype(I32), blk_lo, blk_hi, n_used


def _moe_layer(h2w, route_i, x, t_len, g2, w1, w3, w2, final_gain):
    order, pos, blk_base, blk_valid, blk_lo, blk_hi, n_used = _sort_plan(route_i[0])
    y_sorted = _moe_sorted(h2w, order, blk_base, blk_valid, blk_lo, blk_hi, n_used, w1, w3, w2)
    return _unsort_resid(y_sorted, pos, x, t_len, g2, final_gain)


def _trunk(x3, c, p, past):
    bsz, t_len, d = x3.shape
    n = bsz * t_len
    x = x3.reshape(n, d)
    depth = p['w_ada'].shape[0]
    mod = _ada(c.astype(F32), p['w_ada'], p['b_ada']).reshape(depth, bsz, N_MOD, d)
    gla_s, lru_s, conv_s, ks, vs = [], [], [], [], []
    for l in range(depth):
        sh1, sc1, g1, sh2, sc2, g2 = [mod[l, :, i] for i in range(N_MOD)]
        j = l // 2
        if l % 2 == 0:
            w_in = p['w_ab_in'][j]
            dqk2 = 2 * p['w_gk2'].shape[2]
            dv = GLA_HEADS * p['gla_norm'].shape[1]
            rank = p['w_gk2'].shape[1]
            w_lru = p['lru_lam'].shape[1]
            cuts = [0, dqk2, dqk2 + dv, dqk2 + 2 * dv, dqk2 + 2 * dv + rank,
                    dqk2 + 2 * dv + rank + w_lru, dqk2 + 2 * dv + rank + 2 * w_lru]
            cols = [w_in[:, cuts[i]:cuts[i + 1]] for i in range(6)]
            cols[3] = jnp.pad(cols[3], ((0, 0), (0, LANES - rank)))
            proj = _norm_proj(x, t_len, p['norm_mix'][l], sh1, sc1,
                              [(w.astype(BF16), 1.0, [(F32, None)]) for w in cols])
            if past is None:
                s0 = jnp.zeros((bsz, GLA_HEADS, dqk2 // 2 // GLA_HEADS, dv // GLA_HEADS), F32)
                h0 = jnp.zeros((bsz, w_lru), F32)
                buf = jnp.zeros((bsz, CONV_WIDTH - 1, w_lru), F32)
            else:
                s0, h0, buf = past['state_gla'][j], past['state_lru'][j], past['state_conv'][j]
            mix, sg, sl, sc = _ab_mixer(proj, bsz, t_len, past is None, s0, h0, buf, p['w_gk2'][j], p['b_gk'][j],
                                        p['gla_norm'][j], p['conv_w'][j], p['conv_b'][j], p['lru_wa'][j],
                                        p['lru_ba'][j], p['lru_wi'][j], p['lru_bi'][j], p['lru_lam'][j])
            gla_s.append(sg)
            lru_s.append(sl)
            conv_s.append(sc)
            w_out = p['w_ab_out'][j]
        else:
            w_qkv = p['w_sb_qkv'][j]
            width = w_qkv.shape[1] // 3
            heads = past['cache_k'].shape[3] if past is not None else p['sb_heads']
            dh = width // heads
            split = heads if t_len % ROW_TILE == 0 else None
            pieces = [(w_qkv[:, 0:width].astype(BF16), LOG2_E * dh ** -0.5, [(BF16, None)]),
                      (w_qkv[:, width:2 * width].astype(BF16), 1.0, [(F32, split), (BF16, None)]),
                      (w_qkv[:, 2 * width:].astype(BF16), 1.0, [(F32, split), (BF16, None)])]
            q_b, k_f, k_b, v_f, v_b = _norm_proj(x, t_len, p['norm_mix'][l], sh1, sc1, pieces)
            if past is None:
                mix = _sb_prompt(q_b, k_b, v_b, bsz, t_len, dh)
            else:
                time_minor = lambda c: c.astype(F32).transpose(0, 2, 3, 1)
                mix = _sb_sample(q_b, k_b, v_b, time_minor(past['cache_k'][j]), time_minor(past['cache_v'][j]),
                                 bsz, t_len)
            rows = lambda z: z.transpose(0, 3, 1, 2) if split else z.reshape(bsz, t_len, heads, dh)
            ks.append(rows(k_f))
            vs.append(rows(v_f))
            w_out = p['w_sb_out'][j]
        xn, h2w, route_i = _proj_route(mix, w_out.astype(BF16), x, t_len, g1, p['norm_ffn'][l], sh2, sc2,
                                       p['w_router'], p['b_router'])
        x = _moe_layer(h2w, route_i, xn, t_len, g2,
                       p['w_e1'][l].astype(BF16), p['w_e3'][l].astype(BF16), p['w_e2'][l].astype(BF16),
                       p['norm_out'] if l == depth - 1 else None)
    stack = lambda xs: xs[0][None] if len(xs) == 1 else jnp.stack(xs)
    return x.reshape(bsz, t_len, d), (stack(gla_s), stack(lru_s), stack(conv_s), stack(ks), stack(vs))


def kernel(x_prompt, x_sample, state_gla, state_lru, state_conv, cache_k, cache_v, c_prompt, c_sample,
           w_ada, b_ada, norm_mix, norm_ffn, norm_out, w_ab_in, w_gk2, b_gk, gla_norm, conv_w, conv_b,
           lru_wa, lru_ba, lru_wi, lru_bi, lru_lam, w_ab_out, w_sb_qkv, w_sb_out, w_router, b_router,
           w_e1, w_e3, w_e2):
    p = dict(w_ada=w_ada, b_ada=b_ada, norm_mix=norm_mix, norm_ffn=norm_ffn, norm_out=norm_out,
             w_ab_in=w_ab_in, w_gk2=w_gk2, b_gk=b_gk, gla_norm=gla_norm, conv_w=conv_w, conv_b=conv_b,
             lru_wa=lru_wa, lru_ba=lru_ba, lru_wi=lru_wi, lru_bi=lru_bi, lru_lam=lru_lam, w_ab_out=w_ab_out,
             w_sb_qkv=w_sb_qkv, w_sb_out=w_sb_out, w_router=w_router, b_router=b_router,
             w_e1=w_e1, w_e3=w_e3, w_e2=w_e2, sb_heads=cache_k.shape[3])
    past = dict(state_gla=state_gla, state_lru=state_lru, state_conv=state_conv, cache_k=cache_k, cache_v=cache_v)
    y_prompt, (p_gla, p_lru, p_conv, p_k, p_v) = _trunk(x_prompt, c_prompt, p, None)
    y_sample, (s_gla, s_lru, s_conv, s_k, s_v) = _trunk(x_sample, c_sample, p, past)
    return (y_prompt, y_sample, p_gla, p_lru, p_conv, p_k, p_v, s_gla, s_lru, s_conv, s_k, s_v)
```
